```python
import math
import jax
import jax.numpy as jnp
from jax import lax
import numpy as np

D_MODEL = 1024
BATCH = 32
SEQ = 256
DEPTH = 2
DEC_BATCH = 4
DEC_SEQ = 4096
PAST_LEN = 256

GRID_W = 64
HEAD_DIM = 64
N_MIXERS = 4
GROUP_W = D_MODEL // N_MIXERS
MIX_W = N_MIXERS * GROUP_W
CONV_K = 3
EPS = 1e-6
ROPE_THETA = 10000.0
Q_BLOCK = 128
NEG_INF = -1e30

DN_HEADS = GROUP_W // HEAD_DIM
DN_DK = HEAD_DIM
DN_DV = HEAD_DIM
DN_CHUNK = 64

MLA_HEADS = GROUP_W // HEAD_DIM
MLA_NOPE = 64
MLA_ROPE = 32
MLA_VHD = GROUP_W // MLA_HEADS
MLA_Q_LORA = D_MODEL // 4
MLA_KV_LORA = D_MODEL // 8
MLA_SCALE = (MLA_NOPE + MLA_ROPE) ** -0.5

SSM_HEADS = GROUP_W // 64
SSM_P = GROUP_W // SSM_HEADS
SSM_N = 64
SSM_GROUPS = 2
SSM_CHUNK = 64

SWA_HEADS = GROUP_W // HEAD_DIM
SWA_KV_HEADS = 2
SWA_GQA = SWA_HEADS // SWA_KV_HEADS
SWA_WINDOW = 128
SWA_BLOCK = 128
SWA_SCALE = HEAD_DIM ** -0.5

FF_DIM = -(-(8 * D_MODEL) // (3 * 256)) * 256

DN_IN = 4 * GROUP_W + 4 * DN_HEADS
MLA_IN = MLA_Q_LORA + MLA_KV_LORA + MLA_ROPE
SSM_IN = 2 * GROUP_W + 2 * SSM_GROUPS * SSM_N + 2 * SSM_HEADS
SWA_IN = (SWA_HEADS + 2 * SWA_KV_HEADS) * HEAD_DIM
IN_DIM = DN_IN + MLA_IN + SSM_IN + SWA_IN

kernel_name = 'hybrid_flow_backbone_step'


def split_cols(x, sizes):
    idx = [int(s) for s in np.cumsum(sizes)[:-1]]
    return jnp.split(x, idx, axis=-1)


def rms_norm(x, w):
    xf = x.astype(jnp.float32)
    y = xf * lax.rsqrt(jnp.mean(xf * xf, axis=-1, keepdims=True) + EPS)
    return (y * w.astype(jnp.float32)).astype(x.dtype)


def l2_normalize(x):
    xf = x.astype(jnp.float32)
    return (xf * lax.rsqrt(jnp.sum(xf * xf, axis=-1, keepdims=True) + EPS)).astype(x.dtype)


def modulated_norm(x, w, shift, scale):
    return rms_norm(x, w) * (1 + scale) + shift


def swiglu(h, w_gate_up, w_down):
    gu = h @ w_gate_up
    return (jax.nn.silu(gu[..., :FF_DIM]) * gu[..., FF_DIM:]) @ w_down


def centred_depthwise_conv(u, w):
    pad = (w.shape[0] - 1) // 2
    return lax.conv_general_dilated(u, w[:, None, :].astype(u.dtype), window_strides=(1,),
                                    padding=[(pad, pad)], dimension_numbers=('NWC', 'WIO', 'NWC'),
                                    feature_group_count=u.shape[-1])


def axial_rope(rows, rot_dim):
    row_ids = jnp.broadcast_to(jnp.arange(rows)[:, None], (rows, GRID_W)).reshape(-1).astype(jnp.float32)
    col_ids = jnp.broadcast_to(jnp.arange(GRID_W)[None, :], (rows, GRID_W)).reshape(-1).astype(jnp.float32)
    n_freq = rot_dim // 4
    inv_freq = ROPE_THETA ** (-jnp.arange(n_freq, dtype=jnp.float32) / n_freq)
    ang = jnp.concatenate([row_ids[:, None] * inv_freq, col_ids[:, None] * inv_freq], axis=-1)
    return jnp.cos(ang), jnp.sin(ang)


def apply_rope(x, cos, sin):
    half = x.shape[-1] // 2
    x1, x2 = x[..., :half], x[..., half:]
    c = cos[None, :, None, :].astype(x.dtype)
    s = sin[None, :, None, :].astype(x.dtype)
    return jnp.concatenate([x1 * c - x2 * s, x1 * s + x2 * c], axis=-1)


def gated_delta_chunked(q, k, v, g, beta, s0):
    f32 = jnp.float32
    b, t, h, dk = q.shape
    dv = v.shape[-1]
    n = t // DN_CHUNK

    def chunks(a):
        a = a.astype(f32).reshape((b, n, DN_CHUNK, h) + a.shape[3:])
        return jnp.moveaxis(a, 2, 3).swapaxes(0, 1)

    qc, kc, vc, gc, bc = chunks(q), chunks(k), chunks(v), chunks(g), chunks(beta)
    gcum = jnp.cumsum(gc, axis=-1)
    causal = jnp.tril(jnp.ones((DN_CHUNK, DN_CHUNK), bool))
    strict = jnp.tril(jnp.ones((DN_CHUNK, DN_CHUNK), bool), -1)
    diff = gcum[..., :, None] - gcum[..., None, :]
    decay = jnp.where(causal, jnp.exp(jnp.where(causal, diff, 0.0)), 0.0)
    kb = kc * bc[..., None]
    a_mat = jnp.where(strict, jnp.einsum('nbhid,nbhjd->nbhij', kb, kc) * decay, 0.0)
    tmat = jnp.eye(DN_CHUNK, dtype=f32) + a_mat
    u = lax.linalg.triangular_solve(tmat, vc * bc[..., None], left_side=True, lower=True, unit_diagonal=True)
    w = lax.linalg.triangular_solve(tmat, kb * jnp.exp(gcum)[..., None], left_side=True, lower=True,
                                    unit_diagonal=True)
    qk = jnp.where(causal, jnp.einsum('nbhid,nbhjd->nbhij', qc, kc) * decay, 0.0)

    def step(state, xs):
        q_i, k_i, u_i, w_i, qk_i, g_i = xs
        v_new = u_i - jnp.einsum('bhcd,bhde->bhce', w_i, state)
        o = (jnp.einsum('bhcd,bhde->bhce', q_i * jnp.exp(g_i)[..., None], state)
             + jnp.einsum('bhij,bhje->bhie', qk_i, v_new))
        g_last = g_i[..., -1]
        state = (state * jnp.exp(g_last)[..., None, None]
                 + jnp.einsum('bhcd,bhce->bhde', k_i * jnp.exp(g_last[..., None] - g_i)[..., None], v_new))
        return state, o

    s_fin, o = lax.scan(step, s0.astype(f32), (qc, kc, u, w, qk, gcum))
    o = jnp.moveaxis(o.swapaxes(0, 1), 3, 2).reshape(b, t, h, dv)
    return o, s_fin


def ssd_chunked(x, a, bm, cm, s0):
    f32 = jnp.float32
    b, t, h, pdim = x.shape
    n = t // SSM_CHUNK

    def chunks(arr):
        return arr.astype(f32).reshape((b, n, SSM_CHUNK) + arr.shape[2:]).swapaxes(0, 1)

    xc, bc, cc = chunks(x), chunks(bm), chunks(cm)
    ac = jnp.moveaxis(chunks(a), 2, 3)
    acum = jnp.cumsum(ac, axis=-1)
    causal = jnp.tril(jnp.ones((SSM_CHUNK, SSM_CHUNK), bool))
    diff = acum[..., :, None] - acum[..., None, :]
    lmat = jnp.where(causal, jnp.exp(jnp.where(causal, diff, 0.0)), 0.0)
    scores = jnp.einsum('nbihs,nbjhs->nbhij', cc, bc) * lmat
    y_diag = jnp.einsum('nbhij,nbjhp->nbihp', scores, xc)
    decay_to_end = jnp.exp(acum[..., -1:] - acum)
    chunk_states = jnp.einsum('nbjhs,nbhj,nbjhp->nbhps', bc, decay_to_end, xc)
    chunk_decay = jnp.exp(acum[..., -1])

    def step(state, xs):
        st, dec = xs
        return state * dec[..., None, None] + st, state

    s_fin, s_in = lax.scan(step, s0.astype(f32), (chunk_states, chunk_decay))
    y_off = jnp.einsum('nbihs,nbhps,nbhi->nbihp', cc, s_in, jnp.exp(acum))
    y = (y_diag + y_off).swapaxes(0, 1).reshape(b, t, h, pdim)
    return y, s_fin


def deltanet_mixer(p, lp, s0):
    f32 = jnp.float32
    b, t, _ = p.shape
    qkv, z, beta_raw, alpha_raw = split_cols(p, [3 * GROUP_W, GROUP_W, 2 * DN_HEADS, 2 * DN_HEADS])
    qkv = jax.nn.silu(centred_depthwise_conv(qkv, lp['dn_conv_w']))
    q, k, v = [a.reshape(b, t, DN_HEADS, HEAD_DIM) for a in jnp.split(qkv, 3, axis=-1)]
    q = l2_normalize(q) * (DN_DK ** -0.5)
    k = l2_normalize(k)
    beta = jax.nn.sigmoid(beta_raw.astype(f32)).reshape(b, t, 2, DN_HEADS)
    g = -jnp.exp(lp['dn_a_log'].astype(f32)) * jax.nn.softplus(
        alpha_raw.astype(f32).reshape(b, t, 2, DN_HEADS) + lp['dn_dt_bias'].astype(f32))
    o_f, s_f = gated_delta_chunked(q, k, v, g[:, :, 0], beta[:, :, 0], s0[:, 0])
    o_b, s_b = gated_delta_chunked(jnp.flip(q, 1), jnp.flip(k, 1), jnp.flip(v, 1),
                                   jnp.flip(g[:, :, 1], 1), jnp.flip(beta[:, :, 1], 1), s0[:, 1])
    o = (o_f + jnp.flip(o_b, 1)).astype(p.dtype)
    o = rms_norm(o, lp['dn_norm_w']) * jax.nn.silu(z.reshape(b, t, DN_HEADS, DN_DV))
    return o.reshape(b, t, GROUP_W), jnp.stack([s_f, s_b], axis=1).astype(p.dtype)


def ssd_mixer(p, lp, s0):
    f32 = jnp.float32
    b, t, _ = p.shape
    z, xbc, dt_raw = split_cols(p, [GROUP_W, GROUP_W + 2 * SSM_GROUPS * SSM_N, 2 * SSM_HEADS])
    xbc = jax.nn.silu(centred_depthwise_conv(xbc, lp['ssm_conv_w']) + lp['ssm_conv_b'])
    xs, bm, cm = split_cols(xbc, [GROUP_W, SSM_GROUPS * SSM_N, SSM_GROUPS * SSM_N])
    xs = xs.reshape(b, t, SSM_HEADS, SSM_P)
    rep = SSM_HEADS // SSM_GROUPS
    bm = jnp.repeat(bm.reshape(b, t, SSM_GROUPS, SSM_N), rep, axis=2)
    cm = jnp.repeat(cm.reshape(b, t, SSM_GROUPS, SSM_N), rep, axis=2)
    dt = jax.nn.softplus(dt_raw.astype(f32).reshape(b, t, 2, SSM_HEADS) + lp['ssm_dt_bias'].astype(f32))
    a = -jnp.exp(lp['ssm_a_log'].astype(f32)) * dt
    xdt = xs.astype(f32)[:, :, None] * dt[..., None]
    y_f, s_f = ssd_chunked(xdt[:, :, 0], a[:, :, 0], bm, cm, s0[:, 0])
    y_b, s_b = ssd_chunked(jnp.flip(xdt[:, :, 1], 1), jnp.flip(a[:, :, 1], 1),
                           jnp.flip(bm, 1), jnp.flip(cm, 1), s0[:, 1])
    y = y_f + jnp.flip(y_b, 1) + lp['ssm_d'].astype(f32)[:, None] * xs.astype(f32)
    y = (y.reshape(b, t, GROUP_W) * jax.nn.silu(z.astype(f32))).astype(p.dtype)
    y = rms_norm(y.reshape(b, t, SSM_GROUPS, GROUP_W // SSM_GROUPS),
                 lp['ssm_norm_w'].reshape(SSM_GROUPS, GROUP_W // SSM_GROUPS))
    return y.reshape(b, t, GROUP_W), jnp.stack([s_f, s_b], axis=1).astype(p.dtype)


def mla_project(p, lp):
    b, t, _ = p.shape
    q_lat, kv_lat, k_pe = split_cols(p, [MLA_Q_LORA, MLA_KV_LORA, MLA_ROPE])
    q = (rms_norm(q_lat, lp['mla_q_norm_w']) @ lp['mla_w_uq']).reshape(b, t, MLA_HEADS, MLA_NOPE + MLA_ROPE)
    c_kv = rms_norm(kv_lat, lp['mla_kv_norm_w'])
    return q[..., :MLA_NOPE], q[..., MLA_NOPE:], c_kv, k_pe


def mla_expand(c_kv, k_pe, w_ukv):
    b, t, _ = c_kv.shape
    kv = (c_kv @ w_ukv).reshape(b, t, MLA_HEADS, MLA_NOPE + MLA_VHD)
    k = jnp.concatenate([kv[..., :MLA_NOPE],
                         jnp.broadcast_to(k_pe[:, :, None, :], (b, t, MLA_HEADS, MLA_ROPE))], axis=-1)
    return k, kv[..., MLA_NOPE:]


def swa_project(p):
    b, t, _ = p.shape
    q, k, v = split_cols(p, [SWA_HEADS * HEAD_DIM, SWA_KV_HEADS * HEAD_DIM, SWA_KV_HEADS * HEAD_DIM])
    return (q.reshape(b, t, SWA_KV_HEADS, SWA_GQA, HEAD_DIM),
            k.reshape(b, t, SWA_KV_HEADS, HEAD_DIM), v.reshape(b, t, SWA_KV_HEADS, HEAD_DIM))


def dense_attention(q, k, v, scale, sink=None):
    b, tq, kvh, g, _ = q.shape

    def one_block(i):
        qi = lax.dynamic_slice_in_dim(q, i * Q_BLOCK, Q_BLOCK, axis=1)
        s = jnp.einsum('bqhgd,bkhd->bhgqk', qi, k, preferred_element_type=jnp.float32) * scale
        if sink is not None:
            sk = jnp.broadcast_to(sink.astype(jnp.float32).reshape(1, kvh, g, 1, 1), s.shape[:-1] + (1,))
            prob = jax.nn.softmax(jnp.concatenate([s, sk], axis=-1), axis=-1)[..., :-1]
        else:
            prob = jax.nn.softmax(s, axis=-1)
        return jnp.einsum('bhgqk,bkhd->bqhgd', prob.astype(v.dtype), v)

    o = lax.map(one_block, jnp.arange(tq // Q_BLOCK))
    return jnp.moveaxis(o, 0, 1).reshape(b, tq, kvh, g, v.shape[-1])


def banded_window_attention(q, k, v, k_ctx, v_ctx, sink, scale):
    b, t, kvh, g, _ = q.shape
    n_ctx = k_ctx.shape[1]
    pad = ((0, 0), (SWA_BLOCK, SWA_BLOCK), (0, 0), (0, 0))
    k_pad, v_pad = jnp.pad(k, pad), jnp.pad(v, pad)
    qpos_local = jnp.arange(SWA_BLOCK)
    kpos_local = jnp.arange(3 * SWA_BLOCK)
    in_window = jnp.abs((kpos_local[None, :] - SWA_BLOCK) - qpos_local[:, None]) <= SWA_WINDOW

    def one_block(i):
        start = i * SWA_BLOCK
        qi = lax.dynamic_slice_in_dim(q, start, SWA_BLOCK, axis=1)
        ki = lax.dynamic_slice_in_dim(k_pad, start, 3 * SWA_BLOCK, axis=1)
        vi = lax.dynamic_slice_in_dim(v_pad, start, 3 * SWA_BLOCK, axis=1)
        kpos = start - SWA_BLOCK + kpos_local
        valid = in_window & ((kpos >= 0) & (kpos < t))[None, :]
        s_loc = jnp.einsum('bqhgd,bchd->bhgqc', qi, ki, preferred_element_type=jnp.float32) * scale
        s_loc = jnp.where(valid, s_loc, NEG_INF)
        s_ctx = jnp.einsum('bqhgd,blhd->bhgql', qi, k_ctx, preferred_element_type=jnp.float32) * scale
        sk = jnp.broadcast_to(sink.astype(jnp.float32).reshape(1, kvh, g, 1, 1), s_loc.shape[:-1] + (1,))
        prob = jax.nn.softmax(jnp.concatenate([s_loc, s_ctx, sk], axis=-1), axis=-1)
        p_loc = prob[..., :3 * SWA_BLOCK].astype(v.dtype)
        p_ctx = prob[..., 3 * SWA_BLOCK:3 * SWA_BLOCK + n_ctx].astype(v.dtype)
        return (jnp.einsum('bhgqc,bchd->bqhgd', p_loc, vi)
                + jnp.einsum('bhgql,blhd->bqhgd', p_ctx, v_ctx))

    o = lax.map(one_block, jnp.arange(t // SWA_BLOCK))
    return jnp.moveaxis(o, 0, 1).reshape(b, t, kvh, g, v.shape[-1])


def context_mixers(p, lp):
    b, n_ctx, _ = p.shape
    p_dn, p_mla, p_ssm, p_swa = split_cols(p, [DN_IN, MLA_IN, SSM_IN, SWA_IN])
    o_dn, s_dn = deltanet_mixer(p_dn, lp, jnp.zeros((b, 2, DN_HEADS, DN_DK, DN_DV), p.dtype))
    q_nope, q_pe, c_kv, k_pe = mla_project(p_mla, lp)
    k_m, v_m = mla_expand(c_kv, k_pe, lp['mla_w_ukv'])
    q_m = jnp.concatenate([q_nope, q_pe], axis=-1)[:, :, :, None, :]
    o_mla = dense_attention(q_m, k_m, v_m, MLA_SCALE).reshape(b, n_ctx, GROUP_W)
    o_ssm, s_ssm = ssd_mixer(p_ssm, lp, jnp.zeros((b, 2, SSM_HEADS, SSM_P, SSM_N), p.dtype))
    q_s, k_s, v_s = swa_project(p_swa)
    o_swa = dense_attention(q_s, k_s, v_s, SWA_SCALE, lp['swa_sinks']).reshape(b, n_ctx, GROUP_W)
    mix = jnp.concatenate([o_dn, o_mla, o_ssm, o_swa], axis=-1)
    return mix, (s_dn, c_kv, k_pe, s_ssm, k_s, v_s)


def latent_mixers(p, lp, cache, rope):
    b, t, _ = p.shape
    s0_dn, ckv_ctx, kpe_ctx, s0_ssm, k_ctx, v_ctx = cache
    cos_m, sin_m, cos_s, sin_s = rope
    p_dn, p_mla, p_ssm, p_swa = split_cols(p, [DN_IN, MLA_IN, SSM_IN, SWA_IN])
    o_dn, _ = deltanet_mixer(p_dn, lp, s0_dn)
    q_nope, q_pe, c_kv, k_pe = mla_project(p_mla, lp)
    q_pe = apply_rope(q_pe, cos_m, sin_m)
    k_pe = apply_rope(k_pe[:, :, None, :], cos_m, sin_m)[:, :, 0]
    k_lat, v_lat = mla_expand(c_kv, k_pe, lp['mla_w_ukv'])
    k_c, v_c = mla_expand(ckv_ctx, kpe_ctx, lp['mla_w_ukv'])
    q_m = jnp.concatenate([q_nope, q_pe], axis=-1)[:, :, :, None, :]
    o_mla = dense_attention(q_m, jnp.concatenate([k_c, k_lat], axis=1),
                            jnp.concatenate([v_c, v_lat], axis=1), MLA_SCALE).reshape(b, t, GROUP_W)
    o_ssm, _ = ssd_mixer(p_ssm, lp, s0_ssm)
    q_s, k_s, v_s = swa_project(p_swa)
    q_s = apply_rope(q_s.reshape(b, t, SWA_HEADS, HEAD_DIM), cos_s, sin_s).reshape(
        b, t, SWA_KV_HEADS, SWA_GQA, HEAD_DIM)
    k_s = apply_rope(k_s, cos_s, sin_s)
    o_swa = banded_window_attention(q_s, k_s, v_s, k_ctx, v_ctx, lp['swa_sinks'], SWA_SCALE).reshape(
        b, t, GROUP_W)
    mix = jnp.concatenate([o_dn, o_mla, o_ssm, o_swa], axis=-1)
    return mix, None


def trunk_layer(x, ada, lp, mixer, *mixer_args):
    sh1, sc1, g1, sh2, sc2, g2 = jnp.split(ada, 6, axis=-1)
    p = modulated_norm(x, lp['norm1_w'], sh1, sc1) @ lp['w_in']
    mix, aux = mixer(p, lp, *mixer_args)
    x = x + g1 * (mix @ lp['w_out'])
    x = x + g2 * swiglu(modulated_norm(x, lp['norm2_w'], sh2, sc2), lp['w_gate_up'], lp['w_down'])
    return x, aux


def setup_inputs(seed: int = 0) -> dict:
    key = jax.random.key(seed)
    ks = iter(jax.random.split(key, 40))
    f32 = jnp.float32

    def nrm(shape, scale):
        return jax.random.normal(next(ks), shape, f32) * scale

    def gain(shape):
        return 1.0 + nrm(shape, 0.1)

    def a_log(shape):
        return jnp.log(jax.random.uniform(next(ks), shape, f32, 1.0, 16.0))

    def dt_bias(shape):
        dt = jnp.exp(jax.random.uniform(next(ks), shape, f32, math.log(1e-3), math.log(1e-1)))
        return dt + jnp.log(-jnp.expm1(-dt))

    return {
        'x_prompt': nrm((BATCH, SEQ, D_MODEL), 1.0),
        'x_sample': nrm((DEC_BATCH, DEC_SEQ, D_MODEL), 1.0),
        'c': nrm((DEC_BATCH, D_MODEL), 1.0),
        'state_dn': nrm((DEC_BATCH, DEPTH, 2, DN_HEADS, DN_DK, DN_DV), 0.2),
        'cache_mla_ckv': nrm((DEC_BATCH, DEPTH, PAST_LEN, MLA_KV_LORA), 1.0),
        'cache_mla_kpe': nrm((DEC_BATCH, DEPTH, PAST_LEN, MLA_ROPE), 1.0),
        'state_ssm': nrm((DEC_BATCH, DEPTH, 2, SSM_HEADS, SSM_P, SSM_N), 0.2),
        'cache_swa_k': nrm((DEC_BATCH, DEPTH, PAST_LEN, SWA_KV_HEADS, HEAD_DIM), 1.0),
        'cache_swa_v': nrm((DEC_BATCH, DEPTH, PAST_LEN, SWA_KV_HEADS, HEAD_DIM), 1.0),
        'c_ctx': nrm((D_MODEL,), 1.0),
        'norm1_w': gain((DEPTH, D_MODEL)),
        'norm2_w': gain((DEPTH, D_MODEL)),
        'w_ada': nrm((DEPTH, D_MODEL, 6 * D_MODEL), 0.5 * D_MODEL ** -0.5),
        'b_ada': nrm((DEPTH, 6 * D_MODEL), 0.02),
        'w_in': nrm((DEPTH, D_MODEL, IN_DIM), D_MODEL ** -0.5),
        'w_out': nrm((DEPTH, MIX_W, D_MODEL), MIX_W ** -0.5),
        'dn_conv_w': nrm((DEPTH, CONV_K, 3 * GROUP_W), CONV_K ** -0.5),
        'dn_a_log': a_log((DEPTH, 2, DN_HEADS)),
        'dn_dt_bias': dt_bias((DEPTH, 2, DN_HEADS)),
        'dn_norm_w': gain((DEPTH, DN_DV)),
        'mla_q_norm_w': gain((DEPTH, MLA_Q_LORA)),
        'mla_w_uq': nrm((DEPTH, MLA_Q_LORA, MLA_HEADS * (MLA_NOPE + MLA_ROPE)), MLA_Q_LORA ** -0.5),
        'mla_kv_norm_w': gain((DEPTH, MLA_KV_LORA)),
        'mla_w_ukv': nrm((DEPTH, MLA_KV_LORA, MLA_HEADS * (MLA_NOPE + MLA_VHD)), MLA_KV_LORA ** -0.5),
        'ssm_conv_w': nrm((DEPTH, CONV_K, GROUP_W + 2 * SSM_GROUPS * SSM_N), CONV_K ** -0.5),
        'ssm_conv_b': nrm((DEPTH, GROUP_W + 2 * SSM_GROUPS * SSM_N), 0.02),
        'ssm_a_log': a_log((DEPTH, 2, SSM_HEADS)),
        'ssm_dt_bias': dt_bias((DEPTH, 2, SSM_HEADS)),
        'ssm_d': gain((DEPTH, SSM_HEADS)),
        'ssm_norm_w': gain((DEPTH, GROUP_W)),
        'swa_sinks': nrm((DEPTH, SWA_HEADS), 0.5),
        'w_gate_up': nrm((DEPTH, D_MODEL, 2 * FF_DIM), D_MODEL ** -0.5),
        'w_down': nrm((DEPTH, FF_DIM, D_MODEL), FF_DIM ** -0.5),
        'final_norm_w': gain((D_MODEL,)),
    }


def reference(x_prompt, x_sample, c, state_dn, cache_mla_ckv, cache_mla_kpe, state_ssm, cache_swa_k,
              cache_swa_v, c_ctx, norm1_w, norm2_w, w_ada, b_ada, w_in, w_out, dn_conv_w, dn_a_log,
              dn_dt_bias, dn_norm_w, mla_q_norm_w, mla_w_uq, mla_kv_norm_w, mla_w_ukv, ssm_conv_w,
              ssm_conv_b, ssm_a_log, ssm_dt_bias, ssm_d, ssm_norm_w, swa_sinks, w_gate_up, w_down,
              final_norm_w):
    rows = x_sample.shape[1] // GRID_W
    cos_m, sin_m = axial_rope(rows, MLA_ROPE)
    cos_s, sin_s = axial_rope(rows, HEAD_DIM)
    rope = (cos_m, sin_m, cos_s, sin_s)
    x_ctx, x_lat = x_prompt, x_sample
    st_dn, st_ckv, st_kpe, st_ssm, st_k, st_v = [], [], [], [], [], []
    for l in range(DEPTH):
        lp = {
            'norm1_w': norm1_w[l], 'norm2_w': norm2_w[l], 'w_in': w_in[l], 'w_out': w_out[l],
            'dn_conv_w': dn_conv_w[l], 'dn_a_log': dn_a_log[l], 'dn_dt_bias': dn_dt_bias[l],
            'dn_norm_w': dn_norm_w[l], 'mla_q_norm_w': mla_q_norm_w[l], 'mla_w_uq': mla_w_uq[l],
            'mla_kv_norm_w': mla_kv_norm_w[l], 'mla_w_ukv': mla_w_ukv[l], 'ssm_conv_w': ssm_conv_w[l],
            'ssm_conv_b': ssm_conv_b[l], 'ssm_a_log': ssm_a_log[l], 'ssm_dt_bias': ssm_dt_bias[l],
            'ssm_d': ssm_d[l], 'ssm_norm_w': ssm_norm_w[l], 'swa_sinks': swa_sinks[l],
            'w_gate_up': w_gate_up[l], 'w_down': w_down[l],
        }
        ada_ctx = jax.nn.silu(c_ctx) @ w_ada[l] + b_ada[l]
        x_ctx, (s_dn, ckv, kpe, s_ssm, k_s, v_s) = trunk_layer(x_ctx, ada_ctx, lp, context_mixers)
        st_dn.append(s_dn)
        st_ckv.append(ckv)
        st_kpe.append(kpe)
        st_ssm.append(s_ssm)
        st_k.append(k_s)
        st_v.append(v_s)
        ada_lat = (jax.nn.silu(c) @ w_ada[l] + b_ada[l])[:, None, :]
        cache_l = (state_dn[:, l], cache_mla_ckv[:, l], cache_mla_kpe[:, l], state_ssm[:, l],
                   cache_swa_k[:, l], cache_swa_v[:, l])
        x_lat, _ = trunk_layer(x_lat, ada_lat, lp, latent_mixers, cache_l, rope)
    y_prompt = rms_norm(x_ctx, final_norm_w)
    y_sample = rms_norm(x_lat, final_norm_w)
    new_state_dn = jnp.stack(st_dn, axis=1)
    new_mla_ckv = jnp.stack(st_ckv, axis=1)
    new_mla_kpe = jnp.stack(st_kpe, axis=1)
    new_state_ssm = jnp.stack(st_ssm, axis=1)
    new_swa_k = jnp.stack(st_k, axis=1)
    new_swa_v = jnp.stack(st_v, axis=1)
    return (y_prompt, y_sample, new_state_dn, new_mla_ckv, new_mla_kpe, new_state_ssm, new_swa_k, new_swa_v)
```

```python
import functools

import jax
import jax.numpy as jnp
from jax import lax
from jax.experimental import pallas as pl
from jax.experimental.pallas import tpu as pltpu

F32 = jnp.float32
BF16 = jnp.bfloat16

D_MODEL = 1024
GRID_W = 64
HEAD_DIM = 64
GROUP_W = 256
EPS = 1e-6
ROPE_THETA = 10000.0
NEG_INF = -1e30
N_HEADS = 4
MLA_NOPE = 64
MLA_ROPE = 32
MLA_Q_LORA = 256
MLA_KV_LORA = 128
MLA_SCALE = (MLA_NOPE + MLA_ROPE) ** -0.5
SSM_N = 64
SWA_SCALE = HEAD_DIM ** -0.5
SWA_BLOCK = 128
FF_DIM = 2816
FF_CHUNK = 256

LANES = 128
SUBLANES = 8
SEQ_TILE = 256
ROW_TILE = 512
VMEM_LIMIT = 56 * 1024 * 1024

SEG_WIDTHS = (768, 256, 256, 128, 256, 512, 512, 128)
IN_PAD = sum(SEG_WIDTHS)
SM_BETA, SM_ALPHA, SM_DT, SM_KPE = 0, 8, 16, 64


def _sigmoid(x):
    return 1.0 / (1.0 + jnp.exp(-x))


def _silu(x):
    return x * _sigmoid(x)


def _softplus(x):
    return jnp.maximum(x, 0.0) + jnp.log1p(jnp.exp(-jnp.abs(x)))


def _mm(a, b):
    return jnp.dot(a.astype(BF16), b.astype(BF16), preferred_element_type=F32)


def _mm_nt(a, b):
    return lax.dot_general(a.astype(BF16), b.astype(BF16), (((1,), (1,)), ((), ())),
                           preferred_element_type=F32)


def _mm_tn(a, b):
    return lax.dot_general(a.astype(BF16), b.astype(BF16), (((0,), (0,)), ((), ())),
                           preferred_element_type=F32)


def _split(a, parts):
    out = []
    for _ in range(parts):
        hi = a.astype(BF16)
        out.append(hi)
        a = a - hi.astype(F32)
    return out


def _mm_split_lhs(a, b_exact, parts, nt=False):
    dims = (((1,), (1,)), ((), ())) if nt else (((1,), (0,)), ((), ()))
    acc = None
    for piece in _split(a, parts):
        r = lax.dot_general(piece, b_exact, dims, preferred_element_type=F32)
        acc = r if acc is None else acc + r
    return acc


def _mm_split_rhs(a_exact, b, parts):
    acc = None
    for piece in _split(b, parts):
        r = jnp.dot(a_exact, piece, preferred_element_type=F32)
        acc = r if acc is None else acc + r
    return acc


def _ones_where(mask, dtype):
    return jnp.where(mask, 1.0, 0.0).astype(dtype)


def _rms(x, w):
    return x * lax.rsqrt(jnp.mean(x * x, axis=-1, keepdims=True) + EPS) * w


def _cparams(sem):
    return pltpu.CompilerParams(dimension_semantics=sem, vmem_limit_bytes=VMEM_LIMIT)


def _ada_kernel(c_ref, w_ref, b_ref, o_ref):
    o_ref[0] = _mm(_silu(c_ref[...]), w_ref[0]) + b_ref[0]


def _ada(cc, w_ada, b_ada):
    depth, d, n = w_ada.shape
    tn = 1536
    return pl.pallas_call(
        _ada_kernel,
        out_shape=jax.ShapeDtypeStruct((depth, cc.shape[0], n), F32),
        grid=(depth, n // tn),
        in_specs=[pl.BlockSpec(cc.shape, lambda l, j: (0, 0)),
                  pl.BlockSpec((1, d, tn), lambda l, j: (l, 0, j)),
                  pl.BlockSpec((1, 1, tn), lambda l, j: (l, 0, j))],
        out_specs=pl.BlockSpec((1, cc.shape[0], tn), lambda l, j: (l, 0, j)),
        compiler_params=_cparams(("arbitrary", "arbitrary")),
        name="ada",
    )(cc, w_ada, b_ada.reshape(depth, 1, n))


class _Geom:
    def __init__(self, n_ctx_seq, ctx_len, n_lat_seq, lat_len):
        self.n_ctx_seq, self.ctx_len = n_ctx_seq, ctx_len
        self.n_lat_seq, self.lat_len = n_lat_seq, lat_len
        self.n_ctx = n_ctx_seq * ctx_len
        self.n_lat = n_lat_seq * lat_len
        self.n_tok = self.n_ctx + self.n_lat
        assert ctx_len % SEQ_TILE == 0 and lat_len % ROW_TILE == 0 and self.n_ctx % ROW_TILE == 0
        self.cps = ctx_len // SEQ_TILE
        self.lps = lat_len // SEQ_TILE
        self.n_ctx_tiles = n_ctx_seq * self.cps
        self.n_tiles = self.n_ctx_tiles + n_lat_seq * self.lps
        self.n_seq = n_ctx_seq + n_lat_seq

    def mod_row(self, i):
        r = i * ROW_TILE
        return jnp.where(r >= self.n_ctx, 1 + (r - self.n_ctx) // self.lat_len, 0)

    def tile_pos(self, t):
        is_lat = t >= self.n_ctx_tiles
        u = t - self.n_ctx_tiles
        seq = jnp.where(is_lat, self.n_ctx_seq + u // self.lps, t // self.cps)
        pos = jnp.where(is_lat, u % self.lps, t % self.cps)
        nt = jnp.where(is_lat, self.lps, self.cps)
        return seq, pos, nt

    def mirror(self, t):
        _, pos, nt = self.tile_pos(t)
        return t - pos + (nt - 1 - pos)


def _inproj_kernel(x_ref, ada_ref, nw_ref, w_ref, *out_refs):
    h = _rms(x_ref[...], nw_ref[...]) * (1.0 + ada_ref[0, 1:2, :]) + ada_ref[0, 0:1, :]
    h = h.astype(BF16)
    off = 0
    for o_ref in out_refs:
        wd = o_ref.shape[-1]
        o_ref[...] = jnp.dot(h, w_ref[:, off:off + wd], preferred_element_type=F32)
        off += wd


def _inproj(geom, x, ada_l, norm_w, w_pad):
    d = x.shape[1]
    return pl.pallas_call(
        _inproj_kernel,
        out_shape=[jax.ShapeDtypeStruct((geom.n_tok, wd), F32) for wd in SEG_WIDTHS],
        grid=(geom.n_tok // ROW_TILE,),
        in_specs=[pl.BlockSpec((ROW_TILE, d), lambda i: (i, 0)),
                  pl.BlockSpec((1, 6, d), lambda i: (geom.mod_row(i), 0, 0)),
                  pl.BlockSpec((1, d), lambda i: (0, 0)),
                  pl.BlockSpec((d, IN_PAD), lambda i: (0, 0))],
        out_specs=[pl.BlockSpec((ROW_TILE, wd), lambda i: (i, 0)) for wd in SEG_WIDTHS],
        compiler_params=_cparams(("parallel",)),
        name="inproj",
    )(x, ada_l, norm_w.reshape(1, d), w_pad)


def _conv3(x, x_prev, x_next, w_ref):
    n = x.shape[0]
    r = lax.broadcasted_iota(jnp.int32, x.shape, 0)
    x_dn = jnp.where(r == 0, x_prev, pltpu.roll(x, 1, 0))
    x_up = jnp.where(r == n - 1, x_next, pltpu.roll(x, n - 1, 0))
    return w_ref[0:1, :] * x_dn + w_ref[1:2, :] * x + w_ref[2:3, :] * x_up


def _head_lanes(head_l, colfn):
    out = colfn(N_HEADS - 1)
    for h in range(N_HEADS - 2, -1, -1):
        out = jnp.where(head_l == h, colfn(h), out)
    return out


def _cumulative(z, incl_b, parts=3):
    cum = _mm_split_rhs(incl_b, z, parts)
    cumt = _mm_split_lhs(z.T, incl_b, parts, nt=True)
    return cum, cumt


def _unit_tri_inverse(a, lvl, eye):
    a0 = jnp.where(lvl < 3, a, 0.0)
    x = eye - a0
    p = _mm(a0, a0)
    x = x + _mm(x, p)
    p = _mm(p, p)
    x = x + _mm(x, p)
    for m in range(3, 8):
        am = jnp.where(lvl == m, a, 0.0)
        x = x - _mm(x, _mm(am, x))
    return x


def _seq_tile_specs(geom, width, mirror):
    rows8 = SEQ_TILE // SUBLANES
    last8 = geom.n_tok // SUBLANES - 1
    tile = (lambda t: geom.mirror(t)) if mirror else (lambda t: t)
    return [pl.BlockSpec((SEQ_TILE, width), lambda t: (tile(t), 0)),
            pl.BlockSpec((SUBLANES, width), lambda t: (jnp.maximum(tile(t) * rows8 - 1, 0), 0)),
            pl.BlockSpec((SUBLANES, width), lambda t: (jnp.minimum((tile(t) + 1) * rows8, last8), 0))]


def _dn_kernel(xf_ref, xfp_ref, xfn_ref, xb_ref, xbp_ref, xbn_ref, gf_ref, gb_ref, s0_ref,
               cw_ref, alog_ref, bias_ref, of_ref, ob_ref, sfin_ref, s_scr, *, geom):
    t = pl.program_id(0)
    _, pos, nt = geom.tile_pos(t)
    tt = SEQ_TILE

    @pl.when(pos == 0)
    def _():
        s_scr[...] = s0_ref[0]

    row = lax.broadcasted_iota(jnp.int32, (tt, tt), 0)
    col = lax.broadcasted_iota(jnp.int32, (tt, tt), 1)
    xr = row ^ col
    lvl = jnp.where(xr >= 2, 1, 0)
    for kbit in range(2, 8):
        lvl = lvl + jnp.where(xr >= (1 << kbit), 1, 0)
    head_l = col // HEAD_DIM
    blockdiag = (row // HEAD_DIM) == head_l
    eye = _ones_where(row == col, F32)
    gones = _ones_where(blockdiag, BF16)
    lane_g = lax.broadcasted_iota(jnp.int32, (tt, LANES), 1)
    not_first, not_last = pos > 0, pos < nt - 1

    dirs = ((xf_ref, xfp_ref, xfn_ref, gf_ref, of_ref, not_first, not_last),
            (xb_ref, xbp_ref, xbn_ref, gb_ref, ob_ref, not_last, not_first))
    for d, (x_ref, xp_ref, xn_ref, g_ref, o_ref, has_prev, has_next) in enumerate(dirs):
        x_prev = jnp.where(has_prev, xp_ref[SUBLANES - 1:SUBLANES, :], 0.0)
        x_next = jnp.where(has_next, xn_ref[0:1, :], 0.0)
        y = _silu(_conv3(x_ref[...], x_prev, x_next, cw_ref))
        q, k, v = y[:, :GROUP_W], y[:, GROUP_W:2 * GROUP_W], y[:, 2 * GROUP_W:]
        q = q * lax.rsqrt(_mm_split_lhs(q * q, gones, 2) + EPS) * (HEAD_DIM ** -0.5)
        k = k * lax.rsqrt(_mm_split_lhs(k * k, gones, 2) + EPS)

        s = g_ref[...]
        gate = -jnp.exp(alog_ref[...]) * _softplus(s + bias_ref[...])
        z = jnp.where(lane_g < SM_ALPHA, _sigmoid(s), gate)
        incl = (row >= col) if d == 0 else (row <= col)
        strict = (row > col) if d == 0 else (row < col)
        cum, cumt = _cumulative(z, _ones_where(incl, BF16))
        edge = tt - 1 if d == 0 else 0
        c_beta = SM_BETA + N_HEADS * d
        c_g = SM_ALPHA + N_HEADS * d

        beta_l = _head_lanes(head_l, lambda h: z[:, c_beta + h:c_beta + h + 1])
        cum_l = _head_lanes(head_l, lambda h: cum[:, c_g + h:c_g + h + 1])
        tot_l = _head_lanes(head_l[0:1, :], lambda h: cum[edge:edge + 1, c_g + h:c_g + h + 1])
        eg = jnp.exp(cum_l)
        kb = k * beta_l
        rhs = jnp.concatenate([v * beta_l, kb * eg], axis=1)
        u_all = jnp.zeros((tt, GROUP_W), F32)
        w_all = jnp.zeros((tt, GROUP_W), F32)
        qk_heads = []
        for h in range(N_HEADS):
            hm = head_l == h
            diff = cum[:, c_g + h:c_g + h + 1] - cumt[c_g + h:c_g + h + 1, :]
            decay = jnp.where(incl, jnp.exp(jnp.where(incl, diff, 0.0)), 0.0)
            a = jnp.where(strict, _mm_nt(jnp.where(hm, kb, 0.0), k) * decay, 0.0)
            qk_heads.append(_mm_nt(jnp.where(hm, q, 0.0), k) * decay)
            uw = _mm(_unit_tri_inverse(a, lvl, eye), rhs)
            u_all = jnp.where(hm, uw[:, :GROUP_W], u_all)
            w_all = jnp.where(hm, uw[:, GROUP_W:], w_all)

        state = s_scr[d]
        v_new = u_all - _mm(w_all, state)
        o = _mm(q * eg, state)
        for h in range(N_HEADS):
            o = o + jnp.where(head_l == h, _mm(qk_heads[h], v_new), 0.0)
        o_ref[...] = o
        kd = k * jnp.exp(tot_l - cum_l)
        s_scr[d] = state * jnp.exp(tot_l) + jnp.where(blockdiag, _mm_tn(kd, v_new), 0.0)

    sfin_ref[0] = s_scr[...]


def _deltanet(geom, qkv, small, s0, conv_w, alog_row, bias_row):
    tt = SEQ_TILE
    row_spec = lambda shape: pl.BlockSpec(shape, lambda t: (0,) * len(shape))
    seq_of = lambda t: geom.tile_pos(t)[0]
    in_specs = (_seq_tile_specs(geom, 3 * GROUP_W, False) + _seq_tile_specs(geom, 3 * GROUP_W, True)
                + [pl.BlockSpec((tt, LANES), lambda t: (t, 0)),
                   pl.BlockSpec((tt, LANES), lambda t: (geom.mirror(t), 0)),
                   pl.BlockSpec((1, 2, GROUP_W, GROUP_W), lambda t: (seq_of(t), 0, 0, 0)),
                   row_spec(conv_w.shape), row_spec((1, LANES)), row_spec((1, LANES))])
    return pl.pallas_call(
        functools.partial(_dn_kernel, geom=geom),
        out_shape=[jax.ShapeDtypeStruct((geom.n_tok, GROUP_W), F32),
                   jax.ShapeDtypeStruct((geom.n_tok, GROUP_W), F32),
                   jax.ShapeDtypeStruct((geom.n_seq, 2, GROUP_W, GROUP_W), F32)],
        grid=(geom.n_tiles,),
        in_specs=in_specs,
        out_specs=[pl.BlockSpec((tt, GROUP_W), lambda t: (t, 0)),
                   pl.BlockSpec((tt, GROUP_W), lambda t: (geom.mirror(t), 0)),
                   pl.BlockSpec((1, 2, GROUP_W, GROUP_W), lambda t: (seq_of(t), 0, 0, 0))],
        scratch_shapes=[pltpu.VMEM((2, GROUP_W, GROUP_W), F32)],
        compiler_params=_cparams(("arbitrary",)),
        name="deltanet",
    )(qkv, qkv, qkv, qkv, qkv, qkv, small, small, s0, conv_w, alog_row, bias_row)


def _ssm_kernel(xf_ref, xfp_ref, xfn_ref, xb_ref, xbp_ref, xbn_ref, gf_ref, gb_ref, s0_ref,
                cw_ref, cb_ref, alog_ref, bias_ref, dvec_ref, yf_ref, yb_ref, sfin_ref, s_scr,
                *, geom):
    t = pl.program_id(0)
    _, pos, nt = geom.tile_pos(t)
    tt = SEQ_TILE

    @pl.when(pos == 0)
    def _():
        s_scr[...] = s0_ref[0]

    row = lax.broadcasted_iota(jnp.int32, (tt, tt), 0)
    col = lax.broadcasted_iota(jnp.int32, (tt, tt), 1)
    head_l = col // HEAD_DIM
    lane_g = lax.broadcasted_iota(jnp.int32, (tt, LANES), 1)
    group_l = lane_g // SSM_N
    srow = lax.broadcasted_iota(jnp.int32, (2 * SSM_N, GROUP_W), 0)
    scol = lax.broadcasted_iota(jnp.int32, (2 * SSM_N, GROUP_W), 1)
    state_mask = (srow // SSM_N) == (scol // HEAD_DIM) // 2
    not_first, not_last = pos > 0, pos < nt - 1

    dirs = ((xf_ref, xfp_ref, xfn_ref, gf_ref, yf_ref, not_first, not_last),
            (xb_ref, xbp_ref, xbn_ref, gb_ref, yb_ref, not_last, not_first))
    for d, (x_ref, xp_ref, xn_ref, g_ref, y_ref, has_prev, has_next) in enumerate(dirs):
        x_prev = jnp.where(has_prev, xp_ref[SUBLANES - 1:SUBLANES, :], 0.0)
        x_next = jnp.where(has_next, xn_ref[0:1, :], 0.0)
        y = _silu(_conv3(x_ref[...], x_prev, x_next, cw_ref) + cb_ref[...])
        xs, bm, cm = y[:, :GROUP_W], y[:, GROUP_W:GROUP_W + 2 * SSM_N], y[:, GROUP_W + 2 * SSM_N:]

        dt = _softplus(g_ref[...] + bias_ref[...])
        a = -jnp.exp(alog_ref[...]) * dt
        incl = (row >= col) if d == 0 else (row <= col)
        cum, cumt = _cumulative(a, _ones_where(incl, BF16))
        edge = tt - 1 if d == 0 else 0
        c0 = SM_DT + N_HEADS * d

        dt_l = _head_lanes(head_l, lambda h: dt[:, c0 + h:c0 + h + 1])
        cum_l = _head_lanes(head_l, lambda h: cum[:, c0 + h:c0 + h + 1])
        tot_l = _head_lanes(head_l[0:1, :], lambda h: cum[edge:edge + 1, c0 + h:c0 + h + 1])
        xdt = xs * dt_l
        cb_scores = [_mm_nt(jnp.where(group_l == g, cm, 0.0), bm) for g in range(2)]
        out = jnp.zeros((tt, GROUP_W), F32)
        for h in range(N_HEADS):
            diff = cum[:, c0 + h:c0 + h + 1] - cumt[c0 + h:c0 + h + 1, :]
            lmat = jnp.where(incl, jnp.exp(jnp.where(incl, diff, 0.0)), 0.0)
            out = jnp.where(head_l == h, _mm(cb_scores[h // 2] * lmat, xdt), out)
        state = s_scr[d]
        out = out + _mm(cm, state) * jnp.exp(cum_l)
        if d == 0:
            out = out + dvec_ref[...] * xs
        y_ref[...] = out
        s_scr[d] = (state * jnp.exp(tot_l)
                    + jnp.where(state_mask, _mm_tn(bm, xdt * jnp.exp(tot_l - cum_l)), 0.0))

    sfin_ref[0] = s_scr[...]


def _ssd(geom, xbc, small, s0, conv_w, conv_b, alog_row, bias_row, dvec):
    tt = SEQ_TILE
    wx = GROUP_W + 4 * SSM_N
    row_spec = lambda shape: pl.BlockSpec(shape, lambda t: (0,) * len(shape))
    seq_of = lambda t: geom.tile_pos(t)[0]
    in_specs = (_seq_tile_specs(geom, wx, False) + _seq_tile_specs(geom, wx, True)
                + [pl.BlockSpec((tt, LANES), lambda t: (t, 0)),
                   pl.BlockSpec((tt, LANES), lambda t: (geom.mirror(t), 0)),
                   pl.BlockSpec((1, 2, 2 * SSM_N, GROUP_W), lambda t: (seq_of(t), 0, 0, 0)),
                   row_spec(conv_w.shape), row_spec((1, wx)), row_spec((1, LANES)),
                   row_spec((1, LANES)), row_spec((1, GROUP_W))])
    return pl.pallas_call(
        functools.partial(_ssm_kernel, geom=geom),
        out_shape=[jax.ShapeDtypeStruct((geom.n_tok, GROUP_W), F32),
                   jax.ShapeDtypeStruct((geom.n_tok, GROUP_W), F32),
                   jax.ShapeDtypeStruct((geom.n_seq, 2, 2 * SSM_N, GROUP_W), F32)],
        grid=(geom.n_tiles,),
        in_specs=in_specs,
        out_specs=[pl.BlockSpec((tt, GROUP_W), lambda t: (t, 0)),
                   pl.BlockSpec((tt, GROUP_W), lambda t: (geom.mirror(t), 0)),
                   pl.BlockSpec((1, 2, 2 * SSM_N, GROUP_W), lambda t: (seq_of(t), 0, 0, 0))],
        scratch_shapes=[pltpu.VMEM((2, 2 * SSM_N, GROUP_W), F32)],
        compiler_params=_cparams(("arbitrary",)),
        name="ssd",
    )(xbc, xbc, xbc, xbc, xbc, xbc, small, small, s0, conv_w, conv_b, alog_row, bias_row, dvec)


def _rope_slab(x, cos, sin_a, sin_b, half):
    w = x.shape[-1]
    return x * cos + pltpu.roll(x, w - half, 1) * sin_a + pltpu.roll(x, half, 1) * sin_b


def _mla_proj_kernel(ql_ref, kvl_ref, sm_ref, cos_ref, sa_ref, sb_ref, qnw_ref, kvnw_ref,
                     wuq_ref, wk_ref, wv_ref, qh_ref, kh_ref, vh_ref, ckv_ref):
    cos, sa, sb = cos_ref[...], sa_ref[...], sb_ref[...]
    half = MLA_ROPE // 2
    qp = _mm(_rms(ql_ref[...], qnw_ref[...]), wuq_ref[...]) * MLA_SCALE
    ckv = _rms(kvl_ref[...], kvnw_ref[...])
    ckv_ref[...] = ckv
    lane = lax.broadcasted_iota(jnp.int32, cos.shape, 1)
    is_pe = (lane >= MLA_NOPE) & (lane < MLA_NOPE + MLA_ROPE)
    kpe = jnp.where(is_pe, _rope_slab(sm_ref[...], cos, sa, sb, half), 0.0)
    kp = _mm(ckv, wk_ref[...])
    for h in range(N_HEADS):
        sl = slice(h * LANES, (h + 1) * LANES)
        qh_ref[:, sl] = _rope_slab(qp[:, sl], cos, sa, sb, half).astype(BF16)
        kh_ref[:, sl] = (kp[:, sl] + kpe).astype(BF16)
    vh_ref[...] = _mm(ckv, wv_ref[...]).astype(BF16)


def _mla_proj(geom, q_lat, kv_lat, small, cos, sa, sb, qnw, kvnw, wuq, wk, wv):
    tm = ROW_TILE
    tok = lambda w: pl.BlockSpec((tm, w), lambda i: (i, 0))
    full = lambda a: pl.BlockSpec(a.shape, lambda i: (0,) * a.ndim)
    return pl.pallas_call(
        _mla_proj_kernel,
        out_shape=[jax.ShapeDtypeStruct((geom.n_tok, N_HEADS * LANES), BF16),
                   jax.ShapeDtypeStruct((geom.n_tok, N_HEADS * LANES), BF16),
                   jax.ShapeDtypeStruct((geom.n_tok, GROUP_W), BF16),
                   jax.ShapeDtypeStruct((geom.n_tok, MLA_KV_LORA), F32)],
        grid=(geom.n_tok // tm,),
        in_specs=[tok(MLA_Q_LORA), tok(MLA_KV_LORA), tok(LANES), tok(LANES), tok(LANES), tok(LANES),
                  full(qnw), full(kvnw), full(wuq), full(wk), full(wv)],
        out_specs=[tok(N_HEADS * LANES), tok(N_HEADS * LANES), tok(GROUP_W), tok(MLA_KV_LORA)],
        compiler_params=_cparams(("parallel",)),
        name="mla_proj",
    )(q_lat, kv_lat, small, cos, sa, sb, qnw, kvnw, wuq, wk, wv)


def _mla_attn_kernel(*refs, has_cache):
    if has_cache:
        q_ref, k_ref, v_ref, ckv_ref, kpe_ref, wk_ref, wv_ref, o_ref = refs
        ckv_c = ckv_ref[0]
        v_c = _mm(ckv_c, wv_ref[...])
    else:
        q_ref, k_ref, v_ref, o_ref = refs
    v = v_ref[...]
    outs = []
    for j in range(2):
        sl = slice(j * LANES, (j + 1) * LANES)
        q = q_ref[:, sl]
        s = _mm_nt(q, k_ref[:, sl])
        m = jnp.max(s, axis=-1, keepdims=True)
        if has_cache:
            k_c = _mm(ckv_c, wk_ref[:, sl]) + kpe_ref[0]
            s_c = _mm_nt(q, k_c)
            m = jnp.maximum(m, jnp.max(s_c, axis=-1, keepdims=True))
            p_c = jnp.exp(s_c - m)
        p = jnp.exp(s - m)
        den = jnp.sum(p, axis=-1, keepdims=True)
        acc = _mm(p, v)
        if has_cache:
            den = den + jnp.sum(p_c, axis=-1, keepdims=True)
            acc = acc + _mm(p_c, v_c)
        outs.append(acc / den)
    lane = lax.broadcasted_iota(jnp.int32, outs[0].shape, 1)
    o_ref[...] = jnp.where(lane < HEAD_DIM, outs[0], outs[1])


def _mla_attn(qh, kh, vh, n_seq, seq_len, tok0, cache=None):
    tq = SEQ_TILE
    nq = seq_len // tq
    q0, k0 = tok0 // tq, tok0 // seq_len
    assert tok0 % seq_len == 0
    in_specs = [pl.BlockSpec((tq, 2 * LANES), lambda b, hp, i: (q0 + b * nq + i, hp)),
                pl.BlockSpec((seq_len, 2 * LANES), lambda b, hp, i: (k0 + b, hp)),
                pl.BlockSpec((seq_len, LANES), lambda b, hp, i: (k0 + b, hp))]
    args = [qh, kh, vh]
    if cache is not None:
        ckv_c, kpe_c, wk, wv = cache
        past = ckv_c.shape[1]
        in_specs += [pl.BlockSpec((1, past, MLA_KV_LORA), lambda b, hp, i: (b, 0, 0)),
                     pl.BlockSpec((1, past, LANES), lambda b, hp, i: (b, 0, 0)),
                     pl.BlockSpec((MLA_KV_LORA, 2 * LANES), lambda b, hp, i: (0, hp)),
                     pl.BlockSpec((MLA_KV_LORA, LANES), lambda b, hp, i: (0, hp))]
        args += [ckv_c, kpe_c, wk, wv]
    return pl.pallas_call(
        functools.partial(_mla_attn_kernel, has_cache=cache is not None),
        out_shape=jax.ShapeDtypeStruct((n_seq * seq_len, GROUP_W), F32),
        grid=(n_seq, 2, nq),
        in_specs=in_specs,
        out_specs=pl.BlockSpec((tq, LANES), lambda b, hp, i: (b * nq + i, hp)),
        compiler_params=_cparams(("parallel", "parallel", "arbitrary")),
        name="mla_attn_lat" if cache is not None else "mla_attn_ctx",
    )(*args)


def _swa_core(q, k_all, v_all, e_ref, sink_ref, valid):
    r = q.shape[0]
    e = e_ref[...]
    kx, vx = _mm(k_all, e), _mm(v_all, e)
    head_l = lax.broadcasted_iota(jnp.int32, q.shape, 1) // HEAD_DIM
    qs = jnp.concatenate([jnp.where(head_l == h, q, 0.0) for h in range(N_HEADS)], axis=0)
    s = _mm_nt(qs, kx)
    if valid is not None:
        s = jnp.where(jnp.concatenate([valid] * N_HEADS, axis=0), s, NEG_INF)
    row_head = lax.broadcasted_iota(jnp.int32, (N_HEADS * r, 1), 0) // r
    sink = _head_lanes(row_head, lambda h: sink_ref[:, h * HEAD_DIM:h * HEAD_DIM + 1])
    m = jnp.maximum(jnp.max(s, axis=-1, keepdims=True), sink)
    p = jnp.exp(s - m)
    den = jnp.sum(p, axis=-1, keepdims=True) + jnp.exp(sink - m)
    o4 = _mm(p, vx) / den
    out = o4[(N_HEADS - 1) * r:]
    for h in range(N_HEADS - 2, -1, -1):
        out = jnp.where(head_l == h, o4[h * r:(h + 1) * r], out)
    return out


def _swa_ctx_kernel(x_ref, e_ref, sink_ref, o_ref):
    x = x_ref[...]
    q = x[:, :GROUP_W] * SWA_SCALE
    k, v = x[:, GROUP_W:GROUP_W + LANES], x[:, GROUP_W + LANES:]
    o_ref[...] = _swa_core(q, k, v, e_ref, sink_ref, None)


def _swa_ctx(geom, swa, e_mat, sink_l):
    t = geom.ctx_len
    return pl.pallas_call(
        _swa_ctx_kernel,
        out_shape=jax.ShapeDtypeStruct((geom.n_ctx, GROUP_W), F32),
        grid=(geom.n_ctx_seq,),
        in_specs=[pl.BlockSpec((t, 2 * GROUP_W), lambda b: (b, 0)),
                  pl.BlockSpec(e_mat.shape, lambda b: (0, 0)),
                  pl.BlockSpec((1, GROUP_W), lambda b: (0, 0))],
        out_specs=pl.BlockSpec((t, GROUP_W), lambda b: (b, 0)),
        compiler_params=_cparams(("parallel",)),
        name="swa_ctx",
    )(swa, e_mat, sink_l)


def _swa_lat_kernel(xc_ref, xp_ref, xn_ref, cc_ref, ac_ref, bc_ref, cp_ref, ap_ref, bp_ref,
                    cn_ref, an_ref, bn_ref, kc_ref, vc_ref, e_ref, sink_ref, o_ref, *, nblk):
    i = pl.program_id(1)
    blk = SWA_BLOCK
    half = HEAD_DIM // 2

    def rope(x, c_ref, a_ref, b_ref):
        reps = x.shape[1] // LANES
        wide = lambda t_ref: jnp.concatenate([t_ref[...]] * reps, axis=1) if reps > 1 else t_ref[...]
        return _rope_slab(x, wide(c_ref), wide(a_ref), wide(b_ref), half)

    ksl, vsl = slice(GROUP_W, GROUP_W + LANES), slice(GROUP_W + LANES, 2 * GROUP_W)
    q = rope(xc_ref[:, :GROUP_W], cc_ref, ac_ref, bc_ref) * SWA_SCALE
    k_all = jnp.concatenate([rope(xp_ref[:, ksl], cp_ref, ap_ref, bp_ref),
                             rope(xc_ref[:, ksl], cc_ref, ac_ref, bc_ref),
                             rope(xn_ref[:, ksl], cn_ref, an_ref, bn_ref),
                             kc_ref[0]], axis=0)
    v_all = jnp.concatenate([xp_ref[:, vsl], xc_ref[:, vsl], xn_ref[:, vsl], vc_ref[0]], axis=0)
    nk = k_all.shape[0]
    r = lax.broadcasted_iota(jnp.int32, (blk, nk), 0)
    c = lax.broadcasted_iota(jnp.int32, (blk, nk), 1)
    in_prev = (c < blk) & (c >= r) & (i > 0)
    in_cur = (c >= blk) & (c < 2 * blk)
    in_next = (c >= 2 * blk) & (c < 3 * blk) & (c - 2 * blk <= r) & (i < nblk - 1)
    valid = in_prev | in_cur | in_next | (c >= 3 * blk)
    o_ref[...] = _swa_core(q, k_all, v_all, e_ref, sink_ref, valid)


def _swa_lat(geom, swa, cos, sa, sb, k_cache, v_cache, e_mat, sink_l):
    blk = SWA_BLOCK
    nblk = geom.lat_len // blk
    b0 = geom.n_ctx // blk
    past = k_cache.shape[1]
    cur = lambda b, i: i
    prv = lambda b, i: jnp.maximum(i - 1, 0)
    nxt = lambda b, i: jnp.minimum(i + 1, nblk - 1)
    xspec = lambda f: pl.BlockSpec((blk, 2 * GROUP_W), lambda b, i: (b0 + b * nblk + f(b, i), 0))
    tspec = lambda f: pl.BlockSpec((blk, LANES), lambda b, i: (f(b, i), 0))
    return pl.pallas_call(
        functools.partial(_swa_lat_kernel, nblk=nblk),
        out_shape=jax.ShapeDtypeStruct((geom.n_lat, GROUP_W), F32),
        grid=(geom.n_lat_seq, nblk),
        in_specs=[xspec(cur), xspec(prv), xspec(nxt),
                  tspec(cur), tspec(cur), tspec(cur), tspec(prv), tspec(prv), tspec(prv),
                  tspec(nxt), tspec(nxt), tspec(nxt),
                  pl.BlockSpec((1, past, LANES), lambda b, i: (b, 0, 0)),
                  pl.BlockSpec((1, past, LANES), lambda b, i: (b, 0, 0)),
                  pl.BlockSpec(e_mat.shape, lambda b, i: (0, 0)),
                  pl.BlockSpec((1, GROUP_W), lambda b, i: (0, 0))],
        out_specs=pl.BlockSpec((blk, GROUP_W), lambda b, i: (b * nblk + i, 0)),
        compiler_params=_cparams(("parallel", "arbitrary")),
        name="swa_lat",
    )(swa, swa, swa, cos, sa, sb, cos, sa, sb, cos, sa, sb, k_cache, v_cache, e_mat, sink_l)


def _outproj_kernel(x_ref, ada_ref, dof_ref, dob_ref, dz_ref, syf_ref, syb_ref, sz_ref,
                    omla_ref, oswa_ref, dnw_ref, snw_ref, w_ref, o_ref):
    tm = x_ref.shape[0]
    row = lax.broadcasted_iota(jnp.int32, (GROUP_W, GROUP_W), 0)
    col = lax.broadcasted_iota(jnp.int32, (GROUP_W, GROUP_W), 1)
    gones = _ones_where((row // HEAD_DIM) == (col // HEAD_DIM), BF16)
    o = dof_ref[...] + dob_ref[...]
    ms = _mm_split_lhs(o * o, gones, 2) * (1.0 / HEAD_DIM)
    dn = o * lax.rsqrt(ms + EPS) * dnw_ref[...] * _silu(dz_ref[...])
    acc = _mm(dn, w_ref[0:GROUP_W, :])
    acc = acc + _mm(omla_ref[...], w_ref[GROUP_W:2 * GROUP_W, :])
    y = (syf_ref[...] + syb_ref[...]) * _silu(sz_ref[...])
    for g in range(2):
        sl = slice(g * LANES, (g + 1) * LANES)
        acc = acc + _mm(_rms(y[:, sl], snw_ref[:, sl]),
                        w_ref[2 * GROUP_W + g * LANES:2 * GROUP_W + (g + 1) * LANES, :])
    acc = acc + _mm(oswa_ref[...], w_ref[3 * GROUP_W:, :])
    o_ref[...] = x_ref[...] + ada_ref[0, 2:3, :] * acc


def _outproj(geom, x, ada_l, parts, dnw, snw, w_out):
    tm = ROW_TILE
    d = x.shape[1]
    tok = lambda w: pl.BlockSpec((tm, w), lambda i: (i, 0))
    full = lambda a: pl.BlockSpec(a.shape, lambda i: (0,) * a.ndim)
    return pl.pallas_call(
        _outproj_kernel,
        out_shape=jax.ShapeDtypeStruct(x.shape, F32),
        grid=(geom.n_tok // tm,),
        in_specs=[tok(d), pl.BlockSpec((1, 6, d), lambda i: (geom.mod_row(i), 0, 0))]
        + [tok(GROUP_W)] * 8 + [full(dnw), full(snw), full(w_out)],
        out_specs=tok(d),
        compiler_params=_cparams(("parallel",)),
        name="outproj",
    )(x, ada_l, *parts, dnw, snw, w_out)


def _ffn_kernel(x_ref, ada_ref, nw_ref, wg_ref, wu_ref, wd_ref, fw_ref, o_ref, *, final_norm):
    x = x_ref[...]
    h = (_rms(x, nw_ref[...]) * (1.0 + ada_ref[0, 4:5, :]) + ada_ref[0, 3:4, :]).astype(BF16)
    acc = jnp.zeros(x.shape, F32)
    for c in range(FF_DIM // FF_CHUNK):
        sl = slice(c * FF_CHUNK, (c + 1) * FF_CHUNK)
        g = jnp.dot(h, wg_ref[:, sl], preferred_element_type=F32)
        u = jnp.dot(h, wu_ref[:, sl], preferred_element_type=F32)
        acc = acc + _mm(_silu(g) * u, wd_ref[sl, :])
    y = x + ada_ref[0, 5:6, :] * acc
    o_ref[...] = _rms(y, fw_ref[...]) if final_norm else y


def _ffn(geom, x, ada_l, norm_w, w_gate, w_up, w_down, final_w, final_norm):
    tm = ROW_TILE
    d = x.shape[1]
    resident = lambda a: pl.BlockSpec(a.shape, lambda i: (0,) * a.ndim, pipeline_mode=pl.Buffered(1))
    return pl.pallas_call(
        functools.partial(_ffn_kernel, final_norm=final_norm),
        out_shape=jax.ShapeDtypeStruct(x.shape, F32),
        grid=(geom.n_tok // tm,),
        in_specs=[pl.BlockSpec((tm, d), lambda i: (i, 0)),
                  pl.BlockSpec((1, 6, d), lambda i: (geom.mod_row(i), 0, 0)),
                  pl.BlockSpec((1, d), lambda i: (0, 0)),
                  resident(w_gate), resident(w_up), resident(w_down),
                  pl.BlockSpec((1, d), lambda i: (0, 0))],
        out_specs=pl.BlockSpec((tm, d), lambda i: (i, 0)),
        compiler_params=_cparams(("parallel",)),
        name="ffn",
    )(x, ada_l, norm_w.reshape(1, d), w_gate, w_up, w_down, final_w.reshape(1, d))


def _pad_cols(a, width):
    return jnp.pad(a, ((0, 0), (0, width - a.shape[1])))


def _w_in_layout(w):
    dn, mla, ssm, swa = 0, 1040, 1456, 2232
    small = jnp.concatenate([
        w[:, dn + 1024:dn + 1040],
        w[:, ssm + 768:ssm + 776],
        jnp.zeros((w.shape[0], SM_KPE - 24), w.dtype),
        w[:, mla + 384:mla + 416],
        jnp.zeros((w.shape[0], LANES - SM_KPE - MLA_ROPE), w.dtype)], axis=1)
    return jnp.concatenate([
        w[:, dn:dn + 768], w[:, dn + 768:dn + 1024],
        w[:, mla:mla + 256], w[:, mla + 256:mla + 384],
        w[:, ssm:ssm + 256], w[:, ssm + 256:ssm + 768],
        w[:, swa:swa + 512], small], axis=1).astype(BF16)


def _gate_row(dn_vec, ssm_vec):
    row = jnp.zeros((1, LANES), F32)
    row = row.at[0, SM_ALPHA:SM_ALPHA + 8].set(dn_vec.reshape(8))
    return row.at[0, SM_DT:SM_DT + 8].set(ssm_vec.reshape(8))


def _axial_angles(rows, rot_dim):
    row_ids = jnp.broadcast_to(jnp.arange(rows)[:, None], (rows, GRID_W)).reshape(-1).astype(F32)
    col_ids = jnp.broadcast_to(jnp.arange(GRID_W)[None, :], (rows, GRID_W)).reshape(-1).astype(F32)
    n_freq = rot_dim // 4
    inv_freq = ROPE_THETA ** (-jnp.arange(n_freq, dtype=F32) / n_freq)
    return jnp.concatenate([row_ids[:, None] * inv_freq, col_ids[:, None] * inv_freq], axis=-1)


def _rope_tables(ang, lane0, reps, n_ident):
    n, half = ang.shape
    cos, sin = jnp.cos(ang), jnp.sin(ang)
    zeros = jnp.zeros_like(sin)
    period = LANES // reps

    def table(first, second, fill):
        one = jnp.concatenate([jnp.full((n, lane0), fill, F32), first, second,
                               jnp.full((n, period - lane0 - 2 * half), fill, F32)], axis=1)
        tab = jnp.concatenate([one] * reps, axis=1)
        ident = jnp.full((n_ident, LANES), fill, F32)
        return jnp.concatenate([ident, tab], axis=0)

    return table(cos, cos, 1.0), table(-sin, zeros, 0.0), table(zeros, sin, 0.0)


def kernel(x_prompt, x_sample, c, state_dn, cache_mla_ckv, cache_mla_kpe, state_ssm, cache_swa_k,
           cache_swa_v, c_ctx, norm1_w, norm2_w, w_ada, b_ada, w_in, w_out, dn_conv_w, dn_a_log,
           dn_dt_bias, dn_norm_w, mla_q_norm_w, mla_w_uq, mla_kv_norm_w, mla_w_ukv, ssm_conv_w,
           ssm_conv_b, ssm_a_log, ssm_dt_bias, ssm_d, ssm_norm_w, swa_sinks, w_gate_up, w_down,
           final_norm_w):
    batch, seq, d = x_prompt.shape
    dec_batch, dec_seq, _ = x_sample.shape
    depth = w_in.shape[0]
    geom = _Geom(batch, seq, dec_batch, dec_seq)
    n_ctx = geom.n_ctx

    x = jnp.concatenate([x_prompt.reshape(n_ctx, d), x_sample.reshape(geom.n_lat, d)], axis=0)
    n_mod = -(-(1 + dec_batch) // SUBLANES) * SUBLANES
    cc = jnp.concatenate([c_ctx[None], c, jnp.zeros((n_mod - 1 - dec_batch, d), F32)], axis=0)
    ada = _ada(cc, w_ada, b_ada).reshape(depth, n_mod, 6, d)

    ang_m = _axial_angles(dec_seq // GRID_W, MLA_ROPE)
    ang_m = jnp.concatenate([ang_m] * dec_batch, axis=0)
    mla_tabs = _rope_tables(ang_m, MLA_NOPE, 1, n_ctx)
    swa_tabs = _rope_tables(_axial_angles(dec_seq // GRID_W, HEAD_DIM), 0, 2, 0)

    lane = jnp.arange(GROUP_W)
    e_mat = (jnp.arange(LANES)[:, None] == (lane // LANES) * HEAD_DIM + lane % HEAD_DIM).astype(BF16)
    eye_h = jnp.eye(N_HEADS, dtype=F32)
    grp_h = (jnp.arange(2)[:, None] == jnp.arange(N_HEADS)[None, :] // 2).astype(F32)

    st_dn, st_ckv, st_kpe, st_ssm, st_k, st_v = [], [], [], [], [], []
    for l in range(depth):
        ada_l = ada[l]
        segs = _inproj(geom, x, ada_l, norm1_w[l], _w_in_layout(w_in[l]))
        dn_qkv, dn_z, mla_q, mla_kv, ssm_z, ssm_xbc, swa, small = segs

        s0 = state_dn[:, l][:, :, :, :, None, :] * eye_h[None, None, :, None, :, None]
        s0 = s0.reshape(dec_batch, 2, GROUP_W, GROUP_W)
        s0 = jnp.concatenate([jnp.zeros((batch,) + s0.shape[1:], F32), s0], axis=0)
        alog_row = _gate_row(dn_a_log[l], ssm_a_log[l])
        bias_row = _gate_row(dn_dt_bias[l], ssm_dt_bias[l])
        dn_of, dn_ob, dn_fin = _deltanet(geom, dn_qkv, small, s0, dn_conv_w[l], alog_row, bias_row)
        fin = dn_fin[:batch].reshape(batch, 2, N_HEADS, HEAD_DIM, N_HEADS, HEAD_DIM)
        st_dn.append(jnp.stack([fin[:, :, h, :, h, :] for h in range(N_HEADS)], axis=2))

        s0 = jnp.swapaxes(state_ssm[:, l], -1, -2)
        s0 = s0[:, :, None, :, :, :] * grp_h[None, None, :, :, None, None]
        s0 = jnp.transpose(s0, (0, 1, 2, 4, 3, 5)).reshape(dec_batch, 2, 2 * SSM_N, GROUP_W)
        s0 = jnp.concatenate([jnp.zeros((batch,) + s0.shape[1:], F32), s0], axis=0)
        ssm_yf, ssm_yb, ssm_fin = _ssd(
            geom, ssm_xbc, small, s0, ssm_conv_w[l], ssm_conv_b[l].reshape(1, -1), alog_row, bias_row,
            jnp.repeat(ssm_d[l], HEAD_DIM).reshape(1, GROUP_W))
        fin = ssm_fin[:batch].reshape(batch, 2, 2, SSM_N, N_HEADS, HEAD_DIM)
        st_ssm.append(jnp.stack([jnp.swapaxes(fin[:, :, h // 2, :, h, :], -1, -2)
                                 for h in range(N_HEADS)], axis=2))

        uq = mla_w_uq[l].reshape(MLA_Q_LORA, N_HEADS, MLA_NOPE + MLA_ROPE)
        wuq = jnp.pad(uq, ((0, 0), (0, 0), (0, LANES - MLA_NOPE - MLA_ROPE)))
        wuq = wuq.reshape(MLA_Q_LORA, N_HEADS * LANES).astype(BF16)
        ukv = mla_w_ukv[l].reshape(MLA_KV_LORA, N_HEADS, MLA_NOPE + HEAD_DIM)
        wk = jnp.pad(ukv[:, :, :MLA_NOPE], ((0, 0), (0, 0), (0, LANES - MLA_NOPE)))
        wk = wk.reshape(MLA_KV_LORA, N_HEADS * LANES).astype(BF16)
        wv = ukv[:, :, MLA_NOPE:].reshape(MLA_KV_LORA, GROUP_W).astype(BF16)
        qh, kh, vh, ckv = _mla_proj(geom, mla_q, mla_kv, small, *mla_tabs,
                                    mla_q_norm_w[l].reshape(1, -1), mla_kv_norm_w[l].reshape(1, -1),
                                    wuq, wk, wv)
        o_mla_ctx = _mla_attn(qh, kh, vh, batch, seq, 0)
        kpe_c = jnp.pad(cache_mla_kpe[:, l], ((0, 0), (0, 0), (MLA_NOPE, LANES - MLA_NOPE - MLA_ROPE)))
        o_mla_lat = _mla_attn(qh, kh, vh, dec_batch, dec_seq, n_ctx,
                              cache=(cache_mla_ckv[:, l], kpe_c, wk, wv))
        st_ckv.append(ckv[:n_ctx].reshape(batch, seq, MLA_KV_LORA))
        st_kpe.append(small[:n_ctx, SM_KPE:SM_KPE + MLA_ROPE].reshape(batch, seq, MLA_ROPE))

        sink_l = jnp.repeat(swa_sinks[l], HEAD_DIM).reshape(1, GROUP_W)
        o_swa_ctx = _swa_ctx(geom, swa, e_mat, sink_l)
        past = cache_swa_k.shape[2]
        o_swa_lat = _swa_lat(geom, swa, *swa_tabs, cache_swa_k[:, l].reshape(dec_batch, past, LANES),
                             cache_swa_v[:, l].reshape(dec_batch, past, LANES), e_mat, sink_l)
        st_k.append(swa[:n_ctx, GROUP_W:GROUP_W + LANES].reshape(batch, seq, 2, HEAD_DIM))
        st_v.append(swa[:n_ctx, GROUP_W + LANES:].reshape(batch, seq, 2, HEAD_DIM))

        o_mla = jnp.concatenate([o_mla_ctx, o_mla_lat], axis=0)
        o_swa = jnp.concatenate([o_swa_ctx, o_swa_lat], axis=0)
        x = _outproj(geom, x, ada_l, (dn_of, dn_ob, dn_z, ssm_yf, ssm_yb, ssm_z, o_mla, o_swa),
                     jnp.tile(dn_norm_w[l], N_HEADS).reshape(1, GROUP_W),
                     ssm_norm_w[l].reshape(1, GROUP_W), w_out[l].astype(BF16))
        wgu = w_gate_up[l].astype(BF16)
        x = _ffn(geom, x, ada_l, norm2_w[l], wgu[:, :FF_DIM], wgu[:, FF_DIM:], w_down[l].astype(BF16),
                 final_norm_w, l == depth - 1)

    return (x[:n_ctx].reshape(batch, seq, d), x[n_ctx:].reshape(dec_batch, dec_seq, d),
            jnp.stack(st_dn, axis=1), jnp.stack(st_ckv, axis=1), jnp.stack(st_kpe, axis=1),
            jnp.stack(st_ssm, axis=1), jnp.stack(st_k, axis=1), jnp.stack(st_v, axis=1))
```

```python
import functools

import jax
import jax.numpy as jnp
from jax import lax
from jax.experimental import pallas as pl
from jax.experimental.pallas import tpu as pltpu

F32 = jnp.float32
BF16 = jnp.bfloat16

D_MODEL = 1024
GRID_W = 64
HEAD_DIM = 64
GROUP_W = 256
EPS = 1e-6
ROPE_THETA = 10000.0
NEG_INF = -1e30
N_HEADS = 4
MLA_NOPE = 64
MLA_ROPE = 32
MLA_Q_LORA = 256
MLA_KV_LORA = 128
MLA_SCALE = (MLA_NOPE + MLA_ROPE) ** -0.5
SSM_N = 64
SWA_SCALE = HEAD_DIM ** -0.5
SWA_BLOCK = 128
FF_DIM = 2816
FF_CHUNK = 256

LANES = 128
SUBLANES = 8
SEQ_TILE = 256
ROW_TILE = 512
VMEM_LIMIT = 56 * 1024 * 1024

SEG_WIDTHS = (768, 256, 256, 128, 256, 512, 512, 128)
IN_PAD = sum(SEG_WIDTHS)
SM_BETA, SM_ALPHA, SM_DT, SM_KPE = 0, 8, 16, 64


def _sigmoid(x):
    return 1.0 / (1.0 + jnp.exp(-x))


def _silu(x):
    return x * _sigmoid(x)


def _softplus(x):
    return jnp.maximum(x, 0.0) + jnp.log1p(jnp.exp(-jnp.abs(x)))


def _mm(a, b):
    return jnp.dot(a.astype(BF16), b.astype(BF16), preferred_element_type=F32)


def _mm_nt(a, b):
    return lax.dot_general(a.astype(BF16), b.astype(BF16), (((1,), (1,)), ((), ())),
                           preferred_element_type=F32)


def _mm_tn(a, b):
    return lax.dot_general(a.astype(BF16), b.astype(BF16), (((0,), (0,)), ((), ())),
                           preferred_element_type=F32)


def _split(a, parts):
    out = []
    for _ in range(parts):
        hi = a.astype(BF16)
        out.append(hi)
        a = a - hi.astype(F32)
    return out


def _mm_split_lhs(a, b_exact, parts, nt=False):
    dims = (((1,), (1,)), ((), ())) if nt else (((1,), (0,)), ((), ()))
    acc = None
    for piece in _split(a, parts):
        r = lax.dot_general(piece, b_exact, dims, preferred_element_type=F32)
        acc = r if acc is None else acc + r
    return acc


def _mm_split_rhs(a_exact, b, parts):
    acc = None
    for piece in _split(b, parts):
        r = jnp.dot(a_exact, piece, preferred_element_type=F32)
        acc = r if acc is None else acc + r
    return acc


def _ones_where(mask, dtype):
    return jnp.where(mask, 1.0, 0.0).astype(dtype)


def _rms(x, w):
    return x * lax.rsqrt(jnp.mean(x * x, axis=-1, keepdims=True) + EPS) * w


def _cparams(sem):
    return pltpu.CompilerParams(dimension_semantics=sem, vmem_limit_bytes=VMEM_LIMIT)


def _ada_kernel(c_ref, w_ref, b_ref, o_ref):
    o_ref[0] = _mm(_silu(c_ref[...]), w_ref[0]) + b_ref[0]


def _ada(cc, w_ada, b_ada):
    depth, d, n = w_ada.shape
    tn = 1536
    return pl.pallas_call(
        _ada_kernel,
        out_shape=jax.ShapeDtypeStruct((depth, cc.shape[0], n), F32),
        grid=(depth, n // tn),
        in_specs=[pl.BlockSpec(cc.shape, lambda l, j: (0, 0)),
                  pl.BlockSpec((1, d, tn), lambda l, j: (l, 0, j)),
                  pl.BlockSpec((1, 1, tn), lambda l, j: (l, 0, j))],
        out_specs=pl.BlockSpec((1, cc.shape[0], tn), lambda l, j: (l, 0, j)),
        compiler_params=_cparams(("arbitrary", "arbitrary")),
        name="ada",
    )(cc, w_ada, b_ada.reshape(depth, 1, n))


class _Geom:
    def __init__(self, n_ctx_seq, ctx_len, n_lat_seq, lat_len):
        self.n_ctx_seq, self.ctx_len = n_ctx_seq, ctx_len
        self.n_lat_seq, self.lat_len = n_lat_seq, lat_len
        self.n_ctx = n_ctx_seq * ctx_len
        self.n_lat = n_lat_seq * lat_len
        self.n_tok = self.n_ctx + self.n_lat
        assert ctx_len % SEQ_TILE == 0 and lat_len % ROW_TILE == 0 and self.n_ctx % ROW_TILE == 0
        self.cps = ctx_len // SEQ_TILE
        self.lps = lat_len // SEQ_TILE
        self.n_ctx_tiles = n_ctx_seq * self.cps
        self.n_tiles = self.n_ctx_tiles + n_lat_seq * self.lps
        self.n_seq = n_ctx_seq + n_lat_seq

    def mod_row(self, i):
        r = i * ROW_TILE
        return jnp.where(r >= self.n_ctx, 1 + (r - self.n_ctx) // self.lat_len, 0)

    def tile_pos(self, t):
        is_lat = t >= self.n_ctx_tiles
        u = t - self.n_ctx_tiles
        seq = jnp.where(is_lat, self.n_ctx_seq + u // self.lps, t // self.cps)
        pos = jnp.where(is_lat, u % self.lps, t % self.cps)
        nt = jnp.where(is_lat, self.lps, self.cps)
        return seq, pos, nt

    def mirror(self, t):
        _, pos, nt = self.tile_pos(t)
        return t - pos + (nt - 1 - pos)


def _inproj_kernel(x_ref, ada_ref, nw_ref, w_ref, *out_refs):
    h = _rms(x_ref[...], nw_ref[...]) * (1.0 + ada_ref[0, 1:2, :]) + ada_ref[0, 0:1, :]
    h = h.astype(BF16)
    off = 0
    for o_ref in out_refs:
        wd = o_ref.shape[-1]
        o_ref[...] = jnp.dot(h, w_ref[:, off:off + wd], preferred_element_type=F32)
        off += wd


def _inproj(geom, x, ada_l, norm_w, w_pad):
    d = x.shape[1]
    return pl.pallas_call(
        _inproj_kernel,
        out_shape=[jax.ShapeDtypeStruct((geom.n_tok, wd), F32) for wd in SEG_WIDTHS],
        grid=(geom.n_tok // ROW_TILE,),
        in_specs=[pl.BlockSpec((ROW_TILE, d), lambda i: (i, 0)),
                  pl.BlockSpec((1, 6, d), lambda i: (geom.mod_row(i), 0, 0)),
                  pl.BlockSpec((1, d), lambda i: (0, 0)),
                  pl.BlockSpec((d, IN_PAD), lambda i: (0, 0))],
        out_specs=[pl.BlockSpec((ROW_TILE, wd), lambda i: (i, 0)) for wd in SEG_WIDTHS],
        compiler_params=_cparams(("parallel",)),
        name="inproj",
    )(x, ada_l, norm_w.reshape(1, d), w_pad)


def _conv3(x, x_prev, x_next, w_ref):
    n = x.shape[0]
    r = lax.broadcasted_iota(jnp.int32, x.shape, 0)
    x_dn = jnp.where(r == 0, x_prev, pltpu.roll(x, 1, 0))
    x_up = jnp.where(r == n - 1, x_next, pltpu.roll(x, n - 1, 0))
    return w_ref[0:1, :] * x_dn + w_ref[1:2, :] * x + w_ref[2:3, :] * x_up


def _head_lanes(head_l, colfn):
    out = colfn(N_HEADS - 1)
    for h in range(N_HEADS - 2, -1, -1):
        out = jnp.where(head_l == h, colfn(h), out)
    return out


def _cumulative(z, incl_b, parts=3):
    cum = _mm_split_rhs(incl_b, z, parts)
    cumt = _mm_split_lhs(z.T, incl_b, parts, nt=True)
    return cum, cumt


def _unit_tri_inverses(a_list, lvl, eye):
    dot = functools.partial(jnp.dot, preferred_element_type=F32)
    a0 = [jnp.where(lvl < 3, a, 0.0) for a in a_list]
    a0b = [a.astype(BF16) for a in a0]
    x = [eye - a for a in a0]
    p = [dot(a, a) for a in a0b]
    pb = [v.astype(BF16) for v in p]
    x = [xi + dot(xi.astype(BF16), pi) for xi, pi in zip(x, pb)]
    pb = [dot(pi, pi).astype(BF16) for pi in pb]
    xb = [(xi + dot(xi.astype(BF16), pi)).astype(BF16) for xi, pi in zip(x, pb)]
    for m in range(3, 8):
        sel = lvl == m
        am = [jnp.where(sel, a, 0.0).astype(BF16) for a in a_list]
        y = [dot(ai, xi).astype(BF16) for ai, xi in zip(am, xb)]
        xb = [jnp.where(sel, -dot(xi, yi), xi.astype(F32)).astype(BF16) for xi, yi in zip(xb, y)]
    return xb


def _seq_tile_specs(geom, width, mirror):
    rows8 = SEQ_TILE // SUBLANES
    last8 = geom.n_tok // SUBLANES - 1
    tile = (lambda t: geom.mirror(t)) if mirror else (lambda t: t)
    return [pl.BlockSpec((SEQ_TILE, width), lambda t: (tile(t), 0)),
            pl.BlockSpec((SUBLANES, width), lambda t: (jnp.maximum(tile(t) * rows8 - 1, 0), 0)),
            pl.BlockSpec((SUBLANES, width), lambda t: (jnp.minimum((tile(t) + 1) * rows8, last8), 0))]


def _dn_kernel(xf_ref, xfp_ref, xfn_ref, xb_ref, xbp_ref, xbn_ref, gf_ref, gb_ref, s0_ref,
               cw_ref, alog_ref, bias_ref, of_ref, ob_ref, sfin_ref, s_scr, *, geom):
    t = pl.program_id(0)
    _, pos, nt = geom.tile_pos(t)
    tt = SEQ_TILE

    @pl.when(pos == 0)
    def _():
        s_scr[...] = s0_ref[0]

    row = lax.broadcasted_iota(jnp.int32, (tt, tt), 0)
    col = lax.broadcasted_iota(jnp.int32, (tt, tt), 1)
    xr = row ^ col
    lvl = jnp.where(xr >= 2, 1, 0)
    for kbit in range(2, 8):
        lvl = lvl + jnp.where(xr >= (1 << kbit), 1, 0)
    head_l = col // HEAD_DIM
    blockdiag = (row // HEAD_DIM) == head_l
    eye = _ones_where(row == col, F32)
    gones = _ones_where(blockdiag, BF16)
    lane_g = lax.broadcasted_iota(jnp.int32, (tt, LANES), 1)
    not_first, not_last = pos > 0, pos < nt - 1

    a_list, per_dir = [], []
    dirs = ((xf_ref, xfp_ref, xfn_ref, gf_ref, of_ref, not_first, not_last),
            (xb_ref, xbp_ref, xbn_ref, gb_ref, ob_ref, not_last, not_first))
    for d, (x_ref, xp_ref, xn_ref, g_ref, o_ref, has_prev, has_next) in enumerate(dirs):
        x_prev = jnp.where(has_prev, xp_ref[SUBLANES - 1:SUBLANES, :], 0.0)
        x_next = jnp.where(has_next, xn_ref[0:1, :], 0.0)
        y = _silu(_conv3(x_ref[...], x_prev, x_next, cw_ref))
        q, k, v = y[:, :GROUP_W], y[:, GROUP_W:2 * GROUP_W], y[:, 2 * GROUP_W:]
        q = q * lax.rsqrt(_mm_split_lhs(q * q, gones, 2) + EPS) * (HEAD_DIM ** -0.5)
        k = k * lax.rsqrt(_mm_split_lhs(k * k, gones, 2) + EPS)

        s = g_ref[...]
        gate = -jnp.exp(alog_ref[...]) * _softplus(s + bias_ref[...])
        z = jnp.where(lane_g < SM_ALPHA, _sigmoid(s), gate)
        incl = (row >= col) if d == 0 else (row <= col)
        strict = (row > col) if d == 0 else (row < col)
        cum, cumt = _cumulative(z, _ones_where(incl, BF16))
        edge = tt - 1 if d == 0 else 0
        c_beta = SM_BETA + N_HEADS * d
        c_g = SM_ALPHA + N_HEADS * d

        beta_l = _head_lanes(head_l, lambda h: z[:, c_beta + h:c_beta + h + 1])
        cum_l = _head_lanes(head_l, lambda h: cum[:, c_g + h:c_g + h + 1])
        tot_l = _head_lanes(head_l[0:1, :], lambda h: cum[edge:edge + 1, c_g + h:c_g + h + 1])
        eg = jnp.exp(cum_l)
        kb = k * beta_l
        rhs = jnp.concatenate([v * beta_l, kb * eg], axis=1).astype(BF16)
        kbf = k.astype(BF16)
        qk_heads = []
        for h in range(N_HEADS):
            hm = head_l == h
            diff = cum[:, c_g + h:c_g + h + 1] - cumt[c_g + h:c_g + h + 1, :]
            decay = jnp.where(incl, jnp.exp(jnp.where(incl, diff, 0.0)), 0.0)
            a_list.append(jnp.where(strict, _mm_nt(jnp.where(hm, kb, 0.0), kbf) * decay, 0.0))
            qk_heads.append((_mm_nt(jnp.where(hm, q, 0.0), kbf) * decay).astype(BF16))
        per_dir.append((rhs, qk_heads, (q * eg).astype(BF16), k * jnp.exp(tot_l - cum_l),
                        jnp.exp(tot_l), o_ref))

    x_list = _unit_tri_inverses(a_list, lvl, eye)

    for d, (rhs, qk_heads, qg, kd, e_tot, o_ref) in enumerate(per_dir):
        u_all = jnp.zeros((tt, GROUP_W), F32)
        w_all = jnp.zeros((tt, GROUP_W), F32)
        for h in range(N_HEADS):
            hm = head_l == h
            uw = jnp.dot(x_list[N_HEADS * d + h], rhs, preferred_element_type=F32)
            u_all = jnp.where(hm, uw[:, :GROUP_W], u_all)
            w_all = jnp.where(hm, uw[:, GROUP_W:], w_all)
        state = s_scr[d]
        sb = state.astype(BF16)
        v_new = u_all - _mm(w_all, sb)
        vb = v_new.astype(BF16)
        o = jnp.dot(qg, sb, preferred_element_type=F32)
        for h in range(N_HEADS):
            o = o + jnp.where(head_l == h, jnp.dot(qk_heads[h], vb, preferred_element_type=F32), 0.0)
        o_ref[...] = o
        s_scr[d] = state * e_tot + jnp.where(blockdiag, _mm_tn(kd, vb), 0.0)

    sfin_ref[0] = s_scr[...]


def _deltanet(geom, qkv, small, s0, conv_w, alog_row, bias_row):
    tt = SEQ_TILE
    row_spec = lambda shape: pl.BlockSpec(shape, lambda t: (0,) * len(shape))
    seq_of = lambda t: geom.tile_pos(t)[0]
    in_specs = (_seq_tile_specs(geom, 3 * GROUP_W, False) + _seq_tile_specs(geom, 3 * GROUP_W, True)
                + [pl.BlockSpec((tt, LANES), lambda t: (t, 0)),
                   pl.BlockSpec((tt, LANES), lambda t: (geom.mirror(t), 0)),
                   pl.BlockSpec((1, 2, GROUP_W, GROUP_W), lambda t: (seq_of(t), 0, 0, 0)),
                   row_spec(conv_w.shape), row_spec((1, LANES)), row_spec((1, LANES))])
    return pl.pallas_call(
        functools.partial(_dn_kernel, geom=geom),
        out_shape=[jax.ShapeDtypeStruct((geom.n_tok, GROUP_W), F32),
                   jax.ShapeDtypeStruct((geom.n_tok, GROUP_W), F32),
                   jax.ShapeDtypeStruct((geom.n_seq, 2, GROUP_W, GROUP_W), F32)],
        grid=(geom.n_tiles,),
        in_specs=in_specs,
        out_specs=[pl.BlockSpec((tt, GROUP_W), lambda t: (t, 0)),
                   pl.BlockSpec((tt, GROUP_W), lambda t: (geom.mirror(t), 0)),
                   pl.BlockSpec((1, 2, GROUP_W, GROUP_W), lambda t: (seq_of(t), 0, 0, 0))],
        scratch_shapes=[pltpu.VMEM((2, GROUP_W, GROUP_W), F32)],
        compiler_params=_cparams(("arbitrary",)),
        name="deltanet",
    )(qkv, qkv, qkv, qkv, qkv, qkv, small, small, s0, conv_w, alog_row, bias_row)


def _ssm_kernel(xf_ref, xfp_ref, xfn_ref, xb_ref, xbp_ref, xbn_ref, gf_ref, gb_ref, s0_ref,
                cw_ref, cb_ref, alog_ref, bias_ref, dvec_ref, yf_ref, yb_ref, sfin_ref, s_scr,
                *, geom):
    t = pl.program_id(0)
    _, pos, nt = geom.tile_pos(t)
    tt = SEQ_TILE

    @pl.when(pos == 0)
    def _():
        s_scr[...] = s0_ref[0]

    row = lax.broadcasted_iota(jnp.int32, (tt, tt), 0)
    col = lax.broadcasted_iota(jnp.int32, (tt, tt), 1)
    head_l = col // HEAD_DIM
    lane_g = lax.broadcasted_iota(jnp.int32, (tt, LANES), 1)
    group_l = lane_g // SSM_N
    srow = lax.broadcasted_iota(jnp.int32, (2 * SSM_N, GROUP_W), 0)
    scol = lax.broadcasted_iota(jnp.int32, (2 * SSM_N, GROUP_W), 1)
    state_mask = (srow // SSM_N) == (scol // HEAD_DIM) // 2
    not_first, not_last = pos > 0, pos < nt - 1

    dirs = ((xf_ref, xfp_ref, xfn_ref, gf_ref, yf_ref, not_first, not_last),
            (xb_ref, xbp_ref, xbn_ref, gb_ref, yb_ref, not_last, not_first))
    for d, (x_ref, xp_ref, xn_ref, g_ref, y_ref, has_prev, has_next) in enumerate(dirs):
        x_prev = jnp.where(has_prev, xp_ref[SUBLANES - 1:SUBLANES, :], 0.0)
        x_next = jnp.where(has_next, xn_ref[0:1, :], 0.0)
        y = _silu(_conv3(x_ref[...], x_prev, x_next, cw_ref) + cb_ref[...])
        xs, bm, cm = y[:, :GROUP_W], y[:, GROUP_W:GROUP_W + 2 * SSM_N], y[:, GROUP_W + 2 * SSM_N:]

        dt = _softplus(g_ref[...] + bias_ref[...])
        a = -jnp.exp(alog_ref[...]) * dt
        incl = (row >= col) if d == 0 else (row <= col)
        cum, cumt = _cumulative(a, _ones_where(incl, BF16))
        edge = tt - 1 if d == 0 else 0
        c0 = SM_DT + N_HEADS * d

        dt_l = _head_lanes(head_l, lambda h: dt[:, c0 + h:c0 + h + 1])
        cum_l = _head_lanes(head_l, lambda h: cum[:, c0 + h:c0 + h + 1])
        tot_l = _head_lanes(head_l[0:1, :], lambda h: cum[edge:edge + 1, c0 + h:c0 + h + 1])
        xdt = xs * dt_l
        cb_scores = [_mm_nt(jnp.where(group_l == g, cm, 0.0), bm) for g in range(2)]
        out = jnp.zeros((tt, GROUP_W), F32)
        for h in range(N_HEADS):
            diff = cum[:, c0 + h:c0 + h + 1] - cumt[c0 + h:c0 + h + 1, :]
            lmat = jnp.where(incl, jnp.exp(jnp.where(incl, diff, 0.0)), 0.0)
            out = jnp.where(head_l == h, _mm(cb_scores[h // 2] * lmat, xdt), out)
        state = s_scr[d]
        out = out + _mm(cm, state) * jnp.exp(cum_l)
        if d == 0:
            out = out + dvec_ref[...] * xs
        y_ref[...] = out
        s_scr[d] = (state * jnp.exp(tot_l)
                    + jnp.where(state_mask, _mm_tn(bm, xdt * jnp.exp(tot_l - cum_l)), 0.0))

    sfin_ref[0] = s_scr[...]


def _ssd(geom, xbc, small, s0, conv_w, conv_b, alog_row, bias_row, dvec):
    tt = SEQ_TILE
    wx = GROUP_W + 4 * SSM_N
    row_spec = lambda shape: pl.BlockSpec(shape, lambda t: (0,) * len(shape))
    seq_of = lambda t: geom.tile_pos(t)[0]
    in_specs = (_seq_tile_specs(geom, wx, False) + _seq_tile_specs(geom, wx, True)
                + [pl.BlockSpec((tt, LANES), lambda t: (t, 0)),
                   pl.BlockSpec((tt, LANES), lambda t: (geom.mirror(t), 0)),
                   pl.BlockSpec((1, 2, 2 * SSM_N, GROUP_W), lambda t: (seq_of(t), 0, 0, 0)),
                   row_spec(conv_w.shape), row_spec((1, wx)), row_spec((1, LANES)),
                   row_spec((1, LANES)), row_spec((1, GROUP_W))])
    return pl.pallas_call(
        functools.partial(_ssm_kernel, geom=geom),
        out_shape=[jax.ShapeDtypeStruct((geom.n_tok, GROUP_W), F32),
                   jax.ShapeDtypeStruct((geom.n_tok, GROUP_W), F32),
                   jax.ShapeDtypeStruct((geom.n_seq, 2, 2 * SSM_N, GROUP_W), F32)],
        grid=(geom.n_tiles,),
        in_specs=in_specs,
        out_specs=[pl.BlockSpec((tt, GROUP_W), lambda t: (t, 0)),
                   pl.BlockSpec((tt, GROUP_W), lambda t: (geom.mirror(t), 0)),
                   pl.BlockSpec((1, 2, 2 * SSM_N, GROUP_W), lambda t: (seq_of(t), 0, 0, 0))],
        scratch_shapes=[pltpu.VMEM((2, 2 * SSM_N, GROUP_W), F32)],
        compiler_params=_cparams(("arbitrary",)),
        name="ssd",
    )(xbc, xbc, xbc, xbc, xbc, xbc, small, small, s0, conv_w, conv_b, alog_row, bias_row, dvec)


def _rope_slab(x, cos, sin_a, sin_b, half):
    w = x.shape[-1]
    return x * cos + pltpu.roll(x, w - half, 1) * sin_a + pltpu.roll(x, half, 1) * sin_b


def _mla_proj_kernel(ql_ref, kvl_ref, sm_ref, cos_ref, sa_ref, sb_ref, qnw_ref, kvnw_ref,
                     wuq_ref, wk_ref, wv_ref, qh_ref, kh_ref, vh_ref, ckv_ref):
    cos, sa, sb = cos_ref[...], sa_ref[...], sb_ref[...]
    half = MLA_ROPE // 2
    qp = _mm(_rms(ql_ref[...], qnw_ref[...]), wuq_ref[...]) * MLA_SCALE
    ckv = _rms(kvl_ref[...], kvnw_ref[...])
    ckv_ref[...] = ckv
    lane = lax.broadcasted_iota(jnp.int32, cos.shape, 1)
    is_pe = (lane >= MLA_NOPE) & (lane < MLA_NOPE + MLA_ROPE)
    kpe = jnp.where(is_pe, _rope_slab(sm_ref[...], cos, sa, sb, half), 0.0)
    kp = _mm(ckv, wk_ref[...])
    for h in range(N_HEADS):
        sl = slice(h * LANES, (h + 1) * LANES)
        qh_ref[:, sl] = _rope_slab(qp[:, sl], cos, sa, sb, half).astype(BF16)
        kh_ref[:, sl] = (kp[:, sl] + kpe).astype(BF16)
    vh_ref[...] = _mm(ckv, wv_ref[...]).astype(BF16)


def _mla_proj(geom, q_lat, kv_lat, small, cos, sa, sb, qnw, kvnw, wuq, wk, wv):
    tm = ROW_TILE
    tok = lambda w: pl.BlockSpec((tm, w), lambda i: (i, 0))
    full = lambda a: pl.BlockSpec(a.shape, lambda i: (0,) * a.ndim)
    return pl.pallas_call(
        _mla_proj_kernel,
        out_shape=[jax.ShapeDtypeStruct((geom.n_tok, N_HEADS * LANES), BF16),
                   jax.ShapeDtypeStruct((geom.n_tok, N_HEADS * LANES), BF16),
                   jax.ShapeDtypeStruct((geom.n_tok, GROUP_W), BF16),
                   jax.ShapeDtypeStruct((geom.n_tok, MLA_KV_LORA), F32)],
        grid=(geom.n_tok // tm,),
        in_specs=[tok(MLA_Q_LORA), tok(MLA_KV_LORA), tok(LANES), tok(LANES), tok(LANES), tok(LANES),
                  full(qnw), full(kvnw), full(wuq), full(wk), full(wv)],
        out_specs=[tok(N_HEADS * LANES), tok(N_HEADS * LANES), tok(GROUP_W), tok(MLA_KV_LORA)],
        compiler_params=_cparams(("parallel",)),
        name="mla_proj",
    )(q_lat, kv_lat, small, cos, sa, sb, qnw, kvnw, wuq, wk, wv)


def _mla_attn_kernel(*refs, has_cache):
    if has_cache:
        q_ref, k_ref, v_ref, ckv_ref, kpe_ref, wk_ref, wv_ref, o_ref = refs
        ckv_c = ckv_ref[0]
        v_c = _mm(ckv_c, wv_ref[...])
    else:
        q_ref, k_ref, v_ref, o_ref = refs
    v = v_ref[...]
    outs = []
    for j in range(2):
        sl = slice(j * LANES, (j + 1) * LANES)
        q = q_ref[:, sl]
        s = _mm_nt(q, k_ref[:, sl])
        m = jnp.max(s, axis=-1, keepdims=True)
        if has_cache:
            k_c = _mm(ckv_c, wk_ref[:, sl]) + kpe_ref[0]
            s_c = _mm_nt(q, k_c)
            m = jnp.maximum(m, jnp.max(s_c, axis=-1, keepdims=True))
            p_c = jnp.exp(s_c - m)
        p = jnp.exp(s - m)
        den = jnp.sum(p, axis=-1, keepdims=True)
        acc = _mm(p, v)
        if has_cache:
            den = den + jnp.sum(p_c, axis=-1, keepdims=True)
            acc = acc + _mm(p_c, v_c)
        outs.append(acc / den)
    lane = lax.broadcasted_iota(jnp.int32, outs[0].shape, 1)
    o_ref[...] = jnp.where(lane < HEAD_DIM, outs[0], outs[1])


def _mla_attn(qh, kh, vh, n_seq, seq_len, tok0, cache=None):
    tq = SEQ_TILE
    nq = seq_len // tq
    q0, k0 = tok0 // tq, tok0 // seq_len
    assert tok0 % seq_len == 0
    in_specs = [pl.BlockSpec((tq, 2 * LANES), lambda b, hp, i: (q0 + b * nq + i, hp)),
                pl.BlockSpec((seq_len, 2 * LANES), lambda b, hp, i: (k0 + b, hp)),
                pl.BlockSpec((seq_len, LANES), lambda b, hp, i: (k0 + b, hp))]
    args = [qh, kh, vh]
    if cache is not None:
        ckv_c, kpe_c, wk, wv = cache
        past = ckv_c.shape[1]
        in_specs += [pl.BlockSpec((1, past, MLA_KV_LORA), lambda b, hp, i: (b, 0, 0)),
                     pl.BlockSpec((1, past, LANES), lambda b, hp, i: (b, 0, 0)),
                     pl.BlockSpec((MLA_KV_LORA, 2 * LANES), lambda b, hp, i: (0, hp)),
                     pl.BlockSpec((MLA_KV_LORA, LANES), lambda b, hp, i: (0, hp))]
        args += [ckv_c, kpe_c, wk, wv]
    return pl.pallas_call(
        functools.partial(_mla_attn_kernel, has_cache=cache is not None),
        out_shape=jax.ShapeDtypeStruct((n_seq * seq_len, GROUP_W), F32),
        grid=(n_seq, 2, nq),
        in_specs=in_specs,
        out_specs=pl.BlockSpec((tq, LANES), lambda b, hp, i: (b * nq + i, hp)),
        compiler_params=_cparams(("parallel", "parallel", "arbitrary")),
        name="mla_attn_lat" if cache is not None else "mla_attn_ctx",
    )(*args)


def _swa_core(q, k_all, v_all, e_ref, sink_ref, valid):
    r = q.shape[0]
    e = e_ref[...]
    kx, vx = _mm(k_all, e), _mm(v_all, e)
    head_l = lax.broadcasted_iota(jnp.int32, q.shape, 1) // HEAD_DIM
    qs = jnp.concatenate([jnp.where(head_l == h, q, 0.0) for h in range(N_HEADS)], axis=0)
    s = _mm_nt(qs, kx)
    if valid is not None:
        s = jnp.where(jnp.concatenate([valid] * N_HEADS, axis=0), s, NEG_INF)
    row_head = lax.broadcasted_iota(jnp.int32, (N_HEADS * r, 1), 0) // r
    sink = _head_lanes(row_head, lambda h: sink_ref[:, h * HEAD_DIM:h * HEAD_DIM + 1])
    m = jnp.maximum(jnp.max(s, axis=-1, keepdims=True), sink)
    p = jnp.exp(s - m)
    den = jnp.sum(p, axis=-1, keepdims=True) + jnp.exp(sink - m)
    o4 = _mm(p, vx) / den
    out = o4[(N_HEADS - 1) * r:]
    for h in range(N_HEADS - 2, -1, -1):
        out = jnp.where(head_l == h, o4[h * r:(h + 1) * r], out)
    return out


def _swa_ctx_kernel(x_ref, e_ref, sink_ref, o_ref):
    x = x_ref[...]
    q = x[:, :GROUP_W] * SWA_SCALE
    k, v = x[:, GROUP_W:GROUP_W + LANES], x[:, GROUP_W + LANES:]
    o_ref[...] = _swa_core(q, k, v, e_ref, sink_ref, None)


def _swa_ctx(geom, swa, e_mat, sink_l):
    t = geom.ctx_len
    return pl.pallas_call(
        _swa_ctx_kernel,
        out_shape=jax.ShapeDtypeStruct((geom.n_ctx, GROUP_W), F32),
        grid=(geom.n_ctx_seq,),
        in_specs=[pl.BlockSpec((t, 2 * GROUP_W), lambda b: (b, 0)),
                  pl.BlockSpec(e_mat.shape, lambda b: (0, 0)),
                  pl.BlockSpec((1, GROUP_W), lambda b: (0, 0))],
        out_specs=pl.BlockSpec((t, GROUP_W), lambda b: (b, 0)),
        compiler_params=_cparams(("parallel",)),
        name="swa_ctx",
    )(swa, e_mat, sink_l)


def _swa_lat_kernel(xc_ref, xp_ref, xn_ref, cc_ref, ac_ref, bc_ref, cp_ref, ap_ref, bp_ref,
                    cn_ref, an_ref, bn_ref, kc_ref, vc_ref, e_ref, sink_ref, o_ref, *, nblk):
    i = pl.program_id(1)
    blk = SWA_BLOCK
    half = HEAD_DIM // 2

    def rope(x, c_ref, a_ref, b_ref):
        reps = x.shape[1] // LANES
        wide = lambda t_ref: jnp.concatenate([t_ref[...]] * reps, axis=1) if reps > 1 else t_ref[...]
        return _rope_slab(x, wide(c_ref), wide(a_ref), wide(b_ref), half)

    ksl, vsl = slice(GROUP_W, GROUP_W + LANES), slice(GROUP_W + LANES, 2 * GROUP_W)
    q = rope(xc_ref[:, :GROUP_W], cc_ref, ac_ref, bc_ref) * SWA_SCALE
    k_all = jnp.concatenate([rope(xp_ref[:, ksl], cp_ref, ap_ref, bp_ref),
                             rope(xc_ref[:, ksl], cc_ref, ac_ref, bc_ref),
                             rope(xn_ref[:, ksl], cn_ref, an_ref, bn_ref),
                             kc_ref[0]], axis=0)
    v_all = jnp.concatenate([xp_ref[:, vsl], xc_ref[:, vsl], xn_ref[:, vsl], vc_ref[0]], axis=0)
    nk = k_all.shape[0]
    r = lax.broadcasted_iota(jnp.int32, (blk, nk), 0)
    c = lax.broadcasted_iota(jnp.int32, (blk, nk), 1)
    in_prev = (c < blk) & (c >= r) & (i > 0)
    in_cur = (c >= blk) & (c < 2 * blk)
    in_next = (c >= 2 * blk) & (c < 3 * blk) & (c - 2 * blk <= r) & (i < nblk - 1)
    valid = in_prev | in_cur | in_next | (c >= 3 * blk)
    o_ref[...] = _swa_core(q, k_all, v_all, e_ref, sink_ref, valid)


def _swa_lat(geom, swa, cos, sa, sb, k_cache, v_cache, e_mat, sink_l):
    blk = SWA_BLOCK
    nblk = geom.lat_len // blk
    b0 = geom.n_ctx // blk
    past = k_cache.shape[1]
    cur = lambda b, i: i
    prv = lambda b, i: jnp.maximum(i - 1, 0)
    nxt = lambda b, i: jnp.minimum(i + 1, nblk - 1)
    xspec = lambda f: pl.BlockSpec((blk, 2 * GROUP_W), lambda b, i: (b0 + b * nblk + f(b, i), 0))
    tspec = lambda f: pl.BlockSpec((blk, LANES), lambda b, i: (f(b, i), 0))
    return pl.pallas_call(
        functools.partial(_swa_lat_kernel, nblk=nblk),
        out_shape=jax.ShapeDtypeStruct((geom.n_lat, GROUP_W), F32),
        grid=(geom.n_lat_seq, nblk),
        in_specs=[xspec(cur), xspec(prv), xspec(nxt),
                  tspec(cur), tspec(cur), tspec(cur), tspec(prv), tspec(prv), tspec(prv),
                  tspec(nxt), tspec(nxt), tspec(nxt),
                  pl.BlockSpec((1, past, LANES), lambda b, i: (b, 0, 0)),
                  pl.BlockSpec((1, past, LANES), lambda b, i: (b, 0, 0)),
                  pl.BlockSpec(e_mat.shape, lambda b, i: (0, 0)),
                  pl.BlockSpec((1, GROUP_W), lambda b, i: (0, 0))],
        out_specs=pl.BlockSpec((blk, GROUP_W), lambda b, i: (b * nblk + i, 0)),
        compiler_params=_cparams(("parallel", "arbitrary")),
        name="swa_lat",
    )(swa, swa, swa, cos, sa, sb, cos, sa, sb, cos, sa, sb, k_cache, v_cache, e_mat, sink_l)


def _outproj_kernel(x_ref, ada_ref, dof_ref, dob_ref, dz_ref, syf_ref, syb_ref, sz_ref,
                    omla_ref, oswa_ref, dnw_ref, snw_ref, w_ref, o_ref):
    tm = x_ref.shape[0]
    row = lax.broadcasted_iota(jnp.int32, (GROUP_W, GROUP_W), 0)
    col = lax.broadcasted_iota(jnp.int32, (GROUP_W, GROUP_W), 1)
    gones = _ones_where((row // HEAD_DIM) == (col // HEAD_DIM), BF16)
    o = dof_ref[...] + dob_ref[...]
    ms = _mm_split_lhs(o * o, gones, 2) * (1.0 / HEAD_DIM)
    dn = o * lax.rsqrt(ms + EPS) * dnw_ref[...] * _silu(dz_ref[...])
    acc = _mm(dn, w_ref[0:GROUP_W, :])
    acc = acc + _mm(omla_ref[...], w_ref[GROUP_W:2 * GROUP_W, :])
    y = (syf_ref[...] + syb_ref[...]) * _silu(sz_ref[...])
    for g in range(2):
        sl = slice(g * LANES, (g + 1) * LANES)
        acc = acc + _mm(_rms(y[:, sl], snw_ref[:, sl]),
                        w_ref[2 * GROUP_W + g * LANES:2 * GROUP_W + (g + 1) * LANES, :])
    acc = acc + _mm(oswa_ref[...], w_ref[3 * GROUP_W:, :])
    o_ref[...] = x_ref[...] + ada_ref[0, 2:3, :] * acc


def _outproj(geom, x, ada_l, parts, dnw, snw, w_out):
    tm = ROW_TILE
    d = x.shape[1]
    tok = lambda w: pl.BlockSpec((tm, w), lambda i: (i, 0))
    full = lambda a: pl.BlockSpec(a.shape, lambda i: (0,) * a.ndim)
    return pl.pallas_call(
        _outproj_kernel,
        out_shape=jax.ShapeDtypeStruct(x.shape, F32),
        grid=(geom.n_tok // tm,),
        in_specs=[tok(d), pl.BlockSpec((1, 6, d), lambda i: (geom.mod_row(i), 0, 0))]
        + [tok(GROUP_W)] * 8 + [full(dnw), full(snw), full(w_out)],
        out_specs=tok(d),
        compiler_params=_cparams(("parallel",)),
        name="outproj",
    )(x, ada_l, *parts, dnw, snw, w_out)


def _ffn_kernel(x_ref, ada_ref, nw_ref, wg_ref, wu_ref, wd_ref, fw_ref, o_ref, *, final_norm):
    x = x_ref[...]
    h = (_rms(x, nw_ref[...]) * (1.0 + ada_ref[0, 4:5, :]) + ada_ref[0, 3:4, :]).astype(BF16)
    acc = jnp.zeros(x.shape, F32)
    for c in range(FF_DIM // FF_CHUNK):
        sl = slice(c * FF_CHUNK, (c + 1) * FF_CHUNK)
        g = jnp.dot(h, wg_ref[:, sl], preferred_element_type=F32)
        u = jnp.dot(h, wu_ref[:, sl], preferred_element_type=F32)
        acc = acc + _mm(_silu(g) * u, wd_ref[sl, :])
    y = x + ada_ref[0, 5:6, :] * acc
    o_ref[...] = _rms(y, fw_ref[...]) if final_norm else y


def _ffn(geom, x, ada_l, norm_w, w_gate, w_up, w_down, final_w, final_norm):
    tm = ROW_TILE
    d = x.shape[1]
    resident = lambda a: pl.BlockSpec(a.shape, lambda i: (0,) * a.ndim, pipeline_mode=pl.Buffered(1))
    return pl.pallas_call(
        functools.partial(_ffn_kernel, final_norm=final_norm),
        out_shape=jax.ShapeDtypeStruct(x.shape, F32),
        grid=(geom.n_tok // tm,),
        in_specs=[pl.BlockSpec((tm, d), lambda i: (i, 0)),
                  pl.BlockSpec((1, 6, d), lambda i: (geom.mod_row(i), 0, 0)),
                  pl.BlockSpec((1, d), lambda i: (0, 0)),
                  resident(w_gate), resident(w_up), resident(w_down),
                  pl.BlockSpec((1, d), lambda i: (0, 0))],
        out_specs=pl.BlockSpec((tm, d), lambda i: (i, 0)),
        compiler_params=_cparams(("parallel",)),
        name="ffn",
    )(x, ada_l, norm_w.reshape(1, d), w_gate, w_up, w_down, final_w.reshape(1, d))


def _pad_cols(a, width):
    return jnp.pad(a, ((0, 0), (0, width - a.shape[1])))


def _w_in_layout(w):
    dn, mla, ssm, swa = 0, 1040, 1456, 2232
    small = jnp.concatenate([
        w[:, dn + 1024:dn + 1040],
        w[:, ssm + 768:ssm + 776],
        jnp.zeros((w.shape[0], SM_KPE - 24), w.dtype),
        w[:, mla + 384:mla + 416],
        jnp.zeros((w.shape[0], LANES - SM_KPE - MLA_ROPE), w.dtype)], axis=1)
    return jnp.concatenate([
        w[:, dn:dn + 768], w[:, dn + 768:dn + 1024],
        w[:, mla:mla + 256], w[:, mla + 256:mla + 384],
        w[:, ssm:ssm + 256], w[:, ssm + 256:ssm + 768],
        w[:, swa:swa + 512], small], axis=1).astype(BF16)


def _gate_row(dn_vec, ssm_vec):
    row = jnp.zeros((1, LANES), F32)
    row = row.at[0, SM_ALPHA:SM_ALPHA + 8].set(dn_vec.reshape(8))
    return row.at[0, SM_DT:SM_DT + 8].set(ssm_vec.reshape(8))


def _axial_angles(rows, rot_dim):
    row_ids = jnp.broadcast_to(jnp.arange(rows)[:, None], (rows, GRID_W)).reshape(-1).astype(F32)
    col_ids = jnp.broadcast_to(jnp.arange(GRID_W)[None, :], (rows, GRID_W)).reshape(-1).astype(F32)
    n_freq = rot_dim // 4
    inv_freq = ROPE_THETA ** (-jnp.arange(n_freq, dtype=F32) / n_freq)
    return jnp.concatenate([row_ids[:, None] * inv_freq, col_ids[:, None] * inv_freq], axis=-1)


def _rope_tables(ang, lane0, reps, n_ident):
    n, half = ang.shape
    cos, sin = jnp.cos(ang), jnp.sin(ang)
    zeros = jnp.zeros_like(sin)
    period = LANES // reps

    def table(first, second, fill):
        one = jnp.concatenate([jnp.full((n, lane0), fill, F32), first, second,
                               jnp.full((n, period - lane0 - 2 * half), fill, F32)], axis=1)
        tab = jnp.concatenate([one] * reps, axis=1)
        ident = jnp.full((n_ident, LANES), fill, F32)
        return jnp.concatenate([ident, tab], axis=0)

    return table(cos, cos, 1.0), table(-sin, zeros, 0.0), table(zeros, sin, 0.0)


def kernel(x_prompt, x_sample, c, state_dn, cache_mla_ckv, cache_mla_kpe, state_ssm, cache_swa_k,
           cache_swa_v, c_ctx, norm1_w, norm2_w, w_ada, b_ada, w_in, w_out, dn_conv_w, dn_a_log,
           dn_dt_bias, dn_norm_w, mla_q_norm_w, mla_w_uq, mla_kv_norm_w, mla_w_ukv, ssm_conv_w,
           ssm_conv_b, ssm_a_log, ssm_dt_bias, ssm_d, ssm_norm_w, swa_sinks, w_gate_up, w_down,
           final_norm_w):
    batch, seq, d = x_prompt.shape
    dec_batch, dec_seq, _ = x_sample.shape
    depth = w_in.shape[0]
    geom = _Geom(batch, seq, dec_batch, dec_seq)
    n_ctx = geom.n_ctx

    x = jnp.concatenate([x_prompt.reshape(n_ctx, d), x_sample.reshape(geom.n_lat, d)], axis=0)
    n_mod = -(-(1 + dec_batch) // SUBLANES) * SUBLANES
    cc = jnp.concatenate([c_ctx[None], c, jnp.zeros((n_mod - 1 - dec_batch, d), F32)], axis=0)
    ada = _ada(cc, w_ada, b_ada).reshape(depth, n_mod, 6, d)

    ang_m = _axial_angles(dec_seq // GRID_W, MLA_ROPE)
    ang_m = jnp.concatenate([ang_m] * dec_batch, axis=0)
    mla_tabs = _rope_tables(ang_m, MLA_NOPE, 1, n_ctx)
    swa_tabs = _rope_tables(_axial_angles(dec_seq // GRID_W, HEAD_DIM), 0, 2, 0)

    lane = jnp.arange(GROUP_W)
    e_mat = (jnp.arange(LANES)[:, None] == (lane // LANES) * HEAD_DIM + lane % HEAD_DIM).astype(BF16)
    eye_h = jnp.eye(N_HEADS, dtype=F32)
    grp_h = (jnp.arange(2)[:, None] == jnp.arange(N_HEADS)[None, :] // 2).astype(F32)

    st_dn, st_ckv, st_kpe, st_ssm, st_k, st_v = [], [], [], [], [], []
    for l in range(depth):
        ada_l = ada[l]
        segs = _inproj(geom, x, ada_l, norm1_w[l], _w_in_layout(w_in[l]))
        dn_qkv, dn_z, mla_q, mla_kv, ssm_z, ssm_xbc, swa, small = segs

        s0 = state_dn[:, l][:, :, :, :, None, :] * eye_h[None, None, :, None, :, None]
        s0 = s0.reshape(dec_batch, 2, GROUP_W, GROUP_W)
        s0 = jnp.concatenate([jnp.zeros((batch,) + s0.shape[1:], F32), s0], axis=0)
        alog_row = _gate_row(dn_a_log[l], ssm_a_log[l])
        bias_row = _gate_row(dn_dt_bias[l], ssm_dt_bias[l])
        dn_of, dn_ob, dn_fin = _deltanet(geom, dn_qkv, small, s0, dn_conv_w[l], alog_row, bias_row)
        fin = dn_fin[:batch].reshape(batch, 2, N_HEADS, HEAD_DIM, N_HEADS, HEAD_DIM)
        st_dn.append(jnp.stack([fin[:, :, h, :, h, :] for h in range(N_HEADS)], axis=2))

        s0 = jnp.swapaxes(state_ssm[:, l], -1, -2)
        s0 = s0[:, :, None, :, :, :] * grp_h[None, None, :, :, None, None]
        s0 = jnp.transpose(s0, (0, 1, 2, 4, 3, 5)).reshape(dec_batch, 2, 2 * SSM_N, GROUP_W)
        s0 = jnp.concatenate([jnp.zeros((batch,) + s0.shape[1:], F32), s0], axis=0)
        ssm_yf, ssm_yb, ssm_fin = _ssd(
            geom, ssm_xbc, small, s0, ssm_conv_w[l], ssm_conv_b[l].reshape(1, -1), alog_row, bias_row,
            jnp.repeat(ssm_d[l], HEAD_DIM).reshape(1, GROUP_W))
        fin = ssm_fin[:batch].reshape(batch, 2, 2, SSM_N, N_HEADS, HEAD_DIM)
        st_ssm.append(jnp.stack([jnp.swapaxes(fin[:, :, h // 2, :, h, :], -1, -2)
                                 for h in range(N_HEADS)], axis=2))

        uq = mla_w_uq[l].reshape(MLA_Q_LORA, N_HEADS, MLA_NOPE + MLA_ROPE)
        wuq = jnp.pad(uq, ((0, 0), (0, 0), (0, LANES - MLA_NOPE - MLA_ROPE)))
        wuq = wuq.reshape(MLA_Q_LORA, N_HEADS * LANES).astype(BF16)
        ukv = mla_w_ukv[l].reshape(MLA_KV_LORA, N_HEADS, MLA_NOPE + HEAD_DIM)
        wk = jnp.pad(ukv[:, :, :MLA_NOPE], ((0, 0), (0, 0), (0, LANES - MLA_NOPE)))
        wk = wk.reshape(MLA_KV_LORA, N_HEADS * LANES).astype(BF16)
        wv = ukv[:, :, MLA_NOPE:].reshape(MLA_KV_LORA, GROUP_W).astype(BF16)
        qh, kh, vh, ckv = _mla_proj(geom, mla_q, mla_kv, small, *mla_tabs,
                                    mla_q_norm_w[l].reshape(1, -1), mla_kv_norm_w[l].reshape(1, -1),
                                    wuq, wk, wv)
        o_mla_ctx = _mla_attn(qh, kh, vh, batch, seq, 0)
        kpe_c = jnp.pad(cache_mla_kpe[:, l], ((0, 0), (0, 0), (MLA_NOPE, LANES - MLA_NOPE - MLA_ROPE)))
        o_mla_lat = _mla_attn(qh, kh, vh, dec_batch, dec_seq, n_ctx,
                              cache=(cache_mla_ckv[:, l], kpe_c, wk, wv))
        st_ckv.append(ckv[:n_ctx].reshape(batch, seq, MLA_KV_LORA))
        st_kpe.append(small[:n_ctx, SM_KPE:SM_KPE + MLA_ROPE].reshape(batch, seq, MLA_ROPE))

        sink_l = jnp.repeat(swa_sinks[l], HEAD_DIM).reshape(1, GROUP_W)
        o_swa_ctx = _swa_ctx(geom, swa, e_mat, sink_l)
        past = cache_swa_k.shape[2]
        o_swa_lat = _swa_lat(geom, swa, *swa_tabs, cache_swa_k[:, l].reshape(dec_batch, past, LANES),
                             cache_swa_v[:, l].reshape(dec_batch, past, LANES), e_mat, sink_l)
        st_k.append(swa[:n_ctx, GROUP_W:GROUP_W + LANES].reshape(batch, seq, 2, HEAD_DIM))
        st_v.append(swa[:n_ctx, GROUP_W + LANES:].reshape(batch, seq, 2, HEAD_DIM))

        o_mla = jnp.concatenate([o_mla_ctx, o_mla_lat], axis=0)
        o_swa = jnp.concatenate([o_swa_ctx, o_swa_lat], axis=0)
        x = _outproj(geom, x, ada_l, (dn_of, dn_ob, dn_z, ssm_yf, ssm_yb, ssm_z, o_mla, o_swa),
                     jnp.tile(dn_norm_w[l], N_HEADS).reshape(1, GROUP_W),
                     ssm_norm_w[l].reshape(1, GROUP_W), w_out[l].astype(BF16))
        wgu = w_gate_up[l].astype(BF16)
        x = _ffn(geom, x, ada_l, norm2_w[l], wgu[:, :FF_DIM], wgu[:, FF_DIM:], w_down[l].astype(BF16),
                 final_norm_w, l == depth - 1)

    return (x[:n_ctx].reshape(batch, seq, d), x[n_ctx:].reshape(dec_batch, dec_seq, d),
            jnp.stack(st_dn, axis=1), jnp.stack(st_ckv, axis=1), jnp.stack(st_kpe, axis=1),
            jnp.stack(st_ssm, axis=1), jnp.stack(st_k, axis=1), jnp.stack(st_v, axis=1))
```

```python
import functools

import jax
import jax.numpy as jnp
from jax import lax
from jax.experimental import pallas as pl
from jax.experimental.pallas import tpu as pltpu

F32 = jnp.float32
BF16 = jnp.bfloat16

D_MODEL = 1024
GRID_W = 64
HEAD_DIM = 64
GROUP_W = 256
EPS = 1e-6
ROPE_THETA = 10000.0
NEG_INF = -1e30
N_HEADS = 4
MLA_NOPE = 64
MLA_ROPE = 32
MLA_Q_LORA = 256
MLA_KV_LORA = 128
MLA_SCALE = (MLA_NOPE + MLA_ROPE) ** -0.5
SSM_N = 64
SWA_SCALE = HEAD_DIM ** -0.5
LOG2_E = 1.4426950408889634
SWA_BLOCK = 128
FF_DIM = 2816
FF_CHUNK = 256

LANES = 128
SUBLANES = 8
SEQ_TILE = 256
ROW_TILE = 512
VMEM_LIMIT = 56 * 1024 * 1024

SEG_WIDTHS = (768, 256, 256, 128, 256, 512, 512, 128)
IN_PAD = sum(SEG_WIDTHS)
SM_BETA, SM_ALPHA, SM_DT, SM_KPE = 0, 8, 16, 64


def _sigmoid(x):
    return 1.0 / (1.0 + jnp.exp(-x))


def _silu(x):
    return x * _sigmoid(x)


def _softplus(x):
    return jnp.maximum(x, 0.0) + jnp.log1p(jnp.exp(-jnp.abs(x)))


def _mm(a, b):
    return jnp.dot(a.astype(BF16), b.astype(BF16), preferred_element_type=F32)


def _mm_nt(a, b):
    return lax.dot_general(a.astype(BF16), b.astype(BF16), (((1,), (1,)), ((), ())),
                           preferred_element_type=F32)


def _mm_tn(a, b):
    return lax.dot_general(a.astype(BF16), b.astype(BF16), (((0,), (0,)), ((), ())),
                           preferred_element_type=F32)


def _split(a, parts):
    out = []
    for _ in range(parts):
        hi = a.astype(BF16)
        out.append(hi)
        a = a - hi.astype(F32)
    return out


def _mm_split_lhs(a, b_exact, parts, nt=False):
    dims = (((1,), (1,)), ((), ())) if nt else (((1,), (0,)), ((), ()))
    acc = None
    for piece in _split(a, parts):
        r = lax.dot_general(piece, b_exact, dims, preferred_element_type=F32)
        acc = r if acc is None else acc + r
    return acc


def _mm_split_rhs(a_exact, b, parts):
    acc = None
    for piece in _split(b, parts):
        r = jnp.dot(a_exact, piece, preferred_element_type=F32)
        acc = r if acc is None else acc + r
    return acc


def _ones_where(mask, dtype):
    return jnp.where(mask, 1.0, 0.0).astype(dtype)


def _rms(x, w):
    return x * lax.rsqrt(jnp.mean(x * x, axis=-1, keepdims=True) + EPS) * w


def _cparams(sem):
    return pltpu.CompilerParams(dimension_semantics=sem, vmem_limit_bytes=VMEM_LIMIT)


def _ada_kernel(c_ref, w_ref, b_ref, o_ref):
    o_ref[0] = _mm(_silu(c_ref[...]), w_ref[0]) + b_ref[0]


def _ada(cc, w_ada, b_ada):
    depth, d, n = w_ada.shape
    tn = 1536
    return pl.pallas_call(
        _ada_kernel,
        out_shape=jax.ShapeDtypeStruct((depth, cc.shape[0], n), F32),
        grid=(depth, n // tn),
        in_specs=[pl.BlockSpec(cc.shape, lambda l, j: (0, 0)),
                  pl.BlockSpec((1, d, tn), lambda l, j: (l, 0, j)),
                  pl.BlockSpec((1, 1, tn), lambda l, j: (l, 0, j))],
        out_specs=pl.BlockSpec((1, cc.shape[0], tn), lambda l, j: (l, 0, j)),
        compiler_params=_cparams(("arbitrary", "arbitrary")),
        name="ada",
    )(cc, w_ada, b_ada.reshape(depth, 1, n))


class _Geom:
    def __init__(self, n_ctx_seq, ctx_len, n_lat_seq, lat_len):
        self.n_ctx_seq, self.ctx_len = n_ctx_seq, ctx_len
        self.n_lat_seq, self.lat_len = n_lat_seq, lat_len
        self.n_ctx = n_ctx_seq * ctx_len
        self.n_lat = n_lat_seq * lat_len
        self.n_tok = self.n_ctx + self.n_lat
        assert ctx_len % SEQ_TILE == 0 and lat_len % ROW_TILE == 0 and self.n_ctx % ROW_TILE == 0
        self.cps = ctx_len // SEQ_TILE
        self.lps = lat_len // SEQ_TILE
        self.n_ctx_tiles = n_ctx_seq * self.cps
        self.n_tiles = self.n_ctx_tiles + n_lat_seq * self.lps
        self.n_seq = n_ctx_seq + n_lat_seq

    def mod_row(self, i):
        r = i * ROW_TILE
        return jnp.where(r >= self.n_ctx, 1 + (r - self.n_ctx) // self.lat_len, 0)

    def tile_pos(self, t):
        is_lat = t >= self.n_ctx_tiles
        u = t - self.n_ctx_tiles
        seq = jnp.where(is_lat, self.n_ctx_seq + u // self.lps, t // self.cps)
        pos = jnp.where(is_lat, u % self.lps, t % self.cps)
        nt = jnp.where(is_lat, self.lps, self.cps)
        return seq, pos, nt

    def mirror(self, t):
        _, pos, nt = self.tile_pos(t)
        return t - pos + (nt - 1 - pos)

    def split_specs(self, width):
        nc = self.n_ctx // ROW_TILE
        return [pl.BlockSpec((ROW_TILE, width), lambda i: (jnp.minimum(i, nc - 1), 0)),
                pl.BlockSpec((ROW_TILE, width), lambda i: (jnp.maximum(i - nc, 0), 0))]

    def is_ctx_tile(self, i):
        return i < self.n_ctx // ROW_TILE


def _read_split(ctx_ref, lat_ref, is_ctx):
    return jnp.where(is_ctx, ctx_ref[...], lat_ref[...])


def _inproj_kernel(*refs, geom, n_x):
    x_refs, (ada_ref, nw_ref, w_ref), out_refs = refs[:n_x], refs[n_x:n_x + 3], refs[n_x + 3:]
    x = x_refs[0][...] if n_x == 1 else _read_split(*x_refs, geom.is_ctx_tile(pl.program_id(0)))
    h = _rms(x, nw_ref[...]) * (1.0 + ada_ref[0, 1:2, :]) + ada_ref[0, 0:1, :]
    h = h.astype(BF16)
    off = 0
    for o_ref in out_refs:
        wd = o_ref.shape[-1]
        o_ref[...] = jnp.dot(h, w_ref[:, off:off + wd], preferred_element_type=F32)
        off += wd


def _inproj(geom, xs, ada_l, norm_w, w_pad):
    d = xs[0].shape[1]
    x_specs = [pl.BlockSpec((ROW_TILE, d), lambda i: (i, 0))] if len(xs) == 1 else geom.split_specs(d)
    return pl.pallas_call(
        functools.partial(_inproj_kernel, geom=geom, n_x=len(xs)),
        out_shape=[jax.ShapeDtypeStruct((geom.n_tok, wd), F32) for wd in SEG_WIDTHS],
        grid=(geom.n_tok // ROW_TILE,),
        in_specs=x_specs + [pl.BlockSpec((1, 6, d), lambda i: (geom.mod_row(i), 0, 0)),
                            pl.BlockSpec((1, d), lambda i: (0, 0)),
                            pl.BlockSpec((d, IN_PAD), lambda i: (0, 0))],
        out_specs=[pl.BlockSpec((ROW_TILE, wd), lambda i: (i, 0)) for wd in SEG_WIDTHS],
        compiler_params=_cparams(("arbitrary",)),
        name="inproj",
    )(*xs, ada_l, norm_w.reshape(1, d), w_pad)


def _conv3(x, x_prev, x_next, w_ref):
    n = x.shape[0]
    r = lax.broadcasted_iota(jnp.int32, x.shape, 0)
    x_dn = jnp.where(r == 0, x_prev, pltpu.roll(x, 1, 0))
    x_up = jnp.where(r == n - 1, x_next, pltpu.roll(x, n - 1, 0))
    return w_ref[0:1, :] * x_dn + w_ref[1:2, :] * x + w_ref[2:3, :] * x_up


def _head_lanes(head_l, colfn):
    out = colfn(N_HEADS - 1)
    for h in range(N_HEADS - 2, -1, -1):
        out = jnp.where(head_l == h, colfn(h), out)
    return out


def _cumulative(z, incl_b, parts=3):
    cum = _mm_split_rhs(incl_b, z, parts)
    cumt = _mm_split_lhs(z.T, incl_b, parts, nt=True)
    return cum, cumt


def _unit_tri_inverses(a_list, lvl, eye):
    dot = functools.partial(jnp.dot, preferred_element_type=F32)
    a0 = [jnp.where(lvl < 3, a, 0.0) for a in a_list]
    a0b = [a.astype(BF16) for a in a0]
    x = [eye - a for a in a0]
    p = [dot(a, a) for a in a0b]
    pb = [v.astype(BF16) for v in p]
    x = [xi + dot(xi.astype(BF16), pi) for xi, pi in zip(x, pb)]
    pb = [dot(pi, pi).astype(BF16) for pi in pb]
    xb = [(xi + dot(xi.astype(BF16), pi)).astype(BF16) for xi, pi in zip(x, pb)]
    for m in range(3, a_list[0].shape[0].bit_length() - 1):
        sel = lvl == m
        am = [jnp.where(sel, a, 0.0).astype(BF16) for a in a_list]
        y = [dot(ai, xi).astype(BF16) for ai, xi in zip(am, xb)]
        xb = [jnp.where(sel, -dot(xi, yi), xi.astype(F32)).astype(BF16) for xi, yi in zip(xb, y)]
    return xb


def _seq_tile_specs(geom, width, mirror):
    rows8 = SEQ_TILE // SUBLANES
    last8 = geom.n_tok // SUBLANES - 1
    tile = (lambda t: geom.mirror(t)) if mirror else (lambda t: t)
    return [pl.BlockSpec((SEQ_TILE, width), lambda t: (tile(t), 0)),
            pl.BlockSpec((SUBLANES, width), lambda t: (jnp.maximum(tile(t) * rows8 - 1, 0), 0)),
            pl.BlockSpec((SUBLANES, width), lambda t: (jnp.minimum((tile(t) + 1) * rows8, last8), 0))]


def _dn_kernel(xf_ref, xfp_ref, xfn_ref, xb_ref, xbp_ref, xbn_ref, gf_ref, gb_ref, s0_ref,
               cw_ref, alog_ref, bias_ref, of_ref, ob_ref, sfin_ref, s_scr, *, geom):
    t = pl.program_id(0)
    seq, pos, nt = geom.tile_pos(t)
    is_ctx = seq < geom.n_ctx_seq
    tt = SEQ_TILE

    @pl.when((pos == 0) & is_ctx)
    def _():
        s_scr[...] = jnp.zeros(s_scr.shape, F32)

    @pl.when((pos == 0) & jnp.logical_not(is_ctx))
    def _():
        s_scr[...] = s0_ref[0]

    row = lax.broadcasted_iota(jnp.int32, (tt, tt), 0)
    col = lax.broadcasted_iota(jnp.int32, (tt, tt), 1)
    xr = row ^ col
    lvl = jnp.where(xr >= 2, 1, 0)
    for kbit in range(2, tt.bit_length() - 1):
        lvl = lvl + jnp.where(xr >= (1 << kbit), 1, 0)
    head_l = lax.broadcasted_iota(jnp.int32, (tt, GROUP_W), 1) // HEAD_DIM
    blockdiag = (lax.broadcasted_iota(jnp.int32, (GROUP_W, GROUP_W), 0) // HEAD_DIM
                 == lax.broadcasted_iota(jnp.int32, (GROUP_W, GROUP_W), 1) // HEAD_DIM)
    eye = _ones_where(row == col, F32)
    gones = _ones_where(blockdiag, BF16)
    lane_g = lax.broadcasted_iota(jnp.int32, (tt, LANES), 1)
    not_first, not_last = pos > 0, pos < nt - 1

    a_list, per_dir = [], []
    dirs = ((xf_ref, xfp_ref, xfn_ref, gf_ref, of_ref, not_first, not_last),
            (xb_ref, xbp_ref, xbn_ref, gb_ref, ob_ref, not_last, not_first))
    for d, (x_ref, xp_ref, xn_ref, g_ref, o_ref, has_prev, has_next) in enumerate(dirs):
        x_prev = jnp.where(has_prev, xp_ref[SUBLANES - 1:SUBLANES, :], 0.0)
        x_next = jnp.where(has_next, xn_ref[0:1, :], 0.0)
        y = _silu(_conv3(x_ref[...], x_prev, x_next, cw_ref))
        q, k, v = y[:, :GROUP_W], y[:, GROUP_W:2 * GROUP_W], y[:, 2 * GROUP_W:]
        q = q * lax.rsqrt(_mm_split_lhs(q * q, gones, 2) + EPS) * (HEAD_DIM ** -0.5)
        k = k * lax.rsqrt(_mm_split_lhs(k * k, gones, 2) + EPS)

        s = g_ref[...]
        gate = -jnp.exp(alog_ref[...]) * _softplus(s + bias_ref[...])
        z = jnp.where(lane_g < SM_ALPHA, _sigmoid(s), gate)
        incl = (row >= col) if d == 0 else (row <= col)
        strict = (row > col) if d == 0 else (row < col)
        cum, cumt = _cumulative(z, _ones_where(incl, BF16))
        edge = tt - 1 if d == 0 else 0
        c_beta = SM_BETA + N_HEADS * d
        c_g = SM_ALPHA + N_HEADS * d

        beta_l = _head_lanes(head_l, lambda h: z[:, c_beta + h:c_beta + h + 1])
        cum_l = _head_lanes(head_l, lambda h: cum[:, c_g + h:c_g + h + 1])
        tot_l = _head_lanes(head_l[0:1, :], lambda h: cum[edge:edge + 1, c_g + h:c_g + h + 1])
        eg = jnp.exp(cum_l)
        kb = k * beta_l
        rhs = jnp.concatenate([v * beta_l, kb * eg], axis=1).astype(BF16)
        kbf = k.astype(BF16)
        qk_heads = []
        for h in range(N_HEADS):
            hm = head_l == h
            diff = cum[:, c_g + h:c_g + h + 1] - cumt[c_g + h:c_g + h + 1, :]
            decay = jnp.where(incl, jnp.exp(jnp.where(incl, diff, 0.0)), 0.0)
            a_list.append(jnp.where(strict, _mm_nt(jnp.where(hm, kb, 0.0), kbf) * decay, 0.0))
            qk_heads.append((_mm_nt(jnp.where(hm, q, 0.0), kbf) * decay).astype(BF16))
        per_dir.append((rhs, qk_heads, (q * eg).astype(BF16), k * jnp.exp(tot_l - cum_l),
                        jnp.exp(tot_l), o_ref))

    x_list = _unit_tri_inverses(a_list, lvl, eye)

    for d, (rhs, qk_heads, qg, kd, e_tot, o_ref) in enumerate(per_dir):
        u_all = jnp.zeros((tt, GROUP_W), F32)
        w_all = jnp.zeros((tt, GROUP_W), F32)
        for h in range(N_HEADS):
            hm = head_l == h
            uw = jnp.dot(x_list[N_HEADS * d + h], rhs, preferred_element_type=F32)
            u_all = jnp.where(hm, uw[:, :GROUP_W], u_all)
            w_all = jnp.where(hm, uw[:, GROUP_W:], w_all)
        state = s_scr[d]
        sb = state.astype(BF16)
        v_new = u_all - _mm(w_all, sb)
        vb = v_new.astype(BF16)
        o = jnp.dot(qg, sb, preferred_element_type=F32)
        for h in range(N_HEADS):
            o = o + jnp.where(head_l == h, jnp.dot(qk_heads[h], vb, preferred_element_type=F32), 0.0)
        o_ref[...] = o
        s_scr[d] = state * e_tot + jnp.where(blockdiag, _mm_tn(kd, vb), 0.0)

    @pl.when((pos == nt - 1) & is_ctx)
    def _():
        for d in range(2):
            for h in range(N_HEADS):
                sl = slice(h * HEAD_DIM, (h + 1) * HEAD_DIM)
                sfin_ref[0, d, h] = s_scr[d, sl, sl]


def _state_specs(geom, block):
    seq_of = lambda t: geom.tile_pos(t)[0]
    lat_seq = lambda t: (jnp.maximum(seq_of(t) - geom.n_ctx_seq, 0),) + (0,) * len(block[0])
    ctx_seq = lambda t: (jnp.minimum(seq_of(t), geom.n_ctx_seq - 1),) + (0,) * len(block[1])
    return pl.BlockSpec((1,) + block[0], lat_seq), pl.BlockSpec((1,) + block[1], ctx_seq)


def _deltanet(geom, qkv, small, s0, conv_w, alog_row, bias_row):
    tt = SEQ_TILE
    row_spec = lambda shape: pl.BlockSpec(shape, lambda t: (0,) * len(shape))
    s0_spec, sfin_spec = _state_specs(geom, ((2, GROUP_W, GROUP_W), (2, N_HEADS, HEAD_DIM, HEAD_DIM)))
    in_specs = (_seq_tile_specs(geom, 3 * GROUP_W, False) + _seq_tile_specs(geom, 3 * GROUP_W, True)
                + [pl.BlockSpec((tt, LANES), lambda t: (t, 0)),
                   pl.BlockSpec((tt, LANES), lambda t: (geom.mirror(t), 0)),
                   s0_spec,
                   row_spec(conv_w.shape), row_spec((1, LANES)), row_spec((1, LANES))])
    return pl.pallas_call(
        functools.partial(_dn_kernel, geom=geom),
        out_shape=[jax.ShapeDtypeStruct((geom.n_tok, GROUP_W), F32),
                   jax.ShapeDtypeStruct((geom.n_tok, GROUP_W), F32),
                   jax.ShapeDtypeStruct((geom.n_ctx_seq, 2, N_HEADS, HEAD_DIM, HEAD_DIM), F32)],
        grid=(geom.n_tiles,),
        in_specs=in_specs,
        out_specs=[pl.BlockSpec((tt, GROUP_W), lambda t: (t, 0)),
                   pl.BlockSpec((tt, GROUP_W), lambda t: (geom.mirror(t), 0)),
                   sfin_spec],
        scratch_shapes=[pltpu.VMEM((2, GROUP_W, GROUP_W), F32)],
        compiler_params=_cparams(("arbitrary",)),
        name="deltanet",
    )(qkv, qkv, qkv, qkv, qkv, qkv, small, small, s0, conv_w, alog_row, bias_row)


def _ssm_kernel(xf_ref, xfp_ref, xfn_ref, xb_ref, xbp_ref, xbn_ref, gf_ref, gb_ref, s0_ref,
                cw_ref, cb_ref, alog_ref, bias_ref, dvec_ref, yf_ref, yb_ref, sfin_ref, s_scr,
                *, geom):
    t = pl.program_id(0)
    seq, pos, nt = geom.tile_pos(t)
    is_ctx = seq < geom.n_ctx_seq
    tt = SEQ_TILE

    @pl.when((pos == 0) & is_ctx)
    def _():
        s_scr[...] = jnp.zeros(s_scr.shape, F32)

    @pl.when((pos == 0) & jnp.logical_not(is_ctx))
    def _():
        s_scr[...] = s0_ref[0]

    row = lax.broadcasted_iota(jnp.int32, (tt, tt), 0)
    col = lax.broadcasted_iota(jnp.int32, (tt, tt), 1)
    head_l = lax.broadcasted_iota(jnp.int32, (tt, GROUP_W), 1) // HEAD_DIM
    lane_g = lax.broadcasted_iota(jnp.int32, (tt, LANES), 1)
    group_l = lane_g // SSM_N
    state_head = lax.broadcasted_iota(jnp.int32, (GROUP_W, 2 * SSM_N), 0) // HEAD_DIM
    state_group = lax.broadcasted_iota(jnp.int32, (GROUP_W, 2 * SSM_N), 1) // SSM_N
    state_mask = state_head // 2 == state_group
    not_first, not_last = pos > 0, pos < nt - 1

    dirs = ((xf_ref, xfp_ref, xfn_ref, gf_ref, yf_ref, not_first, not_last),
            (xb_ref, xbp_ref, xbn_ref, gb_ref, yb_ref, not_last, not_first))
    for d, (x_ref, xp_ref, xn_ref, g_ref, y_ref, has_prev, has_next) in enumerate(dirs):
        x_prev = jnp.where(has_prev, xp_ref[SUBLANES - 1:SUBLANES, :], 0.0)
        x_next = jnp.where(has_next, xn_ref[0:1, :], 0.0)
        y = _silu(_conv3(x_ref[...], x_prev, x_next, cw_ref) + cb_ref[...])
        xs, bm, cm = y[:, :GROUP_W], y[:, GROUP_W:GROUP_W + 2 * SSM_N], y[:, GROUP_W + 2 * SSM_N:]

        dt = _softplus(g_ref[...] + bias_ref[...])
        a = -jnp.exp(alog_ref[...]) * dt
        incl = (row >= col) if d == 0 else (row <= col)
        cum, cumt = _cumulative(a, _ones_where(incl, BF16))
        edge = tt - 1 if d == 0 else 0
        c0 = SM_DT + N_HEADS * d

        dt_l = _head_lanes(head_l, lambda h: dt[:, c0 + h:c0 + h + 1])
        cum_l = _head_lanes(head_l, lambda h: cum[:, c0 + h:c0 + h + 1])
        tot_l = _head_lanes(head_l[0:1, :], lambda h: cum[edge:edge + 1, c0 + h:c0 + h + 1])
        xdt = xs * dt_l
        cb_scores = [_mm_nt(jnp.where(group_l == g, cm, 0.0), bm) for g in range(2)]
        out = jnp.zeros((tt, GROUP_W), F32)
        for h in range(N_HEADS):
            diff = cum[:, c0 + h:c0 + h + 1] - cumt[c0 + h:c0 + h + 1, :]
            lmat = jnp.where(incl, jnp.exp(jnp.where(incl, diff, 0.0)), 0.0)
            out = jnp.where(head_l == h, _mm(cb_scores[h // 2] * lmat, xdt), out)
        state = s_scr[d]
        out = out + _mm_nt(cm, state) * jnp.exp(cum_l)
        if d == 0:
            out = out + dvec_ref[...] * xs
        y_ref[...] = out
        tot_rows = _head_lanes(state_head, lambda h: cum[edge:edge + 1, c0 + h:c0 + h + 1])
        s_scr[d] = (state * jnp.exp(tot_rows)
                    + jnp.where(state_mask, _mm_tn(xdt * jnp.exp(tot_l - cum_l), bm), 0.0))

    @pl.when((pos == nt - 1) & is_ctx)
    def _():
        for d in range(2):
            for h in range(N_HEADS):
                g = h // 2
                sfin_ref[0, d, h] = s_scr[d, h * HEAD_DIM:(h + 1) * HEAD_DIM, g * SSM_N:(g + 1) * SSM_N]


def _ssd(geom, xbc, small, s0, conv_w, conv_b, alog_row, bias_row, dvec):
    tt = SEQ_TILE
    wx = GROUP_W + 4 * SSM_N
    row_spec = lambda shape: pl.BlockSpec(shape, lambda t: (0,) * len(shape))
    s0_spec, sfin_spec = _state_specs(geom, ((2, GROUP_W, 2 * SSM_N), (2, N_HEADS, HEAD_DIM, SSM_N)))
    in_specs = (_seq_tile_specs(geom, wx, False) + _seq_tile_specs(geom, wx, True)
                + [pl.BlockSpec((tt, LANES), lambda t: (t, 0)),
                   pl.BlockSpec((tt, LANES), lambda t: (geom.mirror(t), 0)),
                   s0_spec,
                   row_spec(conv_w.shape), row_spec((1, wx)), row_spec((1, LANES)),
                   row_spec((1, LANES)), row_spec((1, GROUP_W))])
    return pl.pallas_call(
        functools.partial(_ssm_kernel, geom=geom),
        out_shape=[jax.ShapeDtypeStruct((geom.n_tok, GROUP_W), F32),
                   jax.ShapeDtypeStruct((geom.n_tok, GROUP_W), F32),
                   jax.ShapeDtypeStruct((geom.n_ctx_seq, 2, N_HEADS, HEAD_DIM, SSM_N), F32)],
        grid=(geom.n_tiles,),
        in_specs=in_specs,
        out_specs=[pl.BlockSpec((tt, GROUP_W), lambda t: (t, 0)),
                   pl.BlockSpec((tt, GROUP_W), lambda t: (geom.mirror(t), 0)),
                   sfin_spec],
        scratch_shapes=[pltpu.VMEM((2, GROUP_W, 2 * SSM_N), F32)],
        compiler_params=_cparams(("arbitrary",)),
        name="ssd",
    )(xbc, xbc, xbc, xbc, xbc, xbc, small, small, s0, conv_w, conv_b, alog_row, bias_row, dvec)


def _rope_slab(x, cos, sin_a, sin_b, half):
    w = x.shape[-1]
    return x * cos + pltpu.roll(x, w - half, 1) * sin_a + pltpu.roll(x, half, 1) * sin_b


def _mla_proj_kernel(ql_ref, kvl_ref, sm_ref, cos_ref, sa_ref, sb_ref, qnw_ref, kvnw_ref,
                     wuq_ref, wk_ref, wv_ref, qh_ref, kh_ref, vh_ref, ckv_ref):
    cos, sa, sb = cos_ref[...], sa_ref[...], sb_ref[...]
    half = MLA_ROPE // 2
    qp = _mm(_rms(ql_ref[...], qnw_ref[...]), wuq_ref[...]) * (MLA_SCALE * LOG2_E)
    ckv = _rms(kvl_ref[...], kvnw_ref[...])
    ckv_ref[...] = ckv
    lane = lax.broadcasted_iota(jnp.int32, cos.shape, 1)
    is_pe = (lane >= MLA_NOPE) & (lane < MLA_NOPE + MLA_ROPE)
    kpe = jnp.where(is_pe, _rope_slab(sm_ref[...], cos, sa, sb, half), 0.0)
    kp = _mm(ckv, wk_ref[...])
    for h in range(N_HEADS):
        sl = slice(h * LANES, (h + 1) * LANES)
        qh_ref[:, sl] = _rope_slab(qp[:, sl], cos, sa, sb, half).astype(BF16)
        kh_ref[:, sl] = (kp[:, sl] + kpe).astype(BF16)
    vh_ref[...] = _mm(ckv, wv_ref[...]).astype(BF16)


def _mla_proj(geom, q_lat, kv_lat, small, cos, sa, sb, qnw, kvnw, wuq, wk, wv):
    tm = ROW_TILE
    tok = lambda w: pl.BlockSpec((tm, w), lambda i: (i, 0))
    full = lambda a: pl.BlockSpec(a.shape, lambda i: (0,) * a.ndim)

    def tab_block(i):
        r = i * tm
        return jnp.where(r >= geom.n_ctx, 1 + ((r - geom.n_ctx) % geom.lat_len) // tm, 0), 0

    tab = pl.BlockSpec((tm, LANES), tab_block)
    return pl.pallas_call(
        _mla_proj_kernel,
        out_shape=[jax.ShapeDtypeStruct((geom.n_tok, N_HEADS * LANES), BF16),
                   jax.ShapeDtypeStruct((geom.n_tok, N_HEADS * LANES), BF16),
                   jax.ShapeDtypeStruct((geom.n_tok, GROUP_W), BF16),
                   jax.ShapeDtypeStruct((geom.n_tok, MLA_KV_LORA), F32)],
        grid=(geom.n_tok // tm,),
        in_specs=[tok(MLA_Q_LORA), tok(MLA_KV_LORA), tok(LANES), tab, tab, tab,
                  full(qnw), full(kvnw), full(wuq), full(wk), full(wv)],
        out_specs=[tok(N_HEADS * LANES), tok(N_HEADS * LANES), tok(GROUP_W), tok(MLA_KV_LORA)],
        compiler_params=_cparams(("parallel",)),
        name="mla_proj",
    )(q_lat, kv_lat, small, cos, sa, sb, qnw, kvnw, wuq, wk, wv)


def _mla_attn_kernel(*refs, has_cache):
    if has_cache:
        q_ref, k_ref, v_ref, ckv_ref, kpe_ref, wk_ref, wv_ref, o_ref = refs
        ckv_c = ckv_ref[0]
        v_c = _mm(ckv_c, wv_ref[...])
    else:
        q_ref, k_ref, v_ref, o_ref = refs
    v = v_ref[...]
    heads = (slice(0, LANES), slice(LANES, 2 * LANES))
    scores = [_mm_nt(q_ref[:, sl], k_ref[:, sl]) for sl in heads]
    if has_cache:
        scores_c = [_mm_nt(q_ref[:, sl], _mm(ckv_c, wk_ref[:, sl]) + kpe_ref[0]) for sl in heads]
    outs = []
    for j in range(2):
        s = scores[j]
        m = jnp.max(s, axis=-1, keepdims=True)
        if has_cache:
            m = jnp.maximum(m, jnp.max(scores_c[j], axis=-1, keepdims=True))
            p_c = jnp.exp2(scores_c[j] - m)
        p = jnp.exp2(s - m)
        den = jnp.sum(p, axis=-1, keepdims=True)
        acc = _mm(p, v)
        if has_cache:
            den = den + jnp.sum(p_c, axis=-1, keepdims=True)
            acc = acc + _mm(p_c, v_c)
        outs.append(acc / den)
    lane = lax.broadcasted_iota(jnp.int32, outs[0].shape, 1)
    o_ref[...] = jnp.where(lane < HEAD_DIM, outs[0], outs[1])


def _mla_attn(qh, kh, vh, n_seq, seq_len, tok0, cache=None):
    tq = min(ROW_TILE, seq_len)
    nq = seq_len // tq
    q0, k0 = tok0 // tq, tok0 // seq_len
    assert tok0 % seq_len == 0
    in_specs = [pl.BlockSpec((tq, 2 * LANES), lambda b, hp, i: (q0 + b * nq + i, hp)),
                pl.BlockSpec((seq_len, 2 * LANES), lambda b, hp, i: (k0 + b, hp)),
                pl.BlockSpec((seq_len, LANES), lambda b, hp, i: (k0 + b, hp))]
    args = [qh, kh, vh]
    if cache is not None:
        ckv_c, kpe_c, wk, wv = cache
        past = ckv_c.shape[1]
        in_specs += [pl.BlockSpec((1, past, MLA_KV_LORA), lambda b, hp, i: (b, 0, 0)),
                     pl.BlockSpec((1, past, LANES), lambda b, hp, i: (b, 0, 0)),
                     pl.BlockSpec((MLA_KV_LORA, 2 * LANES), lambda b, hp, i: (0, hp)),
                     pl.BlockSpec((MLA_KV_LORA, LANES), lambda b, hp, i: (0, hp))]
        args += [ckv_c, kpe_c, wk, wv]
    return pl.pallas_call(
        functools.partial(_mla_attn_kernel, has_cache=cache is not None),
        out_shape=jax.ShapeDtypeStruct((n_seq * seq_len, GROUP_W), F32),
        grid=(n_seq, 2, nq),
        in_specs=in_specs,
        out_specs=pl.BlockSpec((tq, LANES), lambda b, hp, i: (b * nq + i, hp)),
        compiler_params=_cparams(("parallel", "parallel", "arbitrary")),
        name="mla_attn_lat" if cache is not None else "mla_attn_ctx",
    )(*args)


def _swa_core(q, k_all, v_all, e_ref, sink_ref, valid):
    r = q.shape[0]
    e = e_ref[...]
    kx, vx = _mm(k_all, e), _mm(v_all, e)
    head_l = lax.broadcasted_iota(jnp.int32, q.shape, 1) // HEAD_DIM
    qs = jnp.concatenate([jnp.where(head_l == h, q, 0.0) for h in range(N_HEADS)], axis=0)
    s = _mm_nt(qs, kx)
    if valid is not None:
        s = jnp.where(jnp.concatenate([valid] * N_HEADS, axis=0), s, NEG_INF)
    row_head = lax.broadcasted_iota(jnp.int32, (N_HEADS * r, 1), 0) // r
    sink = _head_lanes(row_head, lambda h: sink_ref[:, h * HEAD_DIM:h * HEAD_DIM + 1])
    m = jnp.maximum(jnp.max(s, axis=-1, keepdims=True), sink)
    p = jnp.exp(s - m)
    den = jnp.sum(p, axis=-1, keepdims=True) + jnp.exp(sink - m)
    o4 = _mm(p, vx) / den
    out = o4[(N_HEADS - 1) * r:]
    for h in range(N_HEADS - 2, -1, -1):
        out = jnp.where(head_l == h, o4[h * r:(h + 1) * r], out)
    return out


def _swa_ctx_kernel(x_ref, e_ref, sink_ref, o_ref):
    x = x_ref[...]
    q = x[:, :GROUP_W] * SWA_SCALE
    k, v = x[:, GROUP_W:GROUP_W + LANES], x[:, GROUP_W + LANES:]
    o_ref[...] = _swa_core(q, k, v, e_ref, sink_ref, None)


def _swa_ctx(geom, swa, e_mat, sink_l):
    t = geom.ctx_len
    return pl.pallas_call(
        _swa_ctx_kernel,
        out_shape=jax.ShapeDtypeStruct((geom.n_ctx, GROUP_W), F32),
        grid=(geom.n_ctx_seq,),
        in_specs=[pl.BlockSpec((t, 2 * GROUP_W), lambda b: (b, 0)),
                  pl.BlockSpec(e_mat.shape, lambda b: (0, 0)),
                  pl.BlockSpec((1, GROUP_W), lambda b: (0, 0))],
        out_specs=pl.BlockSpec((t, GROUP_W), lambda b: (b, 0)),
        compiler_params=_cparams(("parallel",)),
        name="swa_ctx",
    )(swa, e_mat, sink_l)


def _swa_lat_kernel(xc_ref, xp_ref, xn_ref, cc_ref, ac_ref, bc_ref, cp_ref, ap_ref, bp_ref,
                    cn_ref, an_ref, bn_ref, kc_ref, vc_ref, e_ref, sink_ref, o_ref, *, nblk):
    i = pl.program_id(1)
    blk = SWA_BLOCK
    half = HEAD_DIM // 2

    def rope(x, c_ref, a_ref, b_ref):
        reps = x.shape[1] // LANES
        wide = lambda t_ref: jnp.concatenate([t_ref[...]] * reps, axis=1) if reps > 1 else t_ref[...]
        return _rope_slab(x, wide(c_ref), wide(a_ref), wide(b_ref), half)

    ksl, vsl = slice(GROUP_W, GROUP_W + LANES), slice(GROUP_W + LANES, 2 * GROUP_W)
    q = rope(xc_ref[:, :GROUP_W], cc_ref, ac_ref, bc_ref) * SWA_SCALE
    k_all = jnp.concatenate([rope(xp_ref[:, ksl], cp_ref, ap_ref, bp_ref),
                             rope(xc_ref[:, ksl], cc_ref, ac_ref, bc_ref),
                             rope(xn_ref[:, ksl], cn_ref, an_ref, bn_ref),
                             kc_ref[0]], axis=0)
    v_all = jnp.concatenate([xp_ref[:, vsl], xc_ref[:, vsl], xn_ref[:, vsl], vc_ref[0]], axis=0)
    nk = k_all.shape[0]
    r = lax.broadcasted_iota(jnp.int32, (blk, nk), 0)
    c = lax.broadcasted_iota(jnp.int32, (blk, nk), 1)
    in_prev = (c < blk) & (c >= r) & (i > 0)
    in_cur = (c >= blk) & (c < 2 * blk)
    in_next = (c >= 2 * blk) & (c < 3 * blk) & (c - 2 * blk <= r) & (i < nblk - 1)
    valid = in_prev | in_cur | in_next | (c >= 3 * blk)
    o_ref[...] = _swa_core(q, k_all, v_all, e_ref, sink_ref, valid)


def _swa_lat(geom, swa, cos, sa, sb, k_cache, v_cache, e_mat, sink_l):
    blk = SWA_BLOCK
    nblk = geom.lat_len // blk
    b0 = geom.n_ctx // blk
    past = k_cache.shape[1]
    cur = lambda b, i: i
    prv = lambda b, i: jnp.maximum(i - 1, 0)
    nxt = lambda b, i: jnp.minimum(i + 1, nblk - 1)
    xspec = lambda f: pl.BlockSpec((blk, 2 * GROUP_W), lambda b, i: (b0 + b * nblk + f(b, i), 0))
    tspec = lambda f: pl.BlockSpec((blk, LANES), lambda b, i: (f(b, i), 0))
    return pl.pallas_call(
        functools.partial(_swa_lat_kernel, nblk=nblk),
        out_shape=jax.ShapeDtypeStruct((geom.n_lat, GROUP_W), F32),
        grid=(geom.n_lat_seq, nblk),
        in_specs=[xspec(cur), xspec(prv), xspec(nxt),
                  tspec(cur), tspec(cur), tspec(cur), tspec(prv), tspec(prv), tspec(prv),
                  tspec(nxt), tspec(nxt), tspec(nxt),
                  pl.BlockSpec((1, past, LANES), lambda b, i: (b, 0, 0)),
                  pl.BlockSpec((1, past, LANES), lambda b, i: (b, 0, 0)),
                  pl.BlockSpec(e_mat.shape, lambda b, i: (0, 0)),
                  pl.BlockSpec((1, GROUP_W), lambda b, i: (0, 0))],
        out_specs=pl.BlockSpec((blk, GROUP_W), lambda b, i: (b * nblk + i, 0)),
        compiler_params=_cparams(("parallel", "arbitrary")),
        name="swa_lat",
    )(swa, swa, swa, cos, sa, sb, cos, sa, sb, cos, sa, sb, k_cache, v_cache, e_mat, sink_l)


def _mix_ffn_kernel(*refs, geom, n_x, final):
    x_refs, refs = refs[:n_x], refs[n_x:]
    (ada_ref, dof_ref, dob_ref, dz_ref, syf_ref, syb_ref, sz_ref, omc_ref, oml_ref, osc_ref, osl_ref,
     dnw_ref, snw_ref, wo_ref, nw_ref, wgu_ref, wd_ref, fw_ref), out_refs = refs[:18], refs[18:]
    is_ctx = geom.is_ctx_tile(pl.program_id(0))
    x = x_refs[0][...] if n_x == 1 else _read_split(*x_refs, is_ctx)

    row = lax.broadcasted_iota(jnp.int32, (GROUP_W, GROUP_W), 0)
    col = lax.broadcasted_iota(jnp.int32, (GROUP_W, GROUP_W), 1)
    gones = _ones_where((row // HEAD_DIM) == (col // HEAD_DIM), BF16)
    o = dof_ref[...] + dob_ref[...]
    ms = _mm_split_lhs(o * o, gones, 2) * (1.0 / HEAD_DIM)
    dn = o * lax.rsqrt(ms + EPS) * dnw_ref[...] * _silu(dz_ref[...])
    acc = _mm(dn, wo_ref[0:GROUP_W, :])
    acc = acc + _mm(_read_split(omc_ref, oml_ref, is_ctx), wo_ref[GROUP_W:2 * GROUP_W, :])
    y = (syf_ref[...] + syb_ref[...]) * _silu(sz_ref[...])
    for g in range(2):
        sl = slice(g * LANES, (g + 1) * LANES)
        acc = acc + _mm(_rms(y[:, sl], snw_ref[:, sl]),
                        wo_ref[2 * GROUP_W + g * LANES:2 * GROUP_W + (g + 1) * LANES, :])
    acc = acc + _mm(_read_split(osc_ref, osl_ref, is_ctx), wo_ref[3 * GROUP_W:, :])
    x = x + ada_ref[0, 2:3, :] * acc

    h = (_rms(x, nw_ref[...]) * (1.0 + ada_ref[0, 4:5, :]) + ada_ref[0, 3:4, :]).astype(BF16)
    acc = jnp.zeros(x.shape, F32)
    for c in range(FF_DIM // FF_CHUNK):
        g = jnp.dot(h, wgu_ref[:, c * FF_CHUNK:(c + 1) * FF_CHUNK], preferred_element_type=F32)
        u = jnp.dot(h, wgu_ref[:, FF_DIM + c * FF_CHUNK:FF_DIM + (c + 1) * FF_CHUNK],
                    preferred_element_type=F32)
        acc = acc + _mm(_silu(g) * u, wd_ref[c * FF_CHUNK:(c + 1) * FF_CHUNK, :])
    y = x + ada_ref[0, 5:6, :] * acc
    if not final:
        out_refs[0][...] = y
        return
    y = _rms(y, fw_ref[...])

    @pl.when(is_ctx)
    def _():
        out_refs[0][...] = y

    @pl.when(jnp.logical_not(is_ctx))
    def _():
        out_refs[1][...] = y


def _mix_ffn(geom, xs, ada_l, parts, dnw, snw, w_out, norm_w, wgu, w_down, final_w, final):
    tm = ROW_TILE
    d = xs[0].shape[1]
    tok = lambda w: pl.BlockSpec((tm, w), lambda i: (i, 0))
    resident = lambda a: pl.BlockSpec(a.shape, lambda i: (0,) * a.ndim, pipeline_mode=pl.Buffered(1))
    x_specs = [tok(d)] if len(xs) == 1 else geom.split_specs(d)
    if final:
        out_shape = [jax.ShapeDtypeStruct((geom.n_ctx, d), F32), jax.ShapeDtypeStruct((geom.n_lat, d), F32)]
        out_specs = geom.split_specs(d)
    else:
        out_shape = [jax.ShapeDtypeStruct((geom.n_tok, d), F32)]
        out_specs = [tok(d)]
    return pl.pallas_call(
        functools.partial(_mix_ffn_kernel, geom=geom, n_x=len(xs), final=final),
        out_shape=out_shape,
        grid=(geom.n_tok // tm,),
        in_specs=x_specs + [pl.BlockSpec((1, 6, d), lambda i: (geom.mod_row(i), 0, 0))]
        + [tok(GROUP_W)] * 6 + geom.split_specs(GROUP_W) + geom.split_specs(GROUP_W)
        + [resident(dnw), resident(snw), resident(w_out), resident(norm_w), resident(wgu),
           resident(w_down), resident(final_w)],
        out_specs=out_specs,
        compiler_params=_cparams(("arbitrary",)),
        name="mix_ffn",
    )(*xs, ada_l, *parts, dnw, snw, w_out, norm_w, wgu, w_down, final_w)


def _pad_cols(a, width):
    return jnp.pad(a, ((0, 0), (0, width - a.shape[1])))


def _w_in_layout(w):
    dn, mla, ssm, swa = 0, 1040, 1456, 2232
    small = jnp.concatenate([
        w[:, dn + 1024:dn + 1040],
        w[:, ssm + 768:ssm + 776],
        jnp.zeros((w.shape[0], SM_KPE - 24), w.dtype),
        w[:, mla + 384:mla + 416],
        jnp.zeros((w.shape[0], LANES - SM_KPE - MLA_ROPE), w.dtype)], axis=1)
    return jnp.concatenate([
        w[:, dn:dn + 768], w[:, dn + 768:dn + 1024],
        w[:, mla:mla + 256], w[:, mla + 256:mla + 384],
        w[:, ssm:ssm + 256], w[:, ssm + 256:ssm + 768],
        w[:, swa:swa + 512], small], axis=1).astype(BF16)


def _gate_row(dn_vec, ssm_vec):
    row = jnp.zeros((1, LANES), F32)
    row = row.at[0, SM_ALPHA:SM_ALPHA + 8].set(dn_vec.reshape(8))
    return row.at[0, SM_DT:SM_DT + 8].set(ssm_vec.reshape(8))


def _axial_angles(rows, rot_dim):
    row_ids = jnp.broadcast_to(jnp.arange(rows)[:, None], (rows, GRID_W)).reshape(-1).astype(F32)
    col_ids = jnp.broadcast_to(jnp.arange(GRID_W)[None, :], (rows, GRID_W)).reshape(-1).astype(F32)
    n_freq = rot_dim // 4
    inv_freq = ROPE_THETA ** (-jnp.arange(n_freq, dtype=F32) / n_freq)
    return jnp.concatenate([row_ids[:, None] * inv_freq, col_ids[:, None] * inv_freq], axis=-1)


def _rope_tables(ang, lane0, reps, n_ident):
    n, half = ang.shape
    cos, sin = jnp.cos(ang), jnp.sin(ang)
    zeros = jnp.zeros_like(sin)
    period = LANES // reps

    def table(first, second, fill):
        one = jnp.concatenate([jnp.full((n, lane0), fill, F32), first, second,
                               jnp.full((n, period - lane0 - 2 * half), fill, F32)], axis=1)
        tab = jnp.concatenate([one] * reps, axis=1)
        ident = jnp.full((n_ident, LANES), fill, F32)
        return jnp.concatenate([ident, tab], axis=0)

    return table(cos, cos, 1.0), table(-sin, zeros, 0.0), table(zeros, sin, 0.0)


def kernel(x_prompt, x_sample, c, state_dn, cache_mla_ckv, cache_mla_kpe, state_ssm, cache_swa_k,
           cache_swa_v, c_ctx, norm1_w, norm2_w, w_ada, b_ada, w_in, w_out, dn_conv_w, dn_a_log,
           dn_dt_bias, dn_norm_w, mla_q_norm_w, mla_w_uq, mla_kv_norm_w, mla_w_ukv, ssm_conv_w,
           ssm_conv_b, ssm_a_log, ssm_dt_bias, ssm_d, ssm_norm_w, swa_sinks, w_gate_up, w_down,
           final_norm_w):
    batch, seq, d = x_prompt.shape
    dec_batch, dec_seq, _ = x_sample.shape
    depth = w_in.shape[0]
    geom = _Geom(batch, seq, dec_batch, dec_seq)
    n_ctx = geom.n_ctx

    xs = (x_prompt.reshape(n_ctx, d), x_sample.reshape(geom.n_lat, d))
    n_mod = -(-(1 + dec_batch) // SUBLANES) * SUBLANES
    cc = jnp.concatenate([c_ctx[None], c, jnp.zeros((n_mod - 1 - dec_batch, d), F32)], axis=0)
    ada = _ada(cc, w_ada, b_ada).reshape(depth, n_mod, 6, d)

    mla_tabs = _rope_tables(_axial_angles(dec_seq // GRID_W, MLA_ROPE), MLA_NOPE, 1, ROW_TILE)
    swa_tabs = _rope_tables(_axial_angles(dec_seq // GRID_W, HEAD_DIM), 0, 2, 0)

    lane = jnp.arange(GROUP_W)
    e_mat = (jnp.arange(LANES)[:, None] == (lane // LANES) * HEAD_DIM + lane % HEAD_DIM).astype(BF16)
    eye_h = jnp.eye(N_HEADS, dtype=F32)
    grp_h = (jnp.arange(N_HEADS)[:, None] // 2 == jnp.arange(2)[None, :]).astype(F32)

    st_dn, st_ckv, st_kpe, st_ssm, st_k, st_v = [], [], [], [], [], []
    for l in range(depth):
        ada_l = ada[l]
        segs = _inproj(geom, xs, ada_l, norm1_w[l], _w_in_layout(w_in[l]))
        dn_qkv, dn_z, mla_q, mla_kv, ssm_z, ssm_xbc, swa, small = segs

        s0 = state_dn[:, l][:, :, :, :, None, :] * eye_h[None, None, :, None, :, None]
        s0 = s0.reshape(dec_batch, 2, GROUP_W, GROUP_W)
        alog_row = _gate_row(dn_a_log[l], ssm_a_log[l])
        bias_row = _gate_row(dn_dt_bias[l], ssm_dt_bias[l])
        dn_of, dn_ob, dn_fin = _deltanet(geom, dn_qkv, small, s0, dn_conv_w[l], alog_row, bias_row)
        st_dn.append(dn_fin)

        s0 = state_ssm[:, l][:, :, :, :, None, :] * grp_h[None, None, :, None, :, None]
        s0 = s0.reshape(dec_batch, 2, GROUP_W, 2 * SSM_N)
        ssm_yf, ssm_yb, ssm_fin = _ssd(
            geom, ssm_xbc, small, s0, ssm_conv_w[l], ssm_conv_b[l].reshape(1, -1), alog_row, bias_row,
            jnp.repeat(ssm_d[l], HEAD_DIM).reshape(1, GROUP_W))
        st_ssm.append(ssm_fin)

        uq = mla_w_uq[l].reshape(MLA_Q_LORA, N_HEADS, MLA_NOPE + MLA_ROPE)
        wuq = jnp.pad(uq, ((0, 0), (0, 0), (0, LANES - MLA_NOPE - MLA_ROPE)))
        wuq = wuq.reshape(MLA_Q_LORA, N_HEADS * LANES).astype(BF16)
        ukv = mla_w_ukv[l].reshape(MLA_KV_LORA, N_HEADS, MLA_NOPE + HEAD_DIM)
        wk = jnp.pad(ukv[:, :, :MLA_NOPE], ((0, 0), (0, 0), (0, LANES - MLA_NOPE)))
        wk = wk.reshape(MLA_KV_LORA, N_HEADS * LANES).astype(BF16)
        wv = ukv[:, :, MLA_NOPE:].reshape(MLA_KV_LORA, GROUP_W).astype(BF16)
        qh, kh, vh, ckv = _mla_proj(geom, mla_q, mla_kv, small, *mla_tabs,
                                    mla_q_norm_w[l].reshape(1, -1), mla_kv_norm_w[l].reshape(1, -1),
                                    wuq, wk, wv)
        o_mla_ctx = _mla_attn(qh, kh, vh, batch, seq, 0)
        kpe_c = jnp.pad(cache_mla_kpe[:, l], ((0, 0), (0, 0), (MLA_NOPE, LANES - MLA_NOPE - MLA_ROPE)))
        o_mla_lat = _mla_attn(qh, kh, vh, dec_batch, dec_seq, n_ctx,
                              cache=(cache_mla_ckv[:, l], kpe_c, wk, wv))
        st_ckv.append(ckv[:n_ctx].reshape(batch, seq, MLA_KV_LORA))
        st_kpe.append(small[:n_ctx, SM_KPE:SM_KPE + MLA_ROPE].reshape(batch, seq, MLA_ROPE))

        sink_l = jnp.repeat(swa_sinks[l], HEAD_DIM).reshape(1, GROUP_W)
        o_swa_ctx = _swa_ctx(geom, swa, e_mat, sink_l)
        past = cache_swa_k.shape[2]
        o_swa_lat = _swa_lat(geom, swa, *swa_tabs, cache_swa_k[:, l].reshape(dec_batch, past, LANES),
                             cache_swa_v[:, l].reshape(dec_batch, past, LANES), e_mat, sink_l)
        st_k.append(swa[:n_ctx, GROUP_W:GROUP_W + LANES].reshape(batch, seq, 2, HEAD_DIM))
        st_v.append(swa[:n_ctx, GROUP_W + LANES:].reshape(batch, seq, 2, HEAD_DIM))

        parts = (dn_of, dn_ob, dn_z, ssm_yf, ssm_yb, ssm_z, o_mla_ctx, o_mla_lat, o_swa_ctx, o_swa_lat)
        xs = _mix_ffn(geom, xs, ada_l, parts, jnp.tile(dn_norm_w[l], N_HEADS).reshape(1, GROUP_W),
                      ssm_norm_w[l].reshape(1, GROUP_W), w_out[l].astype(BF16), norm2_w[l].reshape(1, d),
                      w_gate_up[l].astype(BF16), w_down[l].astype(BF16), final_norm_w.reshape(1, d),
                      l == depth - 1)

    return (xs[0].reshape(batch, seq, d), xs[1].reshape(dec_batch, dec_seq, d),
            jnp.stack(st_dn, axis=1), jnp.stack(st_ckv, axis=1), jnp.stack(st_kpe, axis=1),
            jnp.stack(st_ssm, axis=1), jnp.stack(st_k, axis=1), jnp.stack(st_v, axis=1))
```

```python
import functools

import jax
import jax.numpy as jnp
from jax import lax
from jax.experimental import pallas as pl
from jax.experimental.pallas import tpu as pltpu

F32 = jnp.float32
BF16 = jnp.bfloat16

D_MODEL = 1024
GRID_W = 64
HEAD_DIM = 64
GROUP_W = 256
EPS = 1e-6
ROPE_THETA = 10000.0
NEG_INF = -1e30
N_HEADS = 4
MLA_NOPE = 64
MLA_ROPE = 32
MLA_Q_LORA = 256
MLA_KV_LORA = 128
MLA_SCALE = (MLA_NOPE + MLA_ROPE) ** -0.5
SSM_N = 64
SWA_SCALE = HEAD_DIM ** -0.5
LOG2_E = 1.4426950408889634
SWA_BLOCK = 128
FF_DIM = 2816
FF_CHUNK = 256
KEY_CHUNKS = 4

LANES = 128
SUBLANES = 8
SEQ_TILE = 256
ROW_TILE = 512
VMEM_LIMIT = 56 * 1024 * 1024

SEG_WIDTHS = (768, 256, 256, 128, 256, 512, 512, 128)
IN_PAD = sum(SEG_WIDTHS)
SM_BETA, SM_ALPHA, SM_DT, SM_KPE = 0, 8, 16, 64


def _sigmoid(x):
    return 1.0 / (1.0 + jnp.exp(-x))


def _silu(x):
    return x * _sigmoid(x)


def _softplus(x):
    return jnp.maximum(x, 0.0) + jnp.log1p(jnp.exp(-jnp.abs(x)))


def _mm(a, b):
    return jnp.dot(a.astype(BF16), b.astype(BF16), preferred_element_type=F32)


def _mm_nt(a, b):
    return lax.dot_general(a.astype(BF16), b.astype(BF16), (((1,), (1,)), ((), ())),
                           preferred_element_type=F32)


def _mm_tn(a, b):
    return lax.dot_general(a.astype(BF16), b.astype(BF16), (((0,), (0,)), ((), ())),
                           preferred_element_type=F32)


def _split(a, parts):
    out = []
    for _ in range(parts):
        hi = a.astype(BF16)
        out.append(hi)
        a = a - hi.astype(F32)
    return out


def _mm_split_lhs(a, b_exact, parts, nt=False):
    dims = (((1,), (1,)), ((), ())) if nt else (((1,), (0,)), ((), ()))
    acc = None
    for piece in _split(a, parts):
        r = lax.dot_general(piece, b_exact, dims, preferred_element_type=F32)
        acc = r if acc is None else acc + r
    return acc


def _mm_split_rhs(a_exact, b, parts):
    acc = None
    for piece in _split(b, parts):
        r = jnp.dot(a_exact, piece, preferred_element_type=F32)
        acc = r if acc is None else acc + r
    return acc


def _ones_where(mask, dtype):
    return jnp.where(mask, 1.0, 0.0).astype(dtype)


def _rms(x, w):
    return x * lax.rsqrt(jnp.mean(x * x, axis=-1, keepdims=True) + EPS) * w


def _cparams(sem):
    return pltpu.CompilerParams(dimension_semantics=sem, vmem_limit_bytes=VMEM_LIMIT)


def _ada_kernel(c_ref, w_ref, b_ref, o_ref):
    o_ref[0] = _mm(_silu(c_ref[...]), w_ref[0]) + b_ref[0]


def _ada(cc, w_ada, b_ada):
    depth, d, n = w_ada.shape
    tn = 1536
    return pl.pallas_call(
        _ada_kernel,
        out_shape=jax.ShapeDtypeStruct((depth, cc.shape[0], n), F32),
        grid=(depth, n // tn),
        in_specs=[pl.BlockSpec(cc.shape, lambda l, j: (0, 0)),
                  pl.BlockSpec((1, d, tn), lambda l, j: (l, 0, j)),
                  pl.BlockSpec((1, 1, tn), lambda l, j: (l, 0, j))],
        out_specs=pl.BlockSpec((1, cc.shape[0], tn), lambda l, j: (l, 0, j)),
        compiler_params=_cparams(("arbitrary", "arbitrary")),
        name="ada",
    )(cc, w_ada, b_ada.reshape(depth, 1, n))


class _Geom:
    def __init__(self, n_ctx_seq, ctx_len, n_lat_seq, lat_len):
        self.n_ctx_seq, self.ctx_len = n_ctx_seq, ctx_len
        self.n_lat_seq, self.lat_len = n_lat_seq, lat_len
        self.n_ctx = n_ctx_seq * ctx_len
        self.n_lat = n_lat_seq * lat_len
        self.n_tok = self.n_ctx + self.n_lat
        assert ctx_len % SEQ_TILE == 0 and lat_len % ROW_TILE == 0 and self.n_ctx % ROW_TILE == 0
        self.cps = ctx_len // SEQ_TILE
        self.lps = lat_len // SEQ_TILE
        self.n_ctx_tiles = n_ctx_seq * self.cps
        self.n_tiles = self.n_ctx_tiles + n_lat_seq * self.lps
        self.n_seq = n_ctx_seq + n_lat_seq

    def mod_row(self, i):
        r = i * ROW_TILE
        return jnp.where(r >= self.n_ctx, 1 + (r - self.n_ctx) // self.lat_len, 0)

    def tile_pos(self, t):
        is_lat = t >= self.n_ctx_tiles
        u = t - self.n_ctx_tiles
        seq = jnp.where(is_lat, self.n_ctx_seq + u // self.lps, t // self.cps)
        pos = jnp.where(is_lat, u % self.lps, t % self.cps)
        nt = jnp.where(is_lat, self.lps, self.cps)
        return seq, pos, nt

    def mirror(self, t):
        _, pos, nt = self.tile_pos(t)
        return t - pos + (nt - 1 - pos)

    def split_specs(self, width):
        nc = self.n_ctx // ROW_TILE
        return [pl.BlockSpec((ROW_TILE, width), lambda i: (jnp.minimum(i, nc - 1), 0)),
                pl.BlockSpec((ROW_TILE, width), lambda i: (jnp.maximum(i - nc, 0), 0))]

    def is_ctx_tile(self, i):
        return i < self.n_ctx // ROW_TILE


def _layer_spec(a, l, **kw):
    return pl.BlockSpec((None,) + a.shape[1:], lambda *_: (l,) + (0,) * (a.ndim - 1), **kw)


def _ada_spec(geom, ada, l):
    return pl.BlockSpec((None, 1) + ada.shape[2:], lambda i: (l, geom.mod_row(i), 0, 0))


def _read_split(ctx_ref, lat_ref, is_ctx):
    return jnp.where(is_ctx, ctx_ref[...], lat_ref[...])


def _inproj_kernel(*refs, geom, n_x):
    x_refs, (ada_ref, nw_ref, w_ref), out_refs = refs[:n_x], refs[n_x:n_x + 3], refs[n_x + 3:]
    x = x_refs[0][...] if n_x == 1 else _read_split(*x_refs, geom.is_ctx_tile(pl.program_id(0)))
    h = _rms(x, nw_ref[...]) * (1.0 + ada_ref[0, 1:2, :]) + ada_ref[0, 0:1, :]
    h = h.astype(BF16)
    off = 0
    for o_ref in out_refs:
        wd = o_ref.shape[-1]
        o_ref[...] = jnp.dot(h, w_ref[:, off:off + wd], preferred_element_type=F32)
        off += wd


def _inproj(geom, xs, ada, norm_w, w_pad, l):
    d = xs[0].shape[1]
    x_specs = [pl.BlockSpec((ROW_TILE, d), lambda i: (i, 0))] if len(xs) == 1 else geom.split_specs(d)
    return pl.pallas_call(
        functools.partial(_inproj_kernel, geom=geom, n_x=len(xs)),
        out_shape=[jax.ShapeDtypeStruct((geom.n_tok, wd), F32) for wd in SEG_WIDTHS],
        grid=(geom.n_tok // ROW_TILE,),
        in_specs=x_specs + [_ada_spec(geom, ada, l), _layer_spec(norm_w, l), _layer_spec(w_pad, l)],
        out_specs=[pl.BlockSpec((ROW_TILE, wd), lambda i: (i, 0)) for wd in SEG_WIDTHS],
        compiler_params=_cparams(("arbitrary",)),
        name="inproj",
    )(*xs, ada, norm_w, w_pad)


def _conv3(x, x_prev, x_next, w_ref):
    n = x.shape[0]
    r = lax.broadcasted_iota(jnp.int32, x.shape, 0)
    x_dn = jnp.where(r == 0, x_prev, pltpu.roll(x, 1, 0))
    x_up = jnp.where(r == n - 1, x_next, pltpu.roll(x, n - 1, 0))
    return w_ref[0:1, :] * x_dn + w_ref[1:2, :] * x + w_ref[2:3, :] * x_up


def _head_lanes(head_l, colfn):
    out = colfn(N_HEADS - 1)
    for h in range(N_HEADS - 2, -1, -1):
        out = jnp.where(head_l == h, colfn(h), out)
    return out


def _cumulative(z, incl_b, parts=3):
    cum = _mm_split_rhs(incl_b, z, parts)
    cumt = _mm_split_lhs(z.T, incl_b, parts, nt=True)
    return cum, cumt


def _unit_tri_inverses(a_list, lvl, eye):
    dot = functools.partial(jnp.dot, preferred_element_type=F32)
    a0 = [jnp.where(lvl < 3, a, 0.0) for a in a_list]
    a0b = [a.astype(BF16) for a in a0]
    x = [eye - a for a in a0]
    p = [dot(a, a) for a in a0b]
    pb = [v.astype(BF16) for v in p]
    x = [xi + dot(xi.astype(BF16), pi) for xi, pi in zip(x, pb)]
    pb = [dot(pi, pi).astype(BF16) for pi in pb]
    xb = [(xi + dot(xi.astype(BF16), pi)).astype(BF16) for xi, pi in zip(x, pb)]
    for m in range(3, a_list[0].shape[0].bit_length() - 1):
        sel = lvl == m
        am = [jnp.where(sel, a, 0.0).astype(BF16) for a in a_list]
        y = [dot(ai, xi).astype(BF16) for ai, xi in zip(am, xb)]
        xb = [jnp.where(sel, -dot(xi, yi), xi.astype(F32)).astype(BF16) for xi, yi in zip(xb, y)]
    return xb


def _seq_tile_specs(geom, width, mirror):
    rows8 = SEQ_TILE // SUBLANES
    last8 = geom.n_tok // SUBLANES - 1
    tile = (lambda t: geom.mirror(t)) if mirror else (lambda t: t)
    return [pl.BlockSpec((SEQ_TILE, width), lambda t: (tile(t), 0)),
            pl.BlockSpec((SUBLANES, width), lambda t: (jnp.maximum(tile(t) * rows8 - 1, 0), 0)),
            pl.BlockSpec((SUBLANES, width), lambda t: (jnp.minimum((tile(t) + 1) * rows8, last8), 0))]


def _dn_kernel(xf_ref, xfp_ref, xfn_ref, xb_ref, xbp_ref, xbn_ref, gf_ref, gb_ref, s0_ref,
               cw_ref, alog_ref, bias_ref, of_ref, ob_ref, sfin_ref, s_scr, *, geom):
    t = pl.program_id(0)
    seq, pos, nt = geom.tile_pos(t)
    is_ctx = seq < geom.n_ctx_seq
    tt = SEQ_TILE

    @pl.when((pos == 0) & is_ctx)
    def _():
        s_scr[...] = jnp.zeros(s_scr.shape, F32)

    @pl.when((pos == 0) & jnp.logical_not(is_ctx))
    def _():
        s_scr[...] = s0_ref[0]

    row = lax.broadcasted_iota(jnp.int32, (tt, tt), 0)
    col = lax.broadcasted_iota(jnp.int32, (tt, tt), 1)
    xr = row ^ col
    lvl = jnp.where(xr >= 2, 1, 0)
    for kbit in range(2, tt.bit_length() - 1):
        lvl = lvl + jnp.where(xr >= (1 << kbit), 1, 0)
    head_l = lax.broadcasted_iota(jnp.int32, (tt, GROUP_W), 1) // HEAD_DIM
    blockdiag = (lax.broadcasted_iota(jnp.int32, (GROUP_W, GROUP_W), 0) // HEAD_DIM
                 == lax.broadcasted_iota(jnp.int32, (GROUP_W, GROUP_W), 1) // HEAD_DIM)
    eye = _ones_where(row == col, F32)
    gones = _ones_where(blockdiag, BF16)
    lane_g = lax.broadcasted_iota(jnp.int32, (tt, LANES), 1)
    not_first, not_last = pos > 0, pos < nt - 1

    a_list, per_dir = [], []
    dirs = ((xf_ref, xfp_ref, xfn_ref, gf_ref, of_ref, not_first, not_last),
            (xb_ref, xbp_ref, xbn_ref, gb_ref, ob_ref, not_last, not_first))
    for d, (x_ref, xp_ref, xn_ref, g_ref, o_ref, has_prev, has_next) in enumerate(dirs):
        x_prev = jnp.where(has_prev, xp_ref[SUBLANES - 1:SUBLANES, :], 0.0)
        x_next = jnp.where(has_next, xn_ref[0:1, :], 0.0)
        y = _silu(_conv3(x_ref[...], x_prev, x_next, cw_ref))
        q, k, v = y[:, :GROUP_W], y[:, GROUP_W:2 * GROUP_W], y[:, 2 * GROUP_W:]
        q = q * lax.rsqrt(_mm_split_lhs(q * q, gones, 2) + EPS) * (HEAD_DIM ** -0.5)
        k = k * lax.rsqrt(_mm_split_lhs(k * k, gones, 2) + EPS)

        s = g_ref[...]
        gate = -jnp.exp(alog_ref[...]) * _softplus(s + bias_ref[...])
        z = jnp.where(lane_g < SM_ALPHA, _sigmoid(s), gate)
        incl = (row >= col) if d == 0 else (row <= col)
        strict = (row > col) if d == 0 else (row < col)
        cum, cumt = _cumulative(z, _ones_where(incl, BF16))
        edge = tt - 1 if d == 0 else 0
        c_beta = SM_BETA + N_HEADS * d
        c_g = SM_ALPHA + N_HEADS * d

        beta_l = _head_lanes(head_l, lambda h: z[:, c_beta + h:c_beta + h + 1])
        cum_l = _head_lanes(head_l, lambda h: cum[:, c_g + h:c_g + h + 1])
        tot_l = _head_lanes(head_l[0:1, :], lambda h: cum[edge:edge + 1, c_g + h:c_g + h + 1])
        eg = jnp.exp(cum_l)
        kb = k * beta_l
        rhs = jnp.concatenate([v * beta_l, kb * eg], axis=1).astype(BF16)
        kbf = k.astype(BF16)
        qk_heads = []
        for h in range(N_HEADS):
            hm = head_l == h
            diff = cum[:, c_g + h:c_g + h + 1] - cumt[c_g + h:c_g + h + 1, :]
            decay = jnp.where(incl, jnp.exp(jnp.where(incl, diff, 0.0)), 0.0)
            a_list.append(jnp.where(strict, _mm_nt(jnp.where(hm, kb, 0.0), kbf) * decay, 0.0))
            qk_heads.append((_mm_nt(jnp.where(hm, q, 0.0), kbf) * decay).astype(BF16))
        per_dir.append((rhs, qk_heads, (q * eg).astype(BF16), k * jnp.exp(tot_l - cum_l),
                        jnp.exp(tot_l), o_ref))

    x_list = _unit_tri_inverses(a_list, lvl, eye)

    for d, (rhs, qk_heads, qg, kd, e_tot, o_ref) in enumerate(per_dir):
        u_all = jnp.zeros((tt, GROUP_W), F32)
        w_all = jnp.zeros((tt, GROUP_W), F32)
        for h in range(N_HEADS):
            hm = head_l == h
            uw = jnp.dot(x_list[N_HEADS * d + h], rhs, preferred_element_type=F32)
            u_all = jnp.where(hm, uw[:, :GROUP_W], u_all)
            w_all = jnp.where(hm, uw[:, GROUP_W:], w_all)
        state = s_scr[d]
        sb = state.astype(BF16)
        v_new = u_all - _mm(w_all, sb)
        vb = v_new.astype(BF16)
        o = jnp.dot(qg, sb, preferred_element_type=F32)
        for h in range(N_HEADS):
            o = o + jnp.where(head_l == h, jnp.dot(qk_heads[h], vb, preferred_element_type=F32), 0.0)
        o_ref[...] = o
        s_scr[d] = state * e_tot + jnp.where(blockdiag, _mm_tn(kd, vb), 0.0)

    @pl.when((pos == nt - 1) & is_ctx)
    def _():
        for d in range(2):
            for h in range(N_HEADS):
                sl = slice(h * HEAD_DIM, (h + 1) * HEAD_DIM)
                sfin_ref[0, d, h] = s_scr[d, sl, sl]


def _state_specs(geom, block, l):
    seq_of = lambda t: geom.tile_pos(t)[0]
    lat_seq = lambda t: (jnp.maximum(seq_of(t) - geom.n_ctx_seq, 0), l) + (0,) * len(block[0])
    ctx_seq = lambda t: (jnp.minimum(seq_of(t), geom.n_ctx_seq - 1),) + (0,) * len(block[1])
    return pl.BlockSpec((1, None) + block[0], lat_seq), pl.BlockSpec((1,) + block[1], ctx_seq)


def _deltanet(geom, qkv, small, s0, conv_w, alog_row, bias_row, l):
    tt = SEQ_TILE
    s0_spec, sfin_spec = _state_specs(geom, ((2, GROUP_W, GROUP_W), (2, N_HEADS, HEAD_DIM, HEAD_DIM)), l)
    in_specs = (_seq_tile_specs(geom, 3 * GROUP_W, False) + _seq_tile_specs(geom, 3 * GROUP_W, True)
                + [pl.BlockSpec((tt, LANES), lambda t: (t, 0)),
                   pl.BlockSpec((tt, LANES), lambda t: (geom.mirror(t), 0)),
                   s0_spec, _layer_spec(conv_w, l), _layer_spec(alog_row, l), _layer_spec(bias_row, l)])
    return pl.pallas_call(
        functools.partial(_dn_kernel, geom=geom),
        out_shape=[jax.ShapeDtypeStruct((geom.n_tok, GROUP_W), F32),
                   jax.ShapeDtypeStruct((geom.n_tok, GROUP_W), F32),
                   jax.ShapeDtypeStruct((geom.n_ctx_seq, 2, N_HEADS, HEAD_DIM, HEAD_DIM), F32)],
        grid=(geom.n_tiles,),
        in_specs=in_specs,
        out_specs=[pl.BlockSpec((tt, GROUP_W), lambda t: (t, 0)),
                   pl.BlockSpec((tt, GROUP_W), lambda t: (geom.mirror(t), 0)),
                   sfin_spec],
        scratch_shapes=[pltpu.VMEM((2, GROUP_W, GROUP_W), F32)],
        compiler_params=_cparams(("arbitrary",)),
        name="deltanet",
    )(qkv, qkv, qkv, qkv, qkv, qkv, small, small, s0, conv_w, alog_row, bias_row)


def _ssm_kernel(xf_ref, xfp_ref, xfn_ref, xb_ref, xbp_ref, xbn_ref, gf_ref, gb_ref, s0_ref,
                cw_ref, cb_ref, alog_ref, bias_ref, dvec_ref, yf_ref, yb_ref, sfin_ref, s_scr,
                *, geom):
    t = pl.program_id(0)
    seq, pos, nt = geom.tile_pos(t)
    is_ctx = seq < geom.n_ctx_seq
    tt = SEQ_TILE

    @pl.when((pos == 0) & is_ctx)
    def _():
        s_scr[...] = jnp.zeros(s_scr.shape, F32)

    @pl.when((pos == 0) & jnp.logical_not(is_ctx))
    def _():
        s_scr[...] = s0_ref[0]

    row = lax.broadcasted_iota(jnp.int32, (tt, tt), 0)
    col = lax.broadcasted_iota(jnp.int32, (tt, tt), 1)
    head_l = lax.broadcasted_iota(jnp.int32, (tt, GROUP_W), 1) // HEAD_DIM
    lane_g = lax.broadcasted_iota(jnp.int32, (tt, LANES), 1)
    group_l = lane_g // SSM_N
    state_head = lax.broadcasted_iota(jnp.int32, (GROUP_W, 2 * SSM_N), 0) // HEAD_DIM
    state_group = lax.broadcasted_iota(jnp.int32, (GROUP_W, 2 * SSM_N), 1) // SSM_N
    state_mask = state_head // 2 == state_group
    not_first, not_last = pos > 0, pos < nt - 1

    dirs = ((xf_ref, xfp_ref, xfn_ref, gf_ref, yf_ref, not_first, not_last),
            (xb_ref, xbp_ref, xbn_ref, gb_ref, yb_ref, not_last, not_first))
    for d, (x_ref, xp_ref, xn_ref, g_ref, y_ref, has_prev, has_next) in enumerate(dirs):
        x_prev = jnp.where(has_prev, xp_ref[SUBLANES - 1:SUBLANES, :], 0.0)
        x_next = jnp.where(has_next, xn_ref[0:1, :], 0.0)
        y = _silu(_conv3(x_ref[...], x_prev, x_next, cw_ref) + cb_ref[...])
        xs, bm, cm = y[:, :GROUP_W], y[:, GROUP_W:GROUP_W + 2 * SSM_N], y[:, GROUP_W + 2 * SSM_N:]

        dt = _softplus(g_ref[...] + bias_ref[...])
        a = -jnp.exp(alog_ref[...]) * dt
        incl = (row >= col) if d == 0 else (row <= col)
        cum, cumt = _cumulative(a, _ones_where(incl, BF16))
        edge = tt - 1 if d == 0 else 0
        c0 = SM_DT + N_HEADS * d

        dt_l = _head_lanes(head_l, lambda h: dt[:, c0 + h:c0 + h + 1])
        cum_l = _head_lanes(head_l, lambda h: cum[:, c0 + h:c0 + h + 1])
        tot_l = _head_lanes(head_l[0:1, :], lambda h: cum[edge:edge + 1, c0 + h:c0 + h + 1])
        xdt = xs * dt_l
        cb_scores = [_mm_nt(jnp.where(group_l == g, cm, 0.0), bm) for g in range(2)]
        out = jnp.zeros((tt, GROUP_W), F32)
        for h in range(N_HEADS):
            diff = cum[:, c0 + h:c0 + h + 1] - cumt[c0 + h:c0 + h + 1, :]
            lmat = jnp.where(incl, jnp.exp(jnp.where(incl, diff, 0.0)), 0.0)
            out = jnp.where(head_l == h, _mm(cb_scores[h // 2] * lmat, xdt), out)
        state = s_scr[d]
        out = out + _mm_nt(cm, state) * jnp.exp(cum_l)
        if d == 0:
            out = out + dvec_ref[...] * xs
        y_ref[...] = out
        tot_rows = _head_lanes(state_head, lambda h: cum[edge:edge + 1, c0 + h:c0 + h + 1])
        s_scr[d] = (state * jnp.exp(tot_rows)
                    + jnp.where(state_mask, _mm_tn(xdt * jnp.exp(tot_l - cum_l), bm), 0.0))

    @pl.when((pos == nt - 1) & is_ctx)
    def _():
        for d in range(2):
            for h in range(N_HEADS):
                g = h // 2
                sfin_ref[0, d, h] = s_scr[d, h * HEAD_DIM:(h + 1) * HEAD_DIM, g * SSM_N:(g + 1) * SSM_N]


def _ssd(geom, xbc, small, s0, conv_w, conv_b, alog_row, bias_row, dvec, l):
    tt = SEQ_TILE
    wx = GROUP_W + 4 * SSM_N
    s0_spec, sfin_spec = _state_specs(geom, ((2, GROUP_W, 2 * SSM_N), (2, N_HEADS, HEAD_DIM, SSM_N)), l)
    in_specs = (_seq_tile_specs(geom, wx, False) + _seq_tile_specs(geom, wx, True)
                + [pl.BlockSpec((tt, LANES), lambda t: (t, 0)),
                   pl.BlockSpec((tt, LANES), lambda t: (geom.mirror(t), 0)),
                   s0_spec, _layer_spec(conv_w, l), _layer_spec(conv_b, l), _layer_spec(alog_row, l),
                   _layer_spec(bias_row, l), _layer_spec(dvec, l)])
    return pl.pallas_call(
        functools.partial(_ssm_kernel, geom=geom),
        out_shape=[jax.ShapeDtypeStruct((geom.n_tok, GROUP_W), F32),
                   jax.ShapeDtypeStruct((geom.n_tok, GROUP_W), F32),
                   jax.ShapeDtypeStruct((geom.n_ctx_seq, 2, N_HEADS, HEAD_DIM, SSM_N), F32)],
        grid=(geom.n_tiles,),
        in_specs=in_specs,
        out_specs=[pl.BlockSpec((tt, GROUP_W), lambda t: (t, 0)),
                   pl.BlockSpec((tt, GROUP_W), lambda t: (geom.mirror(t), 0)),
                   sfin_spec],
        scratch_shapes=[pltpu.VMEM((2, GROUP_W, 2 * SSM_N), F32)],
        compiler_params=_cparams(("arbitrary",)),
        name="ssd",
    )(xbc, xbc, xbc, xbc, xbc, xbc, small, small, s0, conv_w, conv_b, alog_row, bias_row, dvec)


def _rope_slab(x, cos, sin_a, sin_b, half):
    w = x.shape[-1]
    return x * cos + pltpu.roll(x, w - half, 1) * sin_a + pltpu.roll(x, half, 1) * sin_b


def _mla_proj_kernel(ql_ref, kvl_ref, sm_ref, cos_ref, sa_ref, sb_ref, qnw_ref, kvnw_ref,
                     wuq_ref, wk_ref, wv_ref, qh_ref, kh_ref, vh_ref, ckv_ref, *, geom):
    cos, sa, sb = cos_ref[...], sa_ref[...], sb_ref[...]
    half = MLA_ROPE // 2
    qp = _mm(_rms(ql_ref[...], qnw_ref[...]), wuq_ref[...]) * (MLA_SCALE * LOG2_E)
    ckv = _rms(kvl_ref[...], kvnw_ref[...])

    @pl.when(geom.is_ctx_tile(pl.program_id(0)))
    def _():
        ckv_ref[...] = ckv

    lane = lax.broadcasted_iota(jnp.int32, cos.shape, 1)
    is_pe = (lane >= MLA_NOPE) & (lane < MLA_NOPE + MLA_ROPE)
    kpe = jnp.where(is_pe, _rope_slab(sm_ref[...], cos, sa, sb, half), 0.0)
    kp = _mm(ckv, wk_ref[...])
    for h in range(N_HEADS):
        sl = slice(h * LANES, (h + 1) * LANES)
        qh_ref[:, sl] = _rope_slab(qp[:, sl], cos, sa, sb, half).astype(BF16)
        kh_ref[:, sl] = (kp[:, sl] + kpe).astype(BF16)
    vh_ref[...] = _mm(ckv, wv_ref[...]).astype(BF16)


def _mla_proj(geom, q_lat, kv_lat, small, cos, sa, sb, qnw, kvnw, wuq, wk, wv, l):
    tm = ROW_TILE
    tok = lambda w: pl.BlockSpec((tm, w), lambda i: (i, 0))
    full = lambda a: _layer_spec(a, l)

    def tab_block(i):
        r = i * tm
        return jnp.where(r >= geom.n_ctx, 1 + ((r - geom.n_ctx) % geom.lat_len) // tm, 0), 0

    tab = pl.BlockSpec((tm, LANES), tab_block)
    return pl.pallas_call(
        functools.partial(_mla_proj_kernel, geom=geom),
        out_shape=[jax.ShapeDtypeStruct((geom.n_tok, N_HEADS * LANES), BF16),
                   jax.ShapeDtypeStruct((geom.n_tok, N_HEADS * LANES), BF16),
                   jax.ShapeDtypeStruct((geom.n_tok, GROUP_W), BF16),
                   jax.ShapeDtypeStruct((geom.n_ctx, MLA_KV_LORA), F32)],
        grid=(geom.n_tok // tm,),
        in_specs=[tok(MLA_Q_LORA), tok(MLA_KV_LORA), tok(LANES), tab, tab, tab,
                  full(qnw), full(kvnw), full(wuq), full(wk), full(wv)],
        out_specs=[tok(N_HEADS * LANES), tok(N_HEADS * LANES), tok(GROUP_W),
                   geom.split_specs(MLA_KV_LORA)[0]],
        compiler_params=_cparams(("arbitrary",)),
        name="mla_proj",
    )(q_lat, kv_lat, small, cos, sa, sb, qnw, kvnw, wuq, wk, wv)


def _mla_attn_kernel(*refs, has_cache):
    if has_cache:
        q_ref, k_ref, v_ref, ckv_ref, kpe_ref, wk_ref, wv_ref, o_ref = refs
        ckv_c = ckv_ref[0]
        v_c = _mm(ckv_c, wv_ref[...])
    else:
        q_ref, k_ref, v_ref, o_ref = refs
    heads = (slice(0, LANES), slice(LANES, 2 * LANES))
    n_keys = k_ref.shape[0]
    chunk = n_keys // KEY_CHUNKS if n_keys % (KEY_CHUNKS * LANES) == 0 else n_keys
    qs = [q_ref[:, sl] for sl in heads]
    if has_cache:
        scores = [_mm_nt(q, _mm(ckv_c, wk_ref[:, sl]) + kpe_ref[0]) for q, sl in zip(qs, heads)]
        ms = [jnp.max(s, axis=-1, keepdims=True) for s in scores]
        ps = [jnp.exp2(s - m) for s, m in zip(scores, ms)]
        dens = [jnp.sum(p, axis=-1, keepdims=True) for p in ps]
        accs = [_mm(p, v_c) for p in ps]
    for c in range(n_keys // chunk):
        rows = slice(c * chunk, (c + 1) * chunk)
        scores = [_mm_nt(q, k_ref[rows, sl]) for q, sl in zip(qs, heads)]
        v = v_ref[rows, :]
        if c == 0 and not has_cache:
            ms = [jnp.max(s, axis=-1, keepdims=True) for s in scores]
            ps = [jnp.exp2(s - m) for s, m in zip(scores, ms)]
            dens = [jnp.sum(p, axis=-1, keepdims=True) for p in ps]
            accs = [_mm(p, v) for p in ps]
            continue
        for j, s in enumerate(scores):
            m_new = jnp.maximum(ms[j], jnp.max(s, axis=-1, keepdims=True))
            alpha = jnp.exp2(ms[j] - m_new)
            p = jnp.exp2(s - m_new)
            dens[j] = dens[j] * alpha + jnp.sum(p, axis=-1, keepdims=True)
            accs[j] = accs[j] * alpha + _mm(p, v)
            ms[j] = m_new
    outs = [acc / den for acc, den in zip(accs, dens)]
    lane = lax.broadcasted_iota(jnp.int32, outs[0].shape, 1)
    o_ref[...] = jnp.where(lane < HEAD_DIM, outs[0], outs[1])


def _mla_attn(qh, kh, vh, n_seq, seq_len, tok0, cache=None, l=0):
    tq = min(ROW_TILE, seq_len)
    nq = seq_len // tq
    q0, k0 = tok0 // tq, tok0 // seq_len
    assert tok0 % seq_len == 0
    in_specs = [pl.BlockSpec((tq, 2 * LANES), lambda b, hp, i: (q0 + b * nq + i, hp)),
                pl.BlockSpec((seq_len, 2 * LANES), lambda b, hp, i: (k0 + b, hp)),
                pl.BlockSpec((seq_len, LANES), lambda b, hp, i: (k0 + b, hp))]
    args = [qh, kh, vh]
    if cache is not None:
        ckv_c, kpe_c, wk, wv = cache
        past = ckv_c.shape[2]
        in_specs += [pl.BlockSpec((1, None, past, MLA_KV_LORA), lambda b, hp, i: (b, l, 0, 0)),
                     pl.BlockSpec((1, None, past, LANES), lambda b, hp, i: (b, l, 0, 0)),
                     pl.BlockSpec((None, MLA_KV_LORA, 2 * LANES), lambda b, hp, i: (l, 0, hp)),
                     pl.BlockSpec((None, MLA_KV_LORA, LANES), lambda b, hp, i: (l, 0, hp))]
        args += [ckv_c, kpe_c, wk, wv]
    return pl.pallas_call(
        functools.partial(_mla_attn_kernel, has_cache=cache is not None),
        out_shape=jax.ShapeDtypeStruct((n_seq * seq_len, GROUP_W), F32),
        grid=(n_seq, 2, nq),
        in_specs=in_specs,
        out_specs=pl.BlockSpec((tq, LANES), lambda b, hp, i: (b * nq + i, hp)),
        compiler_params=_cparams(("parallel", "parallel", "arbitrary")),
        name="mla_attn_lat" if cache is not None else "mla_attn_ctx",
    )(*args)


def _swa_core(q, k_all, v_all, e_ref, sink_ref, valid):
    e = e_ref[...]
    kx, vx = _mm(k_all, e).astype(BF16), _mm(v_all, e).astype(BF16)
    head_l = lax.broadcasted_iota(jnp.int32, q.shape, 1) // HEAD_DIM
    heads = range(N_HEADS)
    s = [_mm_nt(jnp.where(head_l == h, q, 0.0), kx) for h in heads]
    if valid is not None:
        s = [jnp.where(valid, x, NEG_INF) for x in s]
    sink = [sink_ref[:, h * HEAD_DIM:h * HEAD_DIM + 1] * LOG2_E for h in heads]
    m = [jnp.maximum(jnp.max(s[h], axis=-1, keepdims=True), sink[h]) for h in heads]
    p = [jnp.exp2(s[h] - m[h]) for h in heads]
    den = [jnp.sum(p[h], axis=-1, keepdims=True) + jnp.exp2(sink[h] - m[h]) for h in heads]
    o = [_mm(p[h], vx) / den[h] for h in heads]
    out = o[N_HEADS - 1]
    for h in range(N_HEADS - 2, -1, -1):
        out = jnp.where(head_l == h, o[h], out)
    return out


def _swa_ctx_kernel(x_ref, e_ref, sink_ref, o_ref):
    x = x_ref[...]
    q = x[:, :GROUP_W] * (SWA_SCALE * LOG2_E)
    k, v = x[:, GROUP_W:GROUP_W + LANES], x[:, GROUP_W + LANES:]
    o_ref[...] = _swa_core(q, k, v, e_ref, sink_ref, None)


def _swa_ctx(geom, swa, e_mat, sink_l, l):
    t = geom.ctx_len
    return pl.pallas_call(
        _swa_ctx_kernel,
        out_shape=jax.ShapeDtypeStruct((geom.n_ctx, GROUP_W), F32),
        grid=(geom.n_ctx_seq,),
        in_specs=[pl.BlockSpec((t, 2 * GROUP_W), lambda b: (b, 0)),
                  pl.BlockSpec(e_mat.shape, lambda b: (0, 0)),
                  _layer_spec(sink_l, l)],
        out_specs=pl.BlockSpec((t, GROUP_W), lambda b: (b, 0)),
        compiler_params=_cparams(("parallel",)),
        name="swa_ctx",
    )(swa, e_mat, sink_l)


def _swa_lat_kernel(xc_ref, xp_ref, xn_ref, cc_ref, ac_ref, bc_ref, cp_ref, ap_ref, bp_ref,
                    cn_ref, an_ref, bn_ref, kc_ref, vc_ref, e_ref, sink_ref, o_ref, *, n_tiles):
    i = pl.program_id(1)
    win = SWA_BLOCK
    tq = xc_ref.shape[0]
    half = HEAD_DIM // 2

    def rope(x, c_ref, a_ref, b_ref):
        reps = x.shape[1] // LANES
        wide = lambda t_ref: jnp.concatenate([t_ref[...]] * reps, axis=1) if reps > 1 else t_ref[...]
        return _rope_slab(x, wide(c_ref), wide(a_ref), wide(b_ref), half)

    ksl, vsl = slice(GROUP_W, GROUP_W + LANES), slice(GROUP_W + LANES, 2 * GROUP_W)
    q = rope(xc_ref[:, :GROUP_W], cc_ref, ac_ref, bc_ref) * (SWA_SCALE * LOG2_E)
    k_all = jnp.concatenate([rope(xp_ref[:, ksl], cp_ref, ap_ref, bp_ref),
                             rope(xc_ref[:, ksl], cc_ref, ac_ref, bc_ref),
                             rope(xn_ref[:, ksl], cn_ref, an_ref, bn_ref),
                             kc_ref[0]], axis=0)
    v_all = jnp.concatenate([xp_ref[:, vsl], xc_ref[:, vsl], xn_ref[:, vsl], vc_ref[0]], axis=0)
    nk = k_all.shape[0]
    n_local = tq + 2 * win
    r = lax.broadcasted_iota(jnp.int32, (tq, nk), 0)
    c = lax.broadcasted_iota(jnp.int32, (tq, nk), 1)
    in_seq = ((c >= win) | (i > 0)) & ((c < win + tq) | (i < n_tiles - 1))
    valid = ((c >= r) & (c <= r + 2 * win) & in_seq) | (c >= n_local)
    o_ref[...] = _swa_core(q, k_all, v_all, e_ref, sink_ref, valid)


def _swa_lat(geom, swa, cos, sa, sb, k_cache, v_cache, e_mat, sink_l, l):
    win = SWA_BLOCK
    tq = 2 * win
    n_tiles = geom.lat_len // tq
    nblk = geom.lat_len // win
    t0, b0 = geom.n_ctx // tq, geom.n_ctx // win
    past = k_cache.shape[2]
    prv = lambda i: jnp.maximum(2 * i - 1, 0)
    nxt = lambda i: jnp.minimum(2 * i + 2, nblk - 1)
    tok = lambda w: [pl.BlockSpec((tq, w), lambda b, i: (t0 + b * n_tiles + i, 0)),
                     pl.BlockSpec((win, w), lambda b, i: (b0 + b * nblk + prv(i), 0)),
                     pl.BlockSpec((win, w), lambda b, i: (b0 + b * nblk + nxt(i), 0))]
    tab = lambda f, rows: [pl.BlockSpec((rows, LANES), lambda b, i: (f(i), 0))] * 3
    return pl.pallas_call(
        functools.partial(_swa_lat_kernel, n_tiles=n_tiles),
        out_shape=jax.ShapeDtypeStruct((geom.n_lat, GROUP_W), F32),
        grid=(geom.n_lat_seq, n_tiles),
        in_specs=tok(2 * GROUP_W) + tab(lambda i: i, tq) + tab(prv, win) + tab(nxt, win)
        + [pl.BlockSpec((1, None, past, LANES), lambda b, i: (b, l, 0, 0)),
           pl.BlockSpec((1, None, past, LANES), lambda b, i: (b, l, 0, 0)),
           pl.BlockSpec(e_mat.shape, lambda b, i: (0, 0)),
           _layer_spec(sink_l, l)],
        out_specs=pl.BlockSpec((tq, GROUP_W), lambda b, i: (b * n_tiles + i, 0)),
        compiler_params=_cparams(("parallel", "arbitrary")),
        name="swa_lat",
    )(swa, swa, swa, cos, sa, sb, cos, sa, sb, cos, sa, sb, k_cache, v_cache, e_mat, sink_l)


def _mix_ffn_kernel(*refs, geom, n_x, final):
    x_refs, refs = refs[:n_x], refs[n_x:]
    (ada_ref, dof_ref, dob_ref, dz_ref, syf_ref, syb_ref, sz_ref, omc_ref, oml_ref, osc_ref, osl_ref,
     dnw_ref, snw_ref, wo_ref, nw_ref, wgu_ref, wd_ref, fw_ref), out_refs = refs[:18], refs[18:]
    is_ctx = geom.is_ctx_tile(pl.program_id(0))
    x = x_refs[0][...] if n_x == 1 else _read_split(*x_refs, is_ctx)

    row = lax.broadcasted_iota(jnp.int32, (GROUP_W, GROUP_W), 0)
    col = lax.broadcasted_iota(jnp.int32, (GROUP_W, GROUP_W), 1)
    gones = _ones_where((row // HEAD_DIM) == (col // HEAD_DIM), BF16)
    o = dof_ref[...] + dob_ref[...]
    ms = _mm_split_lhs(o * o, gones, 2) * (1.0 / HEAD_DIM)
    dn = o * lax.rsqrt(ms + EPS) * dnw_ref[...] * _silu(dz_ref[...])
    acc = _mm(dn, wo_ref[0:GROUP_W, :])
    acc = acc + _mm(_read_split(omc_ref, oml_ref, is_ctx), wo_ref[GROUP_W:2 * GROUP_W, :])
    y = (syf_ref[...] + syb_ref[...]) * _silu(sz_ref[...])
    for g in range(2):
        sl = slice(g * LANES, (g + 1) * LANES)
        acc = acc + _mm(_rms(y[:, sl], snw_ref[:, sl]),
                        wo_ref[2 * GROUP_W + g * LANES:2 * GROUP_W + (g + 1) * LANES, :])
    acc = acc + _mm(_read_split(osc_ref, osl_ref, is_ctx), wo_ref[3 * GROUP_W:, :])
    x = x + ada_ref[0, 2:3, :] * acc

    h = (_rms(x, nw_ref[...]) * (1.0 + ada_ref[0, 4:5, :]) + ada_ref[0, 3:4, :]).astype(BF16)
    acc = jnp.zeros(x.shape, F32)
    for c in range(FF_DIM // FF_CHUNK):
        g = jnp.dot(h, wgu_ref[:, c * FF_CHUNK:(c + 1) * FF_CHUNK], preferred_element_type=F32)
        u = jnp.dot(h, wgu_ref[:, FF_DIM + c * FF_CHUNK:FF_DIM + (c + 1) * FF_CHUNK],
                    preferred_element_type=F32)
        acc = acc + _mm(_silu(g) * u, wd_ref[c * FF_CHUNK:(c + 1) * FF_CHUNK, :])
    y = x + ada_ref[0, 5:6, :] * acc
    if not final:
        out_refs[0][...] = y
        return
    y = _rms(y, fw_ref[...])

    @pl.when(is_ctx)
    def _():
        out_refs[0][...] = y

    @pl.when(jnp.logical_not(is_ctx))
    def _():
        out_refs[1][...] = y


def _mix_ffn(geom, xs, ada, parts, dnw, snw, w_out, norm_w, wgu, w_down, final_w, l, final):
    tm = ROW_TILE
    d = xs[0].shape[1]
    tok = lambda w: pl.BlockSpec((tm, w), lambda i: (i, 0))
    resident = lambda a: _layer_spec(a, l, pipeline_mode=pl.Buffered(1))
    x_specs = [tok(d)] if len(xs) == 1 else geom.split_specs(d)
    if final:
        out_shape = [jax.ShapeDtypeStruct((geom.n_ctx, d), F32), jax.ShapeDtypeStruct((geom.n_lat, d), F32)]
        out_specs = geom.split_specs(d)
    else:
        out_shape = [jax.ShapeDtypeStruct((geom.n_tok, d), F32)]
        out_specs = [tok(d)]
    return pl.pallas_call(
        functools.partial(_mix_ffn_kernel, geom=geom, n_x=len(xs), final=final),
        out_shape=out_shape,
        grid=(geom.n_tok // tm,),
        in_specs=x_specs + [_ada_spec(geom, ada, l)]
        + [tok(GROUP_W)] * 6 + geom.split_specs(GROUP_W) + geom.split_specs(GROUP_W)
        + [resident(dnw), resident(snw), resident(w_out), resident(norm_w), resident(wgu),
           resident(w_down), pl.BlockSpec(final_w.shape, lambda i: (0, 0))],
        out_specs=out_specs,
        compiler_params=_cparams(("arbitrary",)),
        name="mix_ffn",
    )(*xs, ada, *parts, dnw, snw, w_out, norm_w, wgu, w_down, final_w)


def _w_in_layout(w):
    dn, mla, ssm, swa = 0, 1040, 1456, 2232
    zeros = lambda n: jnp.zeros(w.shape[:-1] + (n,), w.dtype)
    small = jnp.concatenate([
        w[..., dn + 1024:dn + 1040],
        w[..., ssm + 768:ssm + 776],
        zeros(SM_KPE - 24),
        w[..., mla + 384:mla + 416],
        zeros(LANES - SM_KPE - MLA_ROPE)], axis=-1)
    return jnp.concatenate([
        w[..., dn:dn + 768], w[..., dn + 768:dn + 1024],
        w[..., mla:mla + 256], w[..., mla + 256:mla + 384],
        w[..., ssm:ssm + 256], w[..., ssm + 256:ssm + 768],
        w[..., swa:swa + 512], small], axis=-1).astype(BF16)


def _gate_rows(dn_vec, ssm_vec):
    depth = dn_vec.shape[0]
    rows = jnp.zeros((depth, 1, LANES), F32)
    rows = rows.at[:, 0, SM_ALPHA:SM_ALPHA + 8].set(dn_vec.reshape(depth, 8))
    return rows.at[:, 0, SM_DT:SM_DT + 8].set(ssm_vec.reshape(depth, 8))


def _axial_angles(rows, rot_dim):
    row_ids = jnp.broadcast_to(jnp.arange(rows)[:, None], (rows, GRID_W)).reshape(-1).astype(F32)
    col_ids = jnp.broadcast_to(jnp.arange(GRID_W)[None, :], (rows, GRID_W)).reshape(-1).astype(F32)
    n_freq = rot_dim // 4
    inv_freq = ROPE_THETA ** (-jnp.arange(n_freq, dtype=F32) / n_freq)
    return jnp.concatenate([row_ids[:, None] * inv_freq, col_ids[:, None] * inv_freq], axis=-1)


def _rope_tables(ang, lane0, reps, n_ident):
    n, half = ang.shape
    cos, sin = jnp.cos(ang), jnp.sin(ang)
    zeros = jnp.zeros_like(sin)
    period = LANES // reps

    def table(first, second, fill):
        one = jnp.concatenate([jnp.full((n, lane0), fill, F32), first, second,
                               jnp.full((n, period - lane0 - 2 * half), fill, F32)], axis=1)
        tab = jnp.concatenate([one] * reps, axis=1)
        ident = jnp.full((n_ident, LANES), fill, F32)
        return jnp.concatenate([ident, tab], axis=0)

    return table(cos, cos, 1.0), table(-sin, zeros, 0.0), table(zeros, sin, 0.0)


def kernel(x_prompt, x_sample, c, state_dn, cache_mla_ckv, cache_mla_kpe, state_ssm, cache_swa_k,
           cache_swa_v, c_ctx, norm1_w, norm2_w, w_ada, b_ada, w_in, w_out, dn_conv_w, dn_a_log,
           dn_dt_bias, dn_norm_w, mla_q_norm_w, mla_w_uq, mla_kv_norm_w, mla_w_ukv, ssm_conv_w,
           ssm_conv_b, ssm_a_log, ssm_dt_bias, ssm_d, ssm_norm_w, swa_sinks, w_gate_up, w_down,
           final_norm_w):
    batch, seq, d = x_prompt.shape
    dec_batch, dec_seq, _ = x_sample.shape
    depth = w_in.shape[0]
    geom = _Geom(batch, seq, dec_batch, dec_seq)
    n_ctx = geom.n_ctx

    xs = (x_prompt.reshape(n_ctx, d), x_sample.reshape(geom.n_lat, d))
    n_mod = -(-(1 + dec_batch) // SUBLANES) * SUBLANES
    cc = jnp.concatenate([c_ctx[None], c, jnp.zeros((n_mod - 1 - dec_batch, d), F32)], axis=0)
    ada = _ada(cc, w_ada, b_ada).reshape(depth, n_mod, 6, d)

    mla_tabs = _rope_tables(_axial_angles(dec_seq // GRID_W, MLA_ROPE), MLA_NOPE, 1, ROW_TILE)
    swa_tabs = _rope_tables(_axial_angles(dec_seq // GRID_W, HEAD_DIM), 0, 2, 0)

    lane = jnp.arange(GROUP_W)
    e_mat = (jnp.arange(LANES)[:, None] == (lane // LANES) * HEAD_DIM + lane % HEAD_DIM).astype(BF16)
    eye_h = jnp.eye(N_HEADS, dtype=F32)
    grp_h = (jnp.arange(N_HEADS)[:, None] // 2 == jnp.arange(2)[None, :]).astype(F32)
    row = lambda a: a.reshape(depth, 1, -1)
    w_pad = _w_in_layout(w_in)
    alog_rows = _gate_rows(dn_a_log, ssm_a_log)
    bias_rows = _gate_rows(dn_dt_bias, ssm_dt_bias)
    s0_dn = state_dn[:, :, :, :, :, None, :] * eye_h[None, None, None, :, None, :, None]
    s0_dn = s0_dn.reshape(dec_batch, depth, 2, GROUP_W, GROUP_W)
    s0_ssm = state_ssm[:, :, :, :, :, None, :] * grp_h[None, None, None, :, None, :, None]
    s0_ssm = s0_ssm.reshape(dec_batch, depth, 2, GROUP_W, 2 * SSM_N)
    ssm_dvec = row(jnp.repeat(ssm_d, HEAD_DIM, axis=-1))
    uq = mla_w_uq.reshape(depth, MLA_Q_LORA, N_HEADS, MLA_NOPE + MLA_ROPE)
    wuq = jnp.pad(uq, ((0, 0), (0, 0), (0, 0), (0, LANES - MLA_NOPE - MLA_ROPE)))
    wuq = wuq.reshape(depth, MLA_Q_LORA, N_HEADS * LANES).astype(BF16)
    ukv = mla_w_ukv.reshape(depth, MLA_KV_LORA, N_HEADS, MLA_NOPE + HEAD_DIM)
    wk = jnp.pad(ukv[..., :MLA_NOPE], ((0, 0), (0, 0), (0, 0), (0, LANES - MLA_NOPE)))
    wk = wk.reshape(depth, MLA_KV_LORA, N_HEADS * LANES).astype(BF16)
    wv = ukv[..., MLA_NOPE:].reshape(depth, MLA_KV_LORA, GROUP_W).astype(BF16)
    kpe_c = jnp.pad(cache_mla_kpe, ((0, 0), (0, 0), (0, 0), (MLA_NOPE, LANES - MLA_NOPE - MLA_ROPE)))
    past = cache_swa_k.shape[2]
    swa_kc = cache_swa_k.reshape(dec_batch, depth, past, LANES)
    swa_vc = cache_swa_v.reshape(dec_batch, depth, past, LANES)
    sinks = row(jnp.repeat(swa_sinks, HEAD_DIM, axis=-1))
    dnw = row(jnp.tile(dn_norm_w, (1, N_HEADS)))
    w_out_b, wgu_b, w_down_b = w_out.astype(BF16), w_gate_up.astype(BF16), w_down.astype(BF16)

    st_dn, st_ckv, st_kpe, st_ssm, st_k, st_v = [], [], [], [], [], []
    for l in range(depth):
        segs = _inproj(geom, xs, ada, row(norm1_w), w_pad, l)
        dn_qkv, dn_z, mla_q, mla_kv, ssm_z, ssm_xbc, swa, small = segs

        dn_of, dn_ob, dn_fin = _deltanet(geom, dn_qkv, small, s0_dn, dn_conv_w, alog_rows, bias_rows, l)
        ssm_yf, ssm_yb, ssm_fin = _ssd(geom, ssm_xbc, small, s0_ssm, ssm_conv_w, row(ssm_conv_b),
                                       alog_rows, bias_rows, ssm_dvec, l)
        qh, kh, vh, ckv = _mla_proj(geom, mla_q, mla_kv, small, *mla_tabs, row(mla_q_norm_w),
                                    row(mla_kv_norm_w), wuq, wk, wv, l)
        o_mla_ctx = _mla_attn(qh, kh, vh, batch, seq, 0)
        o_mla_lat = _mla_attn(qh, kh, vh, dec_batch, dec_seq, n_ctx,
                              cache=(cache_mla_ckv, kpe_c, wk, wv), l=l)
        o_swa_ctx = _swa_ctx(geom, swa, e_mat, sinks, l)
        o_swa_lat = _swa_lat(geom, swa, *swa_tabs, swa_kc, swa_vc, e_mat, sinks, l)

        st_dn.append(dn_fin)
        st_ssm.append(ssm_fin)
        st_ckv.append(ckv.reshape(batch, seq, MLA_KV_LORA))
        st_kpe.append(small[:n_ctx, SM_KPE:SM_KPE + MLA_ROPE].reshape(batch, seq, MLA_ROPE))
        st_k.append(swa[:n_ctx, GROUP_W:GROUP_W + LANES].reshape(batch, seq, 2, HEAD_DIM))
        st_v.append(swa[:n_ctx, GROUP_W + LANES:].reshape(batch, seq, 2, HEAD_DIM))

        parts = (dn_of, dn_ob, dn_z, ssm_yf, ssm_yb, ssm_z, o_mla_ctx, o_mla_lat, o_swa_ctx, o_swa_lat)
        xs = _mix_ffn(geom, xs, ada, parts, dnw, row(ssm_norm_w), w_out_b, row(norm2_w), wgu_b, w_down_b,
                      final_norm_w.reshape(1, d), l, l == depth - 1)

    return (xs[0].reshape(batch, seq, d), xs[1].reshape(dec_batch, dec_seq, d),
            jnp.stack(st_dn, axis=1), jnp.stack(st_ckv, axis=1), jnp.stack(st_kpe, axis=1),
            jnp.stack(st_ssm, axis=1), jnp.stack(st_k, axis=1), jnp.stack(st_v, axis=1))
```

```python
import functools

import jax
import jax.numpy as jnp
from jax import lax
from jax.experimental import pallas as pl
from jax.experimental.pallas import tpu as pltpu

F32 = jnp.float32
BF16 = jnp.bfloat16

D_MODEL = 1024
GRID_W = 64
HEAD_DIM = 64
GROUP_W = 256
EPS = 1e-6
ROPE_THETA = 10000.0
NEG_INF = -1e30
N_HEADS = 4
MLA_NOPE = 64
MLA_ROPE = 32
MLA_Q_LORA = 256
MLA_KV_LORA = 128
MLA_SCALE = (MLA_NOPE + MLA_ROPE) ** -0.5
SSM_N = 64
SWA_SCALE = HEAD_DIM ** -0.5
LOG2_E = 1.4426950408889634
SWA_BLOCK = 128
FF_DIM = 2816
FF_CHUNK = 256
KEY_CHUNKS = 4

LANES = 128
SUBLANES = 8
SEQ_TILE = 256
ROW_TILE = 512
VMEM_LIMIT = 56 * 1024 * 1024

SEG_WIDTHS = (768, 256, 256, 128, 256, 512, 512, 128)
IN_PAD = sum(SEG_WIDTHS)
SM_BETA, SM_ALPHA, SM_DT, SM_KPE = 0, 8, 16, 64


def _sigmoid(x):
    return 1.0 / (1.0 + jnp.exp(-x))


def _silu(x):
    return x * _sigmoid(x)


def _softplus(x):
    return jnp.maximum(x, 0.0) + jnp.log1p(jnp.exp(-jnp.abs(x)))


def _mm(a, b):
    return jnp.dot(a.astype(BF16), b.astype(BF16), preferred_element_type=F32)


def _mm_nt(a, b):
    return lax.dot_general(a.astype(BF16), b.astype(BF16), (((1,), (1,)), ((), ())),
                           preferred_element_type=F32)


def _mm_tn(a, b):
    return lax.dot_general(a.astype(BF16), b.astype(BF16), (((0,), (0,)), ((), ())),
                           preferred_element_type=F32)


def _split(a, parts):
    out = []
    for _ in range(parts):
        hi = a.astype(BF16)
        out.append(hi)
        a = a - hi.astype(F32)
    return out


def _mm_split_lhs(a, b_exact, parts, nt=False):
    dims = (((1,), (1,)), ((), ())) if nt else (((1,), (0,)), ((), ()))
    acc = None
    for piece in _split(a, parts):
        r = lax.dot_general(piece, b_exact, dims, preferred_element_type=F32)
        acc = r if acc is None else acc + r
    return acc


def _mm_split_rhs(a_exact, b, parts):
    acc = None
    for piece in _split(b, parts):
        r = jnp.dot(a_exact, piece, preferred_element_type=F32)
        acc = r if acc is None else acc + r
    return acc


def _ones_where(mask, dtype):
    return jnp.where(mask, 1.0, 0.0).astype(dtype)


def _rms(x, w):
    return x * lax.rsqrt(jnp.mean(x * x, axis=-1, keepdims=True) + EPS) * w


def _cparams(sem):
    return pltpu.CompilerParams(dimension_semantics=sem, vmem_limit_bytes=VMEM_LIMIT)


def _ada_kernel(c_ref, w_ref, b_ref, o_ref):
    o_ref[0] = _mm(_silu(c_ref[...]), w_ref[0]) + b_ref[0]


def _ada(cc, w_ada, b_ada):
    depth, d, n = w_ada.shape
    tn = 1536
    return pl.pallas_call(
        _ada_kernel,
        out_shape=jax.ShapeDtypeStruct((depth, cc.shape[0], n), F32),
        grid=(depth, n // tn),
        in_specs=[pl.BlockSpec(cc.shape, lambda l, j: (0, 0)),
                  pl.BlockSpec((1, d, tn), lambda l, j: (l, 0, j)),
                  pl.BlockSpec((1, 1, tn), lambda l, j: (l, 0, j))],
        out_specs=pl.BlockSpec((1, cc.shape[0], tn), lambda l, j: (l, 0, j)),
        compiler_params=_cparams(("arbitrary", "arbitrary")),
        name="ada",
    )(cc, w_ada, b_ada.reshape(depth, 1, n))


class _Geom:
    def __init__(self, n_ctx_seq, ctx_len, n_lat_seq, lat_len):
        self.n_ctx_seq, self.ctx_len = n_ctx_seq, ctx_len
        self.n_lat_seq, self.lat_len = n_lat_seq, lat_len
        self.n_ctx = n_ctx_seq * ctx_len
        self.n_lat = n_lat_seq * lat_len
        self.n_tok = self.n_ctx + self.n_lat
        assert ctx_len % SEQ_TILE == 0 and lat_len % ROW_TILE == 0 and self.n_ctx % ROW_TILE == 0
        self.cps = ctx_len // SEQ_TILE
        self.lps = lat_len // SEQ_TILE
        self.n_ctx_tiles = n_ctx_seq * self.cps
        self.n_tiles = self.n_ctx_tiles + n_lat_seq * self.lps
        self.n_seq = n_ctx_seq + n_lat_seq

    def mod_row(self, i):
        r = i * ROW_TILE
        return jnp.where(r >= self.n_ctx, 1 + (r - self.n_ctx) // self.lat_len, 0)

    def tile_pos(self, t):
        is_lat = t >= self.n_ctx_tiles
        u = t - self.n_ctx_tiles
        seq = jnp.where(is_lat, self.n_ctx_seq + u // self.lps, t // self.cps)
        pos = jnp.where(is_lat, u % self.lps, t % self.cps)
        nt = jnp.where(is_lat, self.lps, self.cps)
        return seq, pos, nt

    def mirror(self, t):
        _, pos, nt = self.tile_pos(t)
        return t - pos + (nt - 1 - pos)

    def split_specs(self, width):
        nc = self.n_ctx // ROW_TILE
        return [pl.BlockSpec((ROW_TILE, width), lambda i: (jnp.minimum(i, nc - 1), 0)),
                pl.BlockSpec((ROW_TILE, width), lambda i: (jnp.maximum(i - nc, 0), 0))]

    def is_ctx_tile(self, i):
        return i < self.n_ctx // ROW_TILE


def _layer_spec(a, l, **kw):
    return pl.BlockSpec((None,) + a.shape[1:], lambda *_: (l,) + (0,) * (a.ndim - 1), **kw)


def _ada_spec(geom, ada, l):
    return pl.BlockSpec((None, 1) + ada.shape[2:], lambda i: (l, geom.mod_row(i), 0, 0))


def _read_split(ctx_ref, lat_ref, is_ctx):
    return jnp.where(is_ctx, ctx_ref[...], lat_ref[...])


def _inproj_kernel(*refs, geom, n_x):
    x_refs, (ada_ref, nw_ref, w_ref), out_refs = refs[:n_x], refs[n_x:n_x + 3], refs[n_x + 3:]
    x = x_refs[0][...] if n_x == 1 else _read_split(*x_refs, geom.is_ctx_tile(pl.program_id(0)))
    h = _rms(x, nw_ref[...]) * (1.0 + ada_ref[0, 1:2, :]) + ada_ref[0, 0:1, :]
    h = h.astype(BF16)
    off = 0
    for o_ref in out_refs:
        wd = o_ref.shape[-1]
        o_ref[...] = jnp.dot(h, w_ref[:, off:off + wd], preferred_element_type=F32)
        off += wd


def _inproj(geom, xs, ada, norm_w, w_pad, l):
    d = xs[0].shape[1]
    x_specs = [pl.BlockSpec((ROW_TILE, d), lambda i: (i, 0))] if len(xs) == 1 else geom.split_specs(d)
    return pl.pallas_call(
        functools.partial(_inproj_kernel, geom=geom, n_x=len(xs)),
        out_shape=[jax.ShapeDtypeStruct((geom.n_tok, wd), F32) for wd in SEG_WIDTHS],
        grid=(geom.n_tok // ROW_TILE,),
        in_specs=x_specs + [_ada_spec(geom, ada, l), _layer_spec(norm_w, l), _layer_spec(w_pad, l)],
        out_specs=[pl.BlockSpec((ROW_TILE, wd), lambda i: (i, 0)) for wd in SEG_WIDTHS],
        compiler_params=_cparams(("arbitrary",)),
        name="inproj",
    )(*xs, ada, norm_w, w_pad)


def _conv3(x, x_prev, x_next, w_ref):
    n = x.shape[0]
    r = lax.broadcasted_iota(jnp.int32, x.shape, 0)
    x_dn = jnp.where(r == 0, x_prev, pltpu.roll(x, 1, 0))
    x_up = jnp.where(r == n - 1, x_next, pltpu.roll(x, n - 1, 0))
    return w_ref[0:1, :] * x_dn + w_ref[1:2, :] * x + w_ref[2:3, :] * x_up


def _head_lanes(head_l, colfn):
    out = colfn(N_HEADS - 1)
    for h in range(N_HEADS - 2, -1, -1):
        out = jnp.where(head_l == h, colfn(h), out)
    return out


def _cumulative(z, incl_b, parts=3):
    cum = _mm_split_rhs(incl_b, z, parts)
    cumt = _mm_split_lhs(z.T, incl_b, parts, nt=True)
    return cum, cumt


def _unit_tri_inverses(a_list, lvl, eye, out):
    dot = functools.partial(jnp.dot, preferred_element_type=F32)
    a0 = [jnp.where(lvl < 3, a, 0.0) for a in a_list]
    a0b = [a.astype(BF16) for a in a0]
    x = [eye - a for a in a0]
    pb = [dot(a, a).astype(BF16) for a in a0b]
    yield
    x = [xi + dot(xi.astype(BF16), pi) for xi, pi in zip(x, pb)]
    yield
    pb = [dot(pi, pi).astype(BF16) for pi in pb]
    yield
    xb = [(xi + dot(xi.astype(BF16), pi)).astype(BF16) for xi, pi in zip(x, pb)]
    yield
    for m in range(3, a_list[0].shape[0].bit_length() - 1):
        sel = lvl == m
        am = [jnp.where(sel, a, 0.0).astype(BF16) for a in a_list]
        y = [dot(ai, xi).astype(BF16) for ai, xi in zip(am, xb)]
        yield
        xb = [jnp.where(sel, -dot(xi, yi), xi.astype(F32)).astype(BF16) for xi, yi in zip(xb, y)]
        yield
    out.extend(xb)


def _interleave(*gens):
    live = list(gens)
    while live:
        for g in list(live):
            if next(g, StopIteration) is StopIteration:
                live.remove(g)


def _seq_tile_specs(geom, width, mirror):
    rows8 = SEQ_TILE // SUBLANES
    last8 = geom.n_tok // SUBLANES - 1
    tile = (lambda t: geom.mirror(t)) if mirror else (lambda t: t)
    return [pl.BlockSpec((SEQ_TILE, width), lambda t: (tile(t), 0)),
            pl.BlockSpec((SUBLANES, width), lambda t: (jnp.maximum(tile(t) * rows8 - 1, 0), 0)),
            pl.BlockSpec((SUBLANES, width), lambda t: (jnp.minimum((tile(t) + 1) * rows8, last8), 0))]


def _dn_kernel(xf_ref, xfp_ref, xfn_ref, xb_ref, xbp_ref, xbn_ref, gf_ref, gb_ref, s0_ref,
               cw_ref, alog_ref, bias_ref, of_ref, ob_ref, sfin_ref, s_scr, *, geom):
    t = pl.program_id(0)
    seq, pos, nt = geom.tile_pos(t)
    is_ctx = seq < geom.n_ctx_seq
    tt = SEQ_TILE

    @pl.when((pos == 0) & is_ctx)
    def _():
        s_scr[...] = jnp.zeros(s_scr.shape, F32)

    @pl.when((pos == 0) & jnp.logical_not(is_ctx))
    def _():
        s_scr[...] = s0_ref[0]

    row = lax.broadcasted_iota(jnp.int32, (tt, tt), 0)
    col = lax.broadcasted_iota(jnp.int32, (tt, tt), 1)
    xr = row ^ col
    lvl = jnp.where(xr >= 2, 1, 0)
    for kbit in range(2, tt.bit_length() - 1):
        lvl = lvl + jnp.where(xr >= (1 << kbit), 1, 0)
    head_l = lax.broadcasted_iota(jnp.int32, (tt, GROUP_W), 1) // HEAD_DIM
    blockdiag = (lax.broadcasted_iota(jnp.int32, (GROUP_W, GROUP_W), 0) // HEAD_DIM
                 == lax.broadcasted_iota(jnp.int32, (GROUP_W, GROUP_W), 1) // HEAD_DIM)
    eye = _ones_where(row == col, F32)
    gones = _ones_where(blockdiag, BF16)
    lane_g = lax.broadcasted_iota(jnp.int32, (tt, LANES), 1)
    not_first, not_last = pos > 0, pos < nt - 1

    dirs = ((xf_ref, xfp_ref, xfn_ref, gf_ref, of_ref, not_first, not_last),
            (xb_ref, xbp_ref, xbn_ref, gb_ref, ob_ref, not_last, not_first))
    a_lists, x_lists, pre_out = ([], []), ([], []), [None, None]

    def pre(d):
        x_ref, xp_ref, xn_ref, g_ref, o_ref, has_prev, has_next = dirs[d]
        x_prev = jnp.where(has_prev, xp_ref[SUBLANES - 1:SUBLANES, :], 0.0)
        x_next = jnp.where(has_next, xn_ref[0:1, :], 0.0)
        y = _silu(_conv3(x_ref[...], x_prev, x_next, cw_ref))
        yield
        q, k, v = y[:, :GROUP_W], y[:, GROUP_W:2 * GROUP_W], y[:, 2 * GROUP_W:]
        q = q * lax.rsqrt(_mm_split_lhs(q * q, gones, 2) + EPS) * (HEAD_DIM ** -0.5)
        k = k * lax.rsqrt(_mm_split_lhs(k * k, gones, 2) + EPS)
        yield

        s = g_ref[...]
        gate = -jnp.exp(alog_ref[...]) * _softplus(s + bias_ref[...])
        z = jnp.where(lane_g < SM_ALPHA, _sigmoid(s), gate)
        incl = (row >= col) if d == 0 else (row <= col)
        strict = (row > col) if d == 0 else (row < col)
        cum, cumt = _cumulative(z, _ones_where(incl, BF16))
        cum2, cumt2 = cum * LOG2_E, cumt * LOG2_E
        edge = tt - 1 if d == 0 else 0
        c_beta = SM_BETA + N_HEADS * d
        c_g = SM_ALPHA + N_HEADS * d
        yield

        beta_l = _head_lanes(head_l, lambda h: z[:, c_beta + h:c_beta + h + 1])
        cum_l = _head_lanes(head_l, lambda h: cum[:, c_g + h:c_g + h + 1])
        tot_l = _head_lanes(head_l[0:1, :], lambda h: cum[edge:edge + 1, c_g + h:c_g + h + 1])
        eg = jnp.exp(cum_l)
        kb = k * beta_l
        rhs = jnp.concatenate([v * beta_l, kb * eg], axis=1).astype(BF16)
        kbf = k.astype(BF16)
        yield
        qk_heads = []
        for h in range(N_HEADS):
            hm = head_l == h
            diff = cum2[:, c_g + h:c_g + h + 1] - cumt2[c_g + h:c_g + h + 1, :]
            decay = jnp.where(incl, jnp.exp2(diff), 0.0)
            a_lists[d].append(jnp.where(strict, _mm_nt(jnp.where(hm, kb, 0.0), kbf) * decay, 0.0))
            qk_heads.append((_mm_nt(jnp.where(hm, q, 0.0), kbf) * decay).astype(BF16))
            yield
        pre_out[d] = (rhs, qk_heads, (q * eg).astype(BF16), k * jnp.exp(tot_l - cum_l),
                      jnp.exp(tot_l), o_ref)

    def tail(d):
        rhs, qk_heads, qg, kd, e_tot, o_ref = pre_out[d]
        u_all = jnp.zeros((tt, GROUP_W), F32)
        w_all = jnp.zeros((tt, GROUP_W), F32)
        for h in range(N_HEADS):
            hm = head_l == h
            uw = jnp.dot(x_lists[d][h], rhs, preferred_element_type=F32)
            u_all = jnp.where(hm, uw[:, :GROUP_W], u_all)
            w_all = jnp.where(hm, uw[:, GROUP_W:], w_all)
            yield
        state = s_scr[d]
        sb = state.astype(BF16)
        v_new = u_all - _mm(w_all, sb)
        vb = v_new.astype(BF16)
        yield
        o = jnp.dot(qg, sb, preferred_element_type=F32)
        for h in range(N_HEADS):
            o = o + jnp.where(head_l == h, jnp.dot(qk_heads[h], vb, preferred_element_type=F32), 0.0)
            yield
        o_ref[...] = o
        s_scr[d] = state * e_tot + jnp.where(blockdiag, _mm_tn(kd, vb), 0.0)

    _interleave(pre(0), pre(1))
    x_all = []
    _interleave(_unit_tri_inverses(a_lists[0] + a_lists[1], lvl, eye, x_all))
    x_lists[0].extend(x_all[:N_HEADS])
    x_lists[1].extend(x_all[N_HEADS:])
    _interleave(tail(0), tail(1))

    @pl.when((pos == nt - 1) & is_ctx)
    def _():
        for d in range(2):
            for h in range(N_HEADS):
                sl = slice(h * HEAD_DIM, (h + 1) * HEAD_DIM)
                sfin_ref[0, d, h] = s_scr[d, sl, sl]


def _state_specs(geom, block, l):
    seq_of = lambda t: geom.tile_pos(t)[0]
    lat_seq = lambda t: (jnp.maximum(seq_of(t) - geom.n_ctx_seq, 0), l) + (0,) * len(block[0])
    ctx_seq = lambda t: (jnp.minimum(seq_of(t), geom.n_ctx_seq - 1),) + (0,) * len(block[1])
    return pl.BlockSpec((1, None) + block[0], lat_seq), pl.BlockSpec((1,) + block[1], ctx_seq)


def _deltanet(geom, qkv, small, s0, conv_w, alog_row, bias_row, l):
    tt = SEQ_TILE
    s0_spec, sfin_spec = _state_specs(geom, ((2, GROUP_W, GROUP_W), (2, N_HEADS, HEAD_DIM, HEAD_DIM)), l)
    in_specs = (_seq_tile_specs(geom, 3 * GROUP_W, False) + _seq_tile_specs(geom, 3 * GROUP_W, True)
                + [pl.BlockSpec((tt, LANES), lambda t: (t, 0)),
                   pl.BlockSpec((tt, LANES), lambda t: (geom.mirror(t), 0)),
                   s0_spec, _layer_spec(conv_w, l), _layer_spec(alog_row, l), _layer_spec(bias_row, l)])
    return pl.pallas_call(
        functools.partial(_dn_kernel, geom=geom),
        out_shape=[jax.ShapeDtypeStruct((geom.n_tok, GROUP_W), F32),
                   jax.ShapeDtypeStruct((geom.n_tok, GROUP_W), F32),
                   jax.ShapeDtypeStruct((geom.n_ctx_seq, 2, N_HEADS, HEAD_DIM, HEAD_DIM), F32)],
        grid=(geom.n_tiles,),
        in_specs=in_specs,
        out_specs=[pl.BlockSpec((tt, GROUP_W), lambda t: (t, 0)),
                   pl.BlockSpec((tt, GROUP_W), lambda t: (geom.mirror(t), 0)),
                   sfin_spec],
        scratch_shapes=[pltpu.VMEM((2, GROUP_W, GROUP_W), F32)],
        compiler_params=_cparams(("arbitrary",)),
        name="deltanet",
    )(qkv, qkv, qkv, qkv, qkv, qkv, small, small, s0, conv_w, alog_row, bias_row)


def _ssm_kernel(xf_ref, xfp_ref, xfn_ref, xb_ref, xbp_ref, xbn_ref, gf_ref, gb_ref, s0_ref,
                cw_ref, cb_ref, alog_ref, bias_ref, dvec_ref, yf_ref, yb_ref, sfin_ref, s_scr,
                *, geom):
    t = pl.program_id(0)
    seq, pos, nt = geom.tile_pos(t)
    is_ctx = seq < geom.n_ctx_seq
    tt = SEQ_TILE

    @pl.when((pos == 0) & is_ctx)
    def _():
        s_scr[...] = jnp.zeros(s_scr.shape, F32)

    @pl.when((pos == 0) & jnp.logical_not(is_ctx))
    def _():
        s_scr[...] = s0_ref[0]

    row = lax.broadcasted_iota(jnp.int32, (tt, tt), 0)
    col = lax.broadcasted_iota(jnp.int32, (tt, tt), 1)
    head_l = lax.broadcasted_iota(jnp.int32, (tt, GROUP_W), 1) // HEAD_DIM
    lane_g = lax.broadcasted_iota(jnp.int32, (tt, LANES), 1)
    group_l = lane_g // SSM_N
    state_head = lax.broadcasted_iota(jnp.int32, (GROUP_W, 2 * SSM_N), 0) // HEAD_DIM
    state_group = lax.broadcasted_iota(jnp.int32, (GROUP_W, 2 * SSM_N), 1) // SSM_N
    state_mask = state_head // 2 == state_group
    not_first, not_last = pos > 0, pos < nt - 1

    dirs = ((xf_ref, xfp_ref, xfn_ref, gf_ref, yf_ref, not_first, not_last),
            (xb_ref, xbp_ref, xbn_ref, gb_ref, yb_ref, not_last, not_first))
    for d, (x_ref, xp_ref, xn_ref, g_ref, y_ref, has_prev, has_next) in enumerate(dirs):
        x_prev = jnp.where(has_prev, xp_ref[SUBLANES - 1:SUBLANES, :], 0.0)
        x_next = jnp.where(has_next, xn_ref[0:1, :], 0.0)
        y = _silu(_conv3(x_ref[...], x_prev, x_next, cw_ref) + cb_ref[...])
        xs, bm, cm = y[:, :GROUP_W], y[:, GROUP_W:GROUP_W + 2 * SSM_N], y[:, GROUP_W + 2 * SSM_N:]

        dt = _softplus(g_ref[...] + bias_ref[...])
        a = -jnp.exp(alog_ref[...]) * dt
        incl = (row >= col) if d == 0 else (row <= col)
        cum, cumt = _cumulative(a, _ones_where(incl, BF16))
        cum2, cumt2 = cum * LOG2_E, cumt * LOG2_E
        edge = tt - 1 if d == 0 else 0
        c0 = SM_DT + N_HEADS * d

        dt_l = _head_lanes(head_l, lambda h: dt[:, c0 + h:c0 + h + 1])
        cum_l = _head_lanes(head_l, lambda h: cum[:, c0 + h:c0 + h + 1])
        tot_l = _head_lanes(head_l[0:1, :], lambda h: cum[edge:edge + 1, c0 + h:c0 + h + 1])
        xdt = xs * dt_l
        cb_scores = [_mm_nt(jnp.where(group_l == g, cm, 0.0), bm) for g in range(2)]
        out = jnp.zeros((tt, GROUP_W), F32)
        for h in range(N_HEADS):
            diff = cum2[:, c0 + h:c0 + h + 1] - cumt2[c0 + h:c0 + h + 1, :]
            lmat = jnp.where(incl, jnp.exp2(diff), 0.0)
            out = jnp.where(head_l == h, _mm(cb_scores[h // 2] * lmat, xdt), out)
        state = s_scr[d]
        out = out + _mm_nt(cm, state) * jnp.exp(cum_l)
        if d == 0:
            out = out + dvec_ref[...] * xs
        y_ref[...] = out
        tot_rows = _head_lanes(state_head, lambda h: cum[edge:edge + 1, c0 + h:c0 + h + 1])
        s_scr[d] = (state * jnp.exp(tot_rows)
                    + jnp.where(state_mask, _mm_tn(xdt * jnp.exp(tot_l - cum_l), bm), 0.0))

    @pl.when((pos == nt - 1) & is_ctx)
    def _():
        for d in range(2):
            for h in range(N_HEADS):
                g = h // 2
                sfin_ref[0, d, h] = s_scr[d, h * HEAD_DIM:(h + 1) * HEAD_DIM, g * SSM_N:(g + 1) * SSM_N]


def _ssd(geom, xbc, small, s0, conv_w, conv_b, alog_row, bias_row, dvec, l):
    tt = SEQ_TILE
    wx = GROUP_W + 4 * SSM_N
    s0_spec, sfin_spec = _state_specs(geom, ((2, GROUP_W, 2 * SSM_N), (2, N_HEADS, HEAD_DIM, SSM_N)), l)
    in_specs = (_seq_tile_specs(geom, wx, False) + _seq_tile_specs(geom, wx, True)
                + [pl.BlockSpec((tt, LANES), lambda t: (t, 0)),
                   pl.BlockSpec((tt, LANES), lambda t: (geom.mirror(t), 0)),
                   s0_spec, _layer_spec(conv_w, l), _layer_spec(conv_b, l), _layer_spec(alog_row, l),
                   _layer_spec(bias_row, l), _layer_spec(dvec, l)])
    return pl.pallas_call(
        functools.partial(_ssm_kernel, geom=geom),
        out_shape=[jax.ShapeDtypeStruct((geom.n_tok, GROUP_W), F32),
                   jax.ShapeDtypeStruct((geom.n_tok, GROUP_W), F32),
                   jax.ShapeDtypeStruct((geom.n_ctx_seq, 2, N_HEADS, HEAD_DIM, SSM_N), F32)],
        grid=(geom.n_tiles,),
        in_specs=in_specs,
        out_specs=[pl.BlockSpec((tt, GROUP_W), lambda t: (t, 0)),
                   pl.BlockSpec((tt, GROUP_W), lambda t: (geom.mirror(t), 0)),
                   sfin_spec],
        scratch_shapes=[pltpu.VMEM((2, GROUP_W, 2 * SSM_N), F32)],
        compiler_params=_cparams(("arbitrary",)),
        name="ssd",
    )(xbc, xbc, xbc, xbc, xbc, xbc, small, small, s0, conv_w, conv_b, alog_row, bias_row, dvec)


def _rope_slab(x, cos, sin_a, sin_b, half):
    w = x.shape[-1]
    return x * cos + pltpu.roll(x, w - half, 1) * sin_a + pltpu.roll(x, half, 1) * sin_b


def _mla_proj_kernel(ql_ref, kvl_ref, sm_ref, cos_ref, sa_ref, sb_ref, qnw_ref, kvnw_ref,
                     wuq_ref, wk_ref, wv_ref, qh_ref, kh_ref, vh_ref, ckv_ref, *, geom):
    cos, sa, sb = cos_ref[...], sa_ref[...], sb_ref[...]
    half = MLA_ROPE // 2
    qp = _mm(_rms(ql_ref[...], qnw_ref[...]), wuq_ref[...]) * (MLA_SCALE * LOG2_E)
    ckv = _rms(kvl_ref[...], kvnw_ref[...])

    @pl.when(geom.is_ctx_tile(pl.program_id(0)))
    def _():
        ckv_ref[...] = ckv

    lane = lax.broadcasted_iota(jnp.int32, cos.shape, 1)
    is_pe = (lane >= MLA_NOPE) & (lane < MLA_NOPE + MLA_ROPE)
    kpe = jnp.where(is_pe, _rope_slab(sm_ref[...], cos, sa, sb, half), 0.0)
    kp = _mm(ckv, wk_ref[...])
    for h in range(N_HEADS):
        sl = slice(h * LANES, (h + 1) * LANES)
        qh_ref[:, sl] = _rope_slab(qp[:, sl], cos, sa, sb, half).astype(BF16)
        kh_ref[:, sl] = (kp[:, sl] + kpe).astype(BF16)
    vh_ref[...] = _mm(ckv, wv_ref[...]).astype(BF16)


def _mla_proj(geom, q_lat, kv_lat, small, cos, sa, sb, qnw, kvnw, wuq, wk, wv, l):
    tm = ROW_TILE
    tok = lambda w: pl.BlockSpec((tm, w), lambda i: (i, 0))
    full = lambda a: _layer_spec(a, l)

    def tab_block(i):
        r = i * tm
        return jnp.where(r >= geom.n_ctx, 1 + ((r - geom.n_ctx) % geom.lat_len) // tm, 0), 0

    tab = pl.BlockSpec((tm, LANES), tab_block)
    return pl.pallas_call(
        functools.partial(_mla_proj_kernel, geom=geom),
        out_shape=[jax.ShapeDtypeStruct((geom.n_tok, N_HEADS * LANES), BF16),
                   jax.ShapeDtypeStruct((geom.n_tok, N_HEADS * LANES), BF16),
                   jax.ShapeDtypeStruct((geom.n_tok, GROUP_W), BF16),
                   jax.ShapeDtypeStruct((geom.n_ctx, MLA_KV_LORA), F32)],
        grid=(geom.n_tok // tm,),
        in_specs=[tok(MLA_Q_LORA), tok(MLA_KV_LORA), tok(LANES), tab, tab, tab,
                  full(qnw), full(kvnw), full(wuq), full(wk), full(wv)],
        out_specs=[tok(N_HEADS * LANES), tok(N_HEADS * LANES), tok(GROUP_W),
                   geom.split_specs(MLA_KV_LORA)[0]],
        compiler_params=_cparams(("arbitrary",)),
        name="mla_proj",
    )(q_lat, kv_lat, small, cos, sa, sb, qnw, kvnw, wuq, wk, wv)


def _mla_attn_kernel(*refs, has_cache):
    if has_cache:
        q_ref, k_ref, v_ref, ckv_ref, kpe_ref, wk_ref, wv_ref, o_ref = refs
        ckv_c = ckv_ref[0]
        v_c = _mm(ckv_c, wv_ref[...])
    else:
        q_ref, k_ref, v_ref, o_ref = refs
    heads = (slice(0, LANES), slice(LANES, 2 * LANES))
    n_keys = k_ref.shape[0]
    chunk = n_keys // KEY_CHUNKS if n_keys % (KEY_CHUNKS * LANES) == 0 else n_keys
    qs = [q_ref[:, sl] for sl in heads]
    if has_cache:
        scores = [_mm_nt(q, _mm(ckv_c, wk_ref[:, sl]) + kpe_ref[0]) for q, sl in zip(qs, heads)]
        ms = [jnp.max(s, axis=-1, keepdims=True) for s in scores]
        ps = [jnp.exp2(s - m) for s, m in zip(scores, ms)]
        dens = [jnp.sum(p, axis=-1, keepdims=True) for p in ps]
        accs = [_mm(p, v_c) for p in ps]
    for c in range(n_keys // chunk):
        rows = slice(c * chunk, (c + 1) * chunk)
        scores = [_mm_nt(q, k_ref[rows, sl]) for q, sl in zip(qs, heads)]
        v = v_ref[rows, :]
        if c == 0 and not has_cache:
            ms = [jnp.max(s, axis=-1, keepdims=True) for s in scores]
            ps = [jnp.exp2(s - m) for s, m in zip(scores, ms)]
            dens = [jnp.sum(p, axis=-1, keepdims=True) for p in ps]
            accs = [_mm(p, v) for p in ps]
            continue
        for j, s in enumerate(scores):
            m_new = jnp.maximum(ms[j], jnp.max(s, axis=-1, keepdims=True))
            alpha = jnp.exp2(ms[j] - m_new)
            p = jnp.exp2(s - m_new)
            dens[j] = dens[j] * alpha + jnp.sum(p, axis=-1, keepdims=True)
            accs[j] = accs[j] * alpha + _mm(p, v)
            ms[j] = m_new
    outs = [acc / den for acc, den in zip(accs, dens)]
    lane = lax.broadcasted_iota(jnp.int32, outs[0].shape, 1)
    o_ref[...] = jnp.where(lane < HEAD_DIM, outs[0], outs[1])


def _mla_attn(qh, kh, vh, n_seq, seq_len, tok0, cache=None, l=0):
    tq = min(ROW_TILE, seq_len)
    nq = seq_len // tq
    q0, k0 = tok0 // tq, tok0 // seq_len
    assert tok0 % seq_len == 0
    in_specs = [pl.BlockSpec((tq, 2 * LANES), lambda b, hp, i: (q0 + b * nq + i, hp)),
                pl.BlockSpec((seq_len, 2 * LANES), lambda b, hp, i: (k0 + b, hp)),
                pl.BlockSpec((seq_len, LANES), lambda b, hp, i: (k0 + b, hp))]
    args = [qh, kh, vh]
    if cache is not None:
        ckv_c, kpe_c, wk, wv = cache
        past = ckv_c.shape[2]
        in_specs += [pl.BlockSpec((1, None, past, MLA_KV_LORA), lambda b, hp, i: (b, l, 0, 0)),
                     pl.BlockSpec((1, None, past, LANES), lambda b, hp, i: (b, l, 0, 0)),
                     pl.BlockSpec((None, MLA_KV_LORA, 2 * LANES), lambda b, hp, i: (l, 0, hp)),
                     pl.BlockSpec((None, MLA_KV_LORA, LANES), lambda b, hp, i: (l, 0, hp))]
        args += [ckv_c, kpe_c, wk, wv]
    return pl.pallas_call(
        functools.partial(_mla_attn_kernel, has_cache=cache is not None),
        out_shape=jax.ShapeDtypeStruct((n_seq * seq_len, GROUP_W), F32),
        grid=(n_seq, 2, nq),
        in_specs=in_specs,
        out_specs=pl.BlockSpec((tq, LANES), lambda b, hp, i: (b * nq + i, hp)),
        compiler_params=_cparams(("parallel", "parallel", "arbitrary")),
        name="mla_attn_lat" if cache is not None else "mla_attn_ctx",
    )(*args)


def _swa_core(q, k_all, v_all, et_ref, sink_ref, valid_t):
    et = et_ref[...]
    kx = _mm_nt(k_all, et).astype(BF16)
    vxt = _mm_nt(et, v_all).astype(BF16)
    head_l = lax.broadcasted_iota(jnp.int32, q.shape, 1) // HEAD_DIM
    heads = range(N_HEADS)
    st = [_mm_nt(kx, jnp.where(head_l == h, q, 0.0)) for h in heads]
    if valid_t is not None:
        st = [jnp.where(valid_t, x, NEG_INF) for x in st]
    sink = [sink_ref[:, h * HEAD_DIM:h * HEAD_DIM + 1] * LOG2_E for h in heads]
    m = [jnp.maximum(jnp.max(st[h], axis=0, keepdims=True), sink[h]) for h in heads]
    pt = [jnp.exp2(st[h] - m[h]) for h in heads]
    den = [jnp.sum(pt[h], axis=0, keepdims=True) + jnp.exp2(sink[h] - m[h]) for h in heads]
    ot = [jnp.dot(vxt, pt[h].astype(BF16), preferred_element_type=F32) / den[h] for h in heads]
    row_head = lax.broadcasted_iota(jnp.int32, ot[0].shape, 0) // HEAD_DIM
    out_t = ot[N_HEADS - 1]
    for h in range(N_HEADS - 2, -1, -1):
        out_t = jnp.where(row_head == h, ot[h], out_t)
    return out_t.T


def _swa_ctx_kernel(x_ref, e_ref, sink_ref, o_ref):
    x = x_ref[...]
    q = x[:, :GROUP_W] * (SWA_SCALE * LOG2_E)
    k, v = x[:, GROUP_W:GROUP_W + LANES], x[:, GROUP_W + LANES:]
    o_ref[...] = _swa_core(q, k, v, e_ref, sink_ref, None)


def _swa_ctx(geom, swa, e_mat, sink_l, l):
    t = geom.ctx_len
    return pl.pallas_call(
        _swa_ctx_kernel,
        out_shape=jax.ShapeDtypeStruct((geom.n_ctx, GROUP_W), F32),
        grid=(geom.n_ctx_seq,),
        in_specs=[pl.BlockSpec((t, 2 * GROUP_W), lambda b: (b, 0)),
                  pl.BlockSpec(e_mat.shape, lambda b: (0, 0)),
                  _layer_spec(sink_l, l)],
        out_specs=pl.BlockSpec((t, GROUP_W), lambda b: (b, 0)),
        compiler_params=_cparams(("parallel",)),
        name="swa_ctx",
    )(swa, e_mat, sink_l)


def _swa_lat_kernel(xc_ref, xp_ref, xn_ref, cc_ref, ac_ref, bc_ref, cp_ref, ap_ref, bp_ref,
                    cn_ref, an_ref, bn_ref, kc_ref, vc_ref, e_ref, sink_ref, o_ref, *, n_tiles):
    i = pl.program_id(1)
    win = SWA_BLOCK
    tq = xc_ref.shape[0]
    half = HEAD_DIM // 2

    def rope(x, c_ref, a_ref, b_ref):
        reps = x.shape[1] // LANES
        wide = lambda t_ref: jnp.concatenate([t_ref[...]] * reps, axis=1) if reps > 1 else t_ref[...]
        return _rope_slab(x, wide(c_ref), wide(a_ref), wide(b_ref), half)

    ksl, vsl = slice(GROUP_W, GROUP_W + LANES), slice(GROUP_W + LANES, 2 * GROUP_W)
    q = rope(xc_ref[:, :GROUP_W], cc_ref, ac_ref, bc_ref) * (SWA_SCALE * LOG2_E)
    k_all = jnp.concatenate([rope(xp_ref[:, ksl], cp_ref, ap_ref, bp_ref),
                             rope(xc_ref[:, ksl], cc_ref, ac_ref, bc_ref),
                             rope(xn_ref[:, ksl], cn_ref, an_ref, bn_ref),
                             kc_ref[0]], axis=0)
    v_all = jnp.concatenate([xp_ref[:, vsl], xc_ref[:, vsl], xn_ref[:, vsl], vc_ref[0]], axis=0)
    nk = k_all.shape[0]
    n_local = tq + 2 * win
    c = lax.broadcasted_iota(jnp.int32, (nk, tq), 0)
    r = lax.broadcasted_iota(jnp.int32, (nk, tq), 1)
    in_seq = ((c >= win) | (i > 0)) & ((c < win + tq) | (i < n_tiles - 1))
    valid_t = ((c >= r) & (c <= r + 2 * win) & in_seq) | (c >= n_local)
    o_ref[...] = _swa_core(q, k_all, v_all, e_ref, sink_ref, valid_t)


def _swa_lat(geom, swa, cos, sa, sb, k_cache, v_cache, e_mat, sink_l, l):
    win = SWA_BLOCK
    tq = 2 * win
    n_tiles = geom.lat_len // tq
    nblk = geom.lat_len // win
    t0, b0 = geom.n_ctx // tq, geom.n_ctx // win
    past = k_cache.shape[2]
    prv = lambda i: jnp.maximum(2 * i - 1, 0)
    nxt = lambda i: jnp.minimum(2 * i + 2, nblk - 1)
    tok = lambda w: [pl.BlockSpec((tq, w), lambda b, i: (t0 + b * n_tiles + i, 0)),
                     pl.BlockSpec((win, w), lambda b, i: (b0 + b * nblk + prv(i), 0)),
                     pl.BlockSpec((win, w), lambda b, i: (b0 + b * nblk + nxt(i), 0))]
    tab = lambda f, rows: [pl.BlockSpec((rows, LANES), lambda b, i: (f(i), 0))] * 3
    return pl.pallas_call(
        functools.partial(_swa_lat_kernel, n_tiles=n_tiles),
        out_shape=jax.ShapeDtypeStruct((geom.n_lat, GROUP_W), F32),
        grid=(geom.n_lat_seq, n_tiles),
        in_specs=tok(2 * GROUP_W) + tab(lambda i: i, tq) + tab(prv, win) + tab(nxt, win)
        + [pl.BlockSpec((1, None, past, LANES), lambda b, i: (b, l, 0, 0)),
           pl.BlockSpec((1, None, past, LANES), lambda b, i: (b, l, 0, 0)),
           pl.BlockSpec(e_mat.shape, lambda b, i: (0, 0)),
           _layer_spec(sink_l, l)],
        out_specs=pl.BlockSpec((tq, GROUP_W), lambda b, i: (b * n_tiles + i, 0)),
        compiler_params=_cparams(("parallel", "arbitrary")),
        name="swa_lat",
    )(swa, swa, swa, cos, sa, sb, cos, sa, sb, cos, sa, sb, k_cache, v_cache, e_mat, sink_l)


def _mix_ffn_kernel(*refs, geom, n_x, final):
    x_refs, refs = refs[:n_x], refs[n_x:]
    (ada_ref, dof_ref, dob_ref, dz_ref, syf_ref, syb_ref, sz_ref, omc_ref, oml_ref, osc_ref, osl_ref,
     dnw_ref, snw_ref, wo_ref, nw_ref, wgu_ref, wd_ref, fw_ref), out_refs = refs[:18], refs[18:]
    is_ctx = geom.is_ctx_tile(pl.program_id(0))
    x = x_refs[0][...] if n_x == 1 else _read_split(*x_refs, is_ctx)

    row = lax.broadcasted_iota(jnp.int32, (GROUP_W, GROUP_W), 0)
    col = lax.broadcasted_iota(jnp.int32, (GROUP_W, GROUP_W), 1)
    gones = _ones_where((row // HEAD_DIM) == (col // HEAD_DIM), BF16)
    o = dof_ref[...] + dob_ref[...]
    ms = _mm_split_lhs(o * o, gones, 2) * (1.0 / HEAD_DIM)
    dn = o * lax.rsqrt(ms + EPS) * dnw_ref[...] * _silu(dz_ref[...])
    acc = _mm(dn, wo_ref[0:GROUP_W, :])
    acc = acc + _mm(_read_split(omc_ref, oml_ref, is_ctx), wo_ref[GROUP_W:2 * GROUP_W, :])
    y = (syf_ref[...] + syb_ref[...]) * _silu(sz_ref[...])
    for g in range(2):
        sl = slice(g * LANES, (g + 1) * LANES)
        acc = acc + _mm(_rms(y[:, sl], snw_ref[:, sl]),
                        wo_ref[2 * GROUP_W + g * LANES:2 * GROUP_W + (g + 1) * LANES, :])
    acc = acc + _mm(_read_split(osc_ref, osl_ref, is_ctx), wo_ref[3 * GROUP_W:, :])
    x = x + ada_ref[0, 2:3, :] * acc

    h = (_rms(x, nw_ref[...]) * (1.0 + ada_ref[0, 4:5, :]) + ada_ref[0, 3:4, :]).astype(BF16)
    acc = jnp.zeros(x.shape, F32)
    for c in range(FF_DIM // FF_CHUNK):
        g = jnp.dot(h, wgu_ref[:, c * FF_CHUNK:(c + 1) * FF_CHUNK], preferred_element_type=F32)
        u = jnp.dot(h, wgu_ref[:, FF_DIM + c * FF_CHUNK:FF_DIM + (c + 1) * FF_CHUNK],
                    preferred_element_type=F32)
        acc = acc + _mm(_silu(g) * u, wd_ref[c * FF_CHUNK:(c + 1) * FF_CHUNK, :])
    y = x + ada_ref[0, 5:6, :] * acc
    if not final:
        out_refs[0][...] = y
        return
    y = _rms(y, fw_ref[...])

    @pl.when(is_ctx)
    def _():
        out_refs[0][...] = y

    @pl.when(jnp.logical_not(is_ctx))
    def _():
        out_refs[1][...] = y


def _mix_ffn(geom, xs, ada, parts, dnw, snw, w_out, norm_w, wgu, w_down, final_w, l, final):
    tm = ROW_TILE
    d = xs[0].shape[1]
    tok = lambda w: pl.BlockSpec((tm, w), lambda i: (i, 0))
    resident = lambda a: _layer_spec(a, l, pipeline_mode=pl.Buffered(1))
    x_specs = [tok(d)] if len(xs) == 1 else geom.split_specs(d)
    if final:
        out_shape = [jax.ShapeDtypeStruct((geom.n_ctx, d), F32), jax.ShapeDtypeStruct((geom.n_lat, d), F32)]
        out_specs = geom.split_specs(d)
    else:
        out_shape = [jax.ShapeDtypeStruct((geom.n_tok, d), F32)]
        out_specs = [tok(d)]
    return pl.pallas_call(
        functools.partial(_mix_ffn_kernel, geom=geom, n_x=len(xs), final=final),
        out_shape=out_shape,
        grid=(geom.n_tok // tm,),
        in_specs=x_specs + [_ada_spec(geom, ada, l)]
        + [tok(GROUP_W)] * 6 + geom.split_specs(GROUP_W) + geom.split_specs(GROUP_W)
        + [resident(dnw), resident(snw), resident(w_out), resident(norm_w), resident(wgu),
           resident(w_down), pl.BlockSpec(final_w.shape, lambda i: (0, 0))],
        out_specs=out_specs,
        compiler_params=_cparams(("arbitrary",)),
        name="mix_ffn",
    )(*xs, ada, *parts, dnw, snw, w_out, norm_w, wgu, w_down, final_w)


def _w_in_layout(w):
    dn, mla, ssm, swa = 0, 1040, 1456, 2232
    zeros = lambda n: jnp.zeros(w.shape[:-1] + (n,), w.dtype)
    small = jnp.concatenate([
        w[..., dn + 1024:dn + 1040],
        w[..., ssm + 768:ssm + 776],
        zeros(SM_KPE - 24),
        w[..., mla + 384:mla + 416],
        zeros(LANES - SM_KPE - MLA_ROPE)], axis=-1)
    return jnp.concatenate([
        w[..., dn:dn + 768], w[..., dn + 768:dn + 1024],
        w[..., mla:mla + 256], w[..., mla + 256:mla + 384],
        w[..., ssm:ssm + 256], w[..., ssm + 256:ssm + 768],
        w[..., swa:swa + 512], small], axis=-1).astype(BF16)


def _gate_rows(dn_vec, ssm_vec):
    depth = dn_vec.shape[0]
    rows = jnp.zeros((depth, 1, LANES), F32)
    rows = rows.at[:, 0, SM_ALPHA:SM_ALPHA + 8].set(dn_vec.reshape(depth, 8))
    return rows.at[:, 0, SM_DT:SM_DT + 8].set(ssm_vec.reshape(depth, 8))


def _axial_angles(rows, rot_dim):
    row_ids = jnp.broadcast_to(jnp.arange(rows)[:, None], (rows, GRID_W)).reshape(-1).astype(F32)
    col_ids = jnp.broadcast_to(jnp.arange(GRID_W)[None, :], (rows, GRID_W)).reshape(-1).astype(F32)
    n_freq = rot_dim // 4
    inv_freq = ROPE_THETA ** (-jnp.arange(n_freq, dtype=F32) / n_freq)
    return jnp.concatenate([row_ids[:, None] * inv_freq, col_ids[:, None] * inv_freq], axis=-1)


def _rope_tables(ang, lane0, reps, n_ident):
    n, half = ang.shape
    cos, sin = jnp.cos(ang), jnp.sin(ang)
    zeros = jnp.zeros_like(sin)
    period = LANES // reps

    def table(first, second, fill):
        one = jnp.concatenate([jnp.full((n, lane0), fill, F32), first, second,
                               jnp.full((n, period - lane0 - 2 * half), fill, F32)], axis=1)
        tab = jnp.concatenate([one] * reps, axis=1)
        ident = jnp.full((n_ident, LANES), fill, F32)
        return jnp.concatenate([ident, tab], axis=0)

    return table(cos, cos, 1.0), table(-sin, zeros, 0.0), table(zeros, sin, 0.0)


def kernel(x_prompt, x_sample, c, state_dn, cache_mla_ckv, cache_mla_kpe, state_ssm, cache_swa_k,
           cache_swa_v, c_ctx, norm1_w, norm2_w, w_ada, b_ada, w_in, w_out, dn_conv_w, dn_a_log,
           dn_dt_bias, dn_norm_w, mla_q_norm_w, mla_w_uq, mla_kv_norm_w, mla_w_ukv, ssm_conv_w,
           ssm_conv_b, ssm_a_log, ssm_dt_bias, ssm_d, ssm_norm_w, swa_sinks, w_gate_up, w_down,
           final_norm_w):
    batch, seq, d = x_prompt.shape
    dec_batch, dec_seq, _ = x_sample.shape
    depth = w_in.shape[0]
    geom = _Geom(batch, seq, dec_batch, dec_seq)
    n_ctx = geom.n_ctx

    xs = (x_prompt.reshape(n_ctx, d), x_sample.reshape(geom.n_lat, d))
    n_mod = -(-(1 + dec_batch) // SUBLANES) * SUBLANES
    cc = jnp.concatenate([c_ctx[None], c, jnp.zeros((n_mod - 1 - dec_batch, d), F32)], axis=0)
    ada = _ada(cc, w_ada, b_ada).reshape(depth, n_mod, 6, d)

    mla_tabs = _rope_tables(_axial_angles(dec_seq // GRID_W, MLA_ROPE), MLA_NOPE, 1, ROW_TILE)
    swa_tabs = _rope_tables(_axial_angles(dec_seq // GRID_W, HEAD_DIM), 0, 2, 0)

    lane = jnp.arange(GROUP_W)
    e_mat = ((lane // LANES) * HEAD_DIM + lane % HEAD_DIM)[:, None] == jnp.arange(LANES)[None, :]
    e_mat = e_mat.astype(BF16)
    eye_h = jnp.eye(N_HEADS, dtype=F32)
    grp_h = (jnp.arange(N_HEADS)[:, None] // 2 == jnp.arange(2)[None, :]).astype(F32)
    row = lambda a: a.reshape(depth, 1, -1)
    w_pad = _w_in_layout(w_in)
    alog_rows = _gate_rows(dn_a_log, ssm_a_log)
    bias_rows = _gate_rows(dn_dt_bias, ssm_dt_bias)
    s0_dn = state_dn[:, :, :, :, :, None, :] * eye_h[None, None, None, :, None, :, None]
    s0_dn = s0_dn.reshape(dec_batch, depth, 2, GROUP_W, GROUP_W)
    s0_ssm = state_ssm[:, :, :, :, :, None, :] * grp_h[None, None, None, :, None, :, None]
    s0_ssm = s0_ssm.reshape(dec_batch, depth, 2, GROUP_W, 2 * SSM_N)
    ssm_dvec = row(jnp.repeat(ssm_d, HEAD_DIM, axis=-1))
    uq = mla_w_uq.reshape(depth, MLA_Q_LORA, N_HEADS, MLA_NOPE + MLA_ROPE)
    wuq = jnp.pad(uq, ((0, 0), (0, 0), (0, 0), (0, LANES - MLA_NOPE - MLA_ROPE)))
    wuq = wuq.reshape(depth, MLA_Q_LORA, N_HEADS * LANES).astype(BF16)
    ukv = mla_w_ukv.reshape(depth, MLA_KV_LORA, N_HEADS, MLA_NOPE + HEAD_DIM)
    wk = jnp.pad(ukv[..., :MLA_NOPE], ((0, 0), (0, 0), (0, 0), (0, LANES - MLA_NOPE)))
    wk = wk.reshape(depth, MLA_KV_LORA, N_HEADS * LANES).astype(BF16)
    wv = ukv[..., MLA_NOPE:].reshape(depth, MLA_KV_LORA, GROUP_W).astype(BF16)
    kpe_c = jnp.pad(cache_mla_kpe, ((0, 0), (0, 0), (0, 0), (MLA_NOPE, LANES - MLA_NOPE - MLA_ROPE)))
    past = cache_swa_k.shape[2]
    swa_kc = cache_swa_k.reshape(dec_batch, depth, past, LANES)
    swa_vc = cache_swa_v.reshape(dec_batch, depth, past, LANES)
    sinks = row(jnp.repeat(swa_sinks, HEAD_DIM, axis=-1))
    dnw = row(jnp.tile(dn_norm_w, (1, N_HEADS)))
    w_out_b, wgu_b, w_down_b = w_out.astype(BF16), w_gate_up.astype(BF16), w_down.astype(BF16)

    st_dn, st_ckv, st_kpe, st_ssm, st_k, st_v = [], [], [], [], [], []
    for l in range(depth):
        segs = _inproj(geom, xs, ada, row(norm1_w), w_pad, l)
        dn_qkv, dn_z, mla_q, mla_kv, ssm_z, ssm_xbc, swa, small = segs

        dn_of, dn_ob, dn_fin = _deltanet(geom, dn_qkv, small, s0_dn, dn_conv_w, alog_rows, bias_rows, l)
        ssm_yf, ssm_yb, ssm_fin = _ssd(geom, ssm_xbc, small, s0_ssm, ssm_conv_w, row(ssm_conv_b),
                                       alog_rows, bias_rows, ssm_dvec, l)
        qh, kh, vh, ckv = _mla_proj(geom, mla_q, mla_kv, small, *mla_tabs, row(mla_q_norm_w),
                                    row(mla_kv_norm_w), wuq, wk, wv, l)
        o_mla_ctx = _mla_attn(qh, kh, vh, batch, seq, 0)
        o_mla_lat = _mla_attn(qh, kh, vh, dec_batch, dec_seq, n_ctx,
                              cache=(cache_mla_ckv, kpe_c, wk, wv), l=l)
        o_swa_ctx = _swa_ctx(geom, swa, e_mat, sinks, l)
        o_swa_lat = _swa_lat(geom, swa, *swa_tabs, swa_kc, swa_vc, e_mat, sinks, l)

        st_dn.append(dn_fin)
        st_ssm.append(ssm_fin)
        st_ckv.append(ckv.reshape(batch, seq, MLA_KV_LORA))
        st_kpe.append(small[:n_ctx, SM_KPE:SM_KPE + MLA_ROPE].reshape(batch, seq, MLA_ROPE))
        st_k.append(swa[:n_ctx, GROUP_W:GROUP_W + LANES].reshape(batch, seq, 2, HEAD_DIM))
        st_v.append(swa[:n_ctx, GROUP_W + LANES:].reshape(batch, seq, 2, HEAD_DIM))

        parts = (dn_of, dn_ob, dn_z, ssm_yf, ssm_yb, ssm_z, o_mla_ctx, o_mla_lat, o_swa_ctx, o_swa_lat)
        xs = _mix_ffn(geom, xs, ada, parts, dnw, row(ssm_norm_w), w_out_b, row(norm2_w), wgu_b, w_down_b,
                      final_norm_w.reshape(1, d), l, l == depth - 1)

    return (xs[0].reshape(batch, seq, d), xs[1].reshape(dec_batch, dec_seq, d),
            jnp.stack(st_dn, axis=1), jnp.stack(st_ckv, axis=1), jnp.stack(st_kpe, axis=1),
            jnp.stack(st_ssm, axis=1), jnp.stack(st_k, axis=1), jnp.stack(st_v, axis=1))
```

```python
import functools

import jax
import jax.numpy as jnp
from jax import lax
from jax.experimental import pallas as pl
from jax.experimental.pallas import tpu as pltpu

F32 = jnp.float32
BF16 = jnp.bfloat16

D_MODEL = 1024
GRID_W = 64
HEAD_DIM = 64
GROUP_W = 256
EPS = 1e-6
ROPE_THETA = 10000.0
NEG_INF = -1e30
N_HEADS = 4
MLA_NOPE = 64
MLA_ROPE = 32
MLA_Q_LORA = 256
MLA_KV_LORA = 128
MLA_SCALE = (MLA_NOPE + MLA_ROPE) ** -0.5
SSM_N = 64
SWA_SCALE = HEAD_DIM ** -0.5
LOG2_E = 1.4426950408889634
SWA_BLOCK = 128
FF_DIM = 2816
FF_CHUNK = 256
KEY_CHUNKS = 4

LANES = 128
SUBLANES = 8
SEQ_TILE = 256
ROW_TILE = 512
VMEM_LIMIT = 56 * 1024 * 1024

SEG_WIDTHS = (768, 256, 256, 128, 256, 512, 512, 128)
IN_PAD = sum(SEG_WIDTHS)
SM_BETA, SM_ALPHA, SM_DT, SM_KPE = 0, 8, 16, 64


def _sigmoid(x):
    return 1.0 / (1.0 + jnp.exp(-x))


def _silu(x):
    return x * _sigmoid(x)


def _softplus(x):
    return jnp.maximum(x, 0.0) + jnp.log1p(jnp.exp(-jnp.abs(x)))


def _mm(a, b):
    return jnp.dot(a.astype(BF16), b.astype(BF16), preferred_element_type=F32)


def _mm_nt(a, b):
    return lax.dot_general(a.astype(BF16), b.astype(BF16), (((1,), (1,)), ((), ())),
                           preferred_element_type=F32)


def _mm_tn(a, b):
    return lax.dot_general(a.astype(BF16), b.astype(BF16), (((0,), (0,)), ((), ())),
                           preferred_element_type=F32)


def _split(a, parts):
    out = []
    for _ in range(parts):
        hi = a.astype(BF16)
        out.append(hi)
        a = a - hi.astype(F32)
    return out


def _mm_split_lhs(a, b_exact, parts, nt=False):
    dims = (((1,), (1,)), ((), ())) if nt else (((1,), (0,)), ((), ()))
    acc = None
    for piece in _split(a, parts):
        r = lax.dot_general(piece, b_exact, dims, preferred_element_type=F32)
        acc = r if acc is None else acc + r
    return acc


def _mm_split_rhs(a_exact, b, parts):
    acc = None
    for piece in _split(b, parts):
        r = jnp.dot(a_exact, piece, preferred_element_type=F32)
        acc = r if acc is None else acc + r
    return acc


def _ones_where(mask, dtype):
    return jnp.where(mask, 1.0, 0.0).astype(dtype)


def _rms(x, w):
    return x * lax.rsqrt(jnp.mean(x * x, axis=-1, keepdims=True) + EPS) * w


def _cparams(sem):
    return pltpu.CompilerParams(dimension_semantics=sem, vmem_limit_bytes=VMEM_LIMIT)


def _ada_kernel(c_ref, w_ref, b_ref, o_ref):
    o_ref[0] = _mm(_silu(c_ref[...]), w_ref[0]) + b_ref[0]


def _ada(cc, w_ada, b_ada):
    depth, d, n = w_ada.shape
    tn = 1536
    return pl.pallas_call(
        _ada_kernel,
        out_shape=jax.ShapeDtypeStruct((depth, cc.shape[0], n), F32),
        grid=(depth, n // tn),
        in_specs=[pl.BlockSpec(cc.shape, lambda l, j: (0, 0)),
                  pl.BlockSpec((1, d, tn), lambda l, j: (l, 0, j)),
                  pl.BlockSpec((1, 1, tn), lambda l, j: (l, 0, j))],
        out_specs=pl.BlockSpec((1, cc.shape[0], tn), lambda l, j: (l, 0, j)),
        compiler_params=_cparams(("arbitrary", "arbitrary")),
        name="ada",
    )(cc, w_ada, b_ada.reshape(depth, 1, n))


class _Geom:
    def __init__(self, n_ctx_seq, ctx_len, n_lat_seq, lat_len):
        self.n_ctx_seq, self.ctx_len = n_ctx_seq, ctx_len
        self.n_lat_seq, self.lat_len = n_lat_seq, lat_len
        self.n_ctx = n_ctx_seq * ctx_len
        self.n_lat = n_lat_seq * lat_len
        self.n_tok = self.n_ctx + self.n_lat
        assert ctx_len % SEQ_TILE == 0 and lat_len % ROW_TILE == 0 and self.n_ctx % ROW_TILE == 0
        self.cps = ctx_len // SEQ_TILE
        self.lps = lat_len // SEQ_TILE
        self.n_ctx_tiles = n_ctx_seq * self.cps
        self.n_tiles = self.n_ctx_tiles + n_lat_seq * self.lps
        self.n_seq = n_ctx_seq + n_lat_seq

    def mod_row(self, i):
        r = i * ROW_TILE
        return jnp.where(r >= self.n_ctx, 1 + (r - self.n_ctx) // self.lat_len, 0)

    def tile_pos(self, t):
        is_lat = t >= self.n_ctx_tiles
        u = t - self.n_ctx_tiles
        seq = jnp.where(is_lat, self.n_ctx_seq + u // self.lps, t // self.cps)
        pos = jnp.where(is_lat, u % self.lps, t % self.cps)
        nt = jnp.where(is_lat, self.lps, self.cps)
        return seq, pos, nt

    def mirror(self, t):
        _, pos, nt = self.tile_pos(t)
        return t - pos + (nt - 1 - pos)

    def split_specs(self, width):
        nc = self.n_ctx // ROW_TILE
        return [pl.BlockSpec((ROW_TILE, width), lambda i: (jnp.minimum(i, nc - 1), 0)),
                pl.BlockSpec((ROW_TILE, width), lambda i: (jnp.maximum(i - nc, 0), 0))]

    def is_ctx_tile(self, i):
        return i < self.n_ctx // ROW_TILE


def _layer_spec(a, l, **kw):
    return pl.BlockSpec((None,) + a.shape[1:], lambda *_: (l,) + (0,) * (a.ndim - 1), **kw)


def _ada_spec(geom, ada, l):
    return pl.BlockSpec((None, 1) + ada.shape[2:], lambda i: (l, geom.mod_row(i), 0, 0))


def _read_split(ctx_ref, lat_ref, is_ctx):
    return jnp.where(is_ctx, ctx_ref[...], lat_ref[...])


def _inproj_kernel(*refs, geom, n_x):
    x_refs, (ada_ref, nw_ref, w_ref), out_refs = refs[:n_x], refs[n_x:n_x + 3], refs[n_x + 3:]
    x = x_refs[0][...] if n_x == 1 else _read_split(*x_refs, geom.is_ctx_tile(pl.program_id(0)))
    h = _rms(x, nw_ref[...]) * (1.0 + ada_ref[0, 1:2, :]) + ada_ref[0, 0:1, :]
    h = h.astype(BF16)
    off = 0
    for o_ref in out_refs:
        wd = o_ref.shape[-1]
        o_ref[...] = jnp.dot(h, w_ref[:, off:off + wd], preferred_element_type=F32)
        off += wd


def _inproj(geom, xs, ada, norm_w, w_pad, l):
    d = xs[0].shape[1]
    x_specs = [pl.BlockSpec((ROW_TILE, d), lambda i: (i, 0))] if len(xs) == 1 else geom.split_specs(d)
    return pl.pallas_call(
        functools.partial(_inproj_kernel, geom=geom, n_x=len(xs)),
        out_shape=[jax.ShapeDtypeStruct((geom.n_tok, wd), F32) for wd in SEG_WIDTHS],
        grid=(geom.n_tok // ROW_TILE,),
        in_specs=x_specs + [_ada_spec(geom, ada, l), _layer_spec(norm_w, l), _layer_spec(w_pad, l)],
        out_specs=[pl.BlockSpec((ROW_TILE, wd), lambda i: (i, 0)) for wd in SEG_WIDTHS],
        compiler_params=_cparams(("arbitrary",)),
        name="inproj",
    )(*xs, ada, norm_w, w_pad)


def _conv3(x, x_prev, x_next, w_ref):
    n = x.shape[0]
    r = lax.broadcasted_iota(jnp.int32, x.shape, 0)
    x_dn = jnp.where(r == 0, x_prev, pltpu.roll(x, 1, 0))
    x_up = jnp.where(r == n - 1, x_next, pltpu.roll(x, n - 1, 0))
    return w_ref[0:1, :] * x_dn + w_ref[1:2, :] * x + w_ref[2:3, :] * x_up


def _head_lanes(head_l, colfn):
    out = colfn(N_HEADS - 1)
    for h in range(N_HEADS - 2, -1, -1):
        out = jnp.where(head_l == h, colfn(h), out)
    return out


def _cumulative(z, incl_b, parts=2):
    cum = _mm_split_rhs(incl_b, z, parts)
    cumt = _mm_split_lhs(z.T, incl_b, parts, nt=True)
    return cum, cumt


def _unit_tri_inverses(a_list, lvl, eye, out):
    dot = functools.partial(jnp.dot, preferred_element_type=F32)
    a0 = [jnp.where(lvl < 3, a, 0.0) for a in a_list]
    a0b = [a.astype(BF16) for a in a0]
    x = [eye - a for a in a0]
    pb = [dot(a, a).astype(BF16) for a in a0b]
    yield
    x = [xi + dot(xi.astype(BF16), pi) for xi, pi in zip(x, pb)]
    yield
    pb = [dot(pi, pi).astype(BF16) for pi in pb]
    yield
    xb = [(xi + dot(xi.astype(BF16), pi)).astype(BF16) for xi, pi in zip(x, pb)]
    yield
    for m in range(3, a_list[0].shape[0].bit_length() - 1):
        sel = lvl == m
        am = [jnp.where(sel, a, 0.0).astype(BF16) for a in a_list]
        y = [dot(ai, xi).astype(BF16) for ai, xi in zip(am, xb)]
        yield
        xb = [jnp.where(sel, -dot(xi, yi), xi.astype(F32)).astype(BF16) for xi, yi in zip(xb, y)]
        yield
    out.extend(xb)


def _interleave(*gens):
    live = list(gens)
    while live:
        for g in list(live):
            if next(g, StopIteration) is StopIteration:
                live.remove(g)


def _seq_tile_specs(geom, width, mirror):
    rows8 = SEQ_TILE // SUBLANES
    last8 = geom.n_tok // SUBLANES - 1
    tile = (lambda t: geom.mirror(t)) if mirror else (lambda t: t)
    return [pl.BlockSpec((SEQ_TILE, width), lambda t: (tile(t), 0)),
            pl.BlockSpec((SUBLANES, width), lambda t: (jnp.maximum(tile(t) * rows8 - 1, 0), 0)),
            pl.BlockSpec((SUBLANES, width), lambda t: (jnp.minimum((tile(t) + 1) * rows8, last8), 0))]


def _dn_kernel(xf_ref, xfp_ref, xfn_ref, xb_ref, xbp_ref, xbn_ref, gf_ref, gb_ref, s0_ref,
               cw_ref, alog_ref, bias_ref, of_ref, ob_ref, sfin_ref, s_scr, *, geom):
    t = pl.program_id(0)
    seq, pos, nt = geom.tile_pos(t)
    is_ctx = seq < geom.n_ctx_seq
    tt = SEQ_TILE

    @pl.when((pos == 0) & is_ctx)
    def _():
        s_scr[...] = jnp.zeros(s_scr.shape, F32)

    @pl.when((pos == 0) & jnp.logical_not(is_ctx))
    def _():
        s_scr[...] = s0_ref[0]

    row = lax.broadcasted_iota(jnp.int32, (tt, tt), 0)
    col = lax.broadcasted_iota(jnp.int32, (tt, tt), 1)
    xr = row ^ col
    lvl = jnp.where(xr >= 2, 1, 0)
    for kbit in range(2, tt.bit_length() - 1):
        lvl = lvl + jnp.where(xr >= (1 << kbit), 1, 0)
    head_l = lax.broadcasted_iota(jnp.int32, (tt, GROUP_W), 1) // HEAD_DIM
    blockdiag = (lax.broadcasted_iota(jnp.int32, (GROUP_W, GROUP_W), 0) // HEAD_DIM
                 == lax.broadcasted_iota(jnp.int32, (GROUP_W, GROUP_W), 1) // HEAD_DIM)
    eye = _ones_where(row == col, F32)
    gones = _ones_where(blockdiag, BF16)
    lane_g = lax.broadcasted_iota(jnp.int32, (tt, LANES), 1)
    not_first, not_last = pos > 0, pos < nt - 1

    dirs = ((xf_ref, xfp_ref, xfn_ref, gf_ref, of_ref, not_first, not_last),
            (xb_ref, xbp_ref, xbn_ref, gb_ref, ob_ref, not_last, not_first))
    a_lists, x_lists, pre_out = ([], []), ([], []), [None, None]

    def pre(d):
        x_ref, xp_ref, xn_ref, g_ref, o_ref, has_prev, has_next = dirs[d]
        x_prev = jnp.where(has_prev, xp_ref[SUBLANES - 1:SUBLANES, :], 0.0)
        x_next = jnp.where(has_next, xn_ref[0:1, :], 0.0)
        y = _silu(_conv3(x_ref[...], x_prev, x_next, cw_ref))
        yield
        q, k, v = y[:, :GROUP_W], y[:, GROUP_W:2 * GROUP_W], y[:, 2 * GROUP_W:]
        q = q * lax.rsqrt(_mm(q * q, gones) + EPS) * (HEAD_DIM ** -0.5)
        k = k * lax.rsqrt(_mm(k * k, gones) + EPS)
        yield

        s = g_ref[...]
        gate = -jnp.exp(alog_ref[...]) * _softplus(s + bias_ref[...])
        z = jnp.where(lane_g < SM_ALPHA, _sigmoid(s), gate)
        incl = (row >= col) if d == 0 else (row <= col)
        strict = (row > col) if d == 0 else (row < col)
        cum, cumt = _cumulative(z, _ones_where(incl, BF16))
        cum2, cumt2 = cum * LOG2_E, cumt * LOG2_E
        edge = tt - 1 if d == 0 else 0
        c_beta = SM_BETA + N_HEADS * d
        c_g = SM_ALPHA + N_HEADS * d
        yield

        beta_l = _head_lanes(head_l, lambda h: z[:, c_beta + h:c_beta + h + 1])
        cum_l = _head_lanes(head_l, lambda h: cum[:, c_g + h:c_g + h + 1])
        tot_l = _head_lanes(head_l[0:1, :], lambda h: cum[edge:edge + 1, c_g + h:c_g + h + 1])
        eg = jnp.exp(cum_l)
        kb = k * beta_l
        rhs = jnp.concatenate([v * beta_l, kb * eg], axis=1).astype(BF16)
        kbf = k.astype(BF16)
        yield
        qk_heads = []
        for h in range(N_HEADS):
            hm = head_l == h
            diff = cum2[:, c_g + h:c_g + h + 1] - cumt2[c_g + h:c_g + h + 1, :]
            decay = jnp.where(incl, jnp.exp2(diff), 0.0)
            a_lists[d].append(jnp.where(strict, _mm_nt(jnp.where(hm, kb, 0.0), kbf) * decay, 0.0))
            qk_heads.append((_mm_nt(jnp.where(hm, q, 0.0), kbf) * decay).astype(BF16))
            yield
        pre_out[d] = (rhs, qk_heads, (q * eg).astype(BF16), k * jnp.exp(tot_l - cum_l),
                      jnp.exp(tot_l), o_ref)

    def tail(d):
        rhs, qk_heads, qg, kd, e_tot, o_ref = pre_out[d]
        u_all = jnp.zeros((tt, GROUP_W), F32)
        w_all = jnp.zeros((tt, GROUP_W), F32)
        for h in range(N_HEADS):
            hm = head_l == h
            uw = jnp.dot(x_lists[d][h], rhs, preferred_element_type=F32)
            u_all = jnp.where(hm, uw[:, :GROUP_W], u_all)
            w_all = jnp.where(hm, uw[:, GROUP_W:], w_all)
            yield
        state = s_scr[d]
        sb = state.astype(BF16)
        v_new = u_all - _mm(w_all, sb)
        vb = v_new.astype(BF16)
        yield
        o = jnp.dot(qg, sb, preferred_element_type=F32)
        for h in range(N_HEADS):
            o = o + jnp.where(head_l == h, jnp.dot(qk_heads[h], vb, preferred_element_type=F32), 0.0)
            yield
        o_ref[...] = o
        s_scr[d] = state * e_tot + jnp.where(blockdiag, _mm_tn(kd, vb), 0.0)

    _interleave(pre(0), pre(1))
    x_all = []
    _interleave(_unit_tri_inverses(a_lists[0] + a_lists[1], lvl, eye, x_all))
    x_lists[0].extend(x_all[:N_HEADS])
    x_lists[1].extend(x_all[N_HEADS:])
    _interleave(tail(0), tail(1))

    @pl.when((pos == nt - 1) & is_ctx)
    def _():
        for d in range(2):
            for h in range(N_HEADS):
                sl = slice(h * HEAD_DIM, (h + 1) * HEAD_DIM)
                sfin_ref[0, d, h] = s_scr[d, sl, sl]


def _state_specs(geom, block, l):
    seq_of = lambda t: geom.tile_pos(t)[0]
    lat_seq = lambda t: (jnp.maximum(seq_of(t) - geom.n_ctx_seq, 0), l) + (0,) * len(block[0])
    ctx_seq = lambda t: (jnp.minimum(seq_of(t), geom.n_ctx_seq - 1),) + (0,) * len(block[1])
    return pl.BlockSpec((1, None) + block[0], lat_seq), pl.BlockSpec((1,) + block[1], ctx_seq)


def _deltanet(geom, qkv, small, s0, conv_w, alog_row, bias_row, l):
    tt = SEQ_TILE
    s0_spec, sfin_spec = _state_specs(geom, ((2, GROUP_W, GROUP_W), (2, N_HEADS, HEAD_DIM, HEAD_DIM)), l)
    in_specs = (_seq_tile_specs(geom, 3 * GROUP_W, False) + _seq_tile_specs(geom, 3 * GROUP_W, True)
                + [pl.BlockSpec((tt, LANES), lambda t: (t, 0)),
                   pl.BlockSpec((tt, LANES), lambda t: (geom.mirror(t), 0)),
                   s0_spec, _layer_spec(conv_w, l), _layer_spec(alog_row, l), _layer_spec(bias_row, l)])
    return pl.pallas_call(
        functools.partial(_dn_kernel, geom=geom),
        out_shape=[jax.ShapeDtypeStruct((geom.n_tok, GROUP_W), F32),
                   jax.ShapeDtypeStruct((geom.n_tok, GROUP_W), F32),
                   jax.ShapeDtypeStruct((geom.n_ctx_seq, 2, N_HEADS, HEAD_DIM, HEAD_DIM), F32)],
        grid=(geom.n_tiles,),
        in_specs=in_specs,
        out_specs=[pl.BlockSpec((tt, GROUP_W), lambda t: (t, 0)),
                   pl.BlockSpec((tt, GROUP_W), lambda t: (geom.mirror(t), 0)),
                   sfin_spec],
        scratch_shapes=[pltpu.VMEM((2, GROUP_W, GROUP_W), F32)],
        compiler_params=_cparams(("arbitrary",)),
        name="deltanet",
    )(qkv, qkv, qkv, qkv, qkv, qkv, small, small, s0, conv_w, alog_row, bias_row)


def _ssm_kernel(xf_ref, xfp_ref, xfn_ref, xb_ref, xbp_ref, xbn_ref, gf_ref, gb_ref, s0_ref,
                cw_ref, cb_ref, alog_ref, bias_ref, dvec_ref, yf_ref, yb_ref, sfin_ref, s_scr,
                *, geom):
    t = pl.program_id(0)
    seq, pos, nt = geom.tile_pos(t)
    is_ctx = seq < geom.n_ctx_seq
    tt = SEQ_TILE

    @pl.when((pos == 0) & is_ctx)
    def _():
        s_scr[...] = jnp.zeros(s_scr.shape, F32)

    @pl.when((pos == 0) & jnp.logical_not(is_ctx))
    def _():
        s_scr[...] = s0_ref[0]

    row = lax.broadcasted_iota(jnp.int32, (tt, tt), 0)
    col = lax.broadcasted_iota(jnp.int32, (tt, tt), 1)
    head_l = lax.broadcasted_iota(jnp.int32, (tt, GROUP_W), 1) // HEAD_DIM
    lane_g = lax.broadcasted_iota(jnp.int32, (tt, LANES), 1)
    group_l = lane_g // SSM_N
    state_head = lax.broadcasted_iota(jnp.int32, (GROUP_W, 2 * SSM_N), 0) // HEAD_DIM
    state_group = lax.broadcasted_iota(jnp.int32, (GROUP_W, 2 * SSM_N), 1) // SSM_N
    state_mask = state_head // 2 == state_group
    not_first, not_last = pos > 0, pos < nt - 1

    dirs = ((xf_ref, xfp_ref, xfn_ref, gf_ref, yf_ref, not_first, not_last),
            (xb_ref, xbp_ref, xbn_ref, gb_ref, yb_ref, not_last, not_first))
    for d, (x_ref, xp_ref, xn_ref, g_ref, y_ref, has_prev, has_next) in enumerate(dirs):
        x_prev = jnp.where(has_prev, xp_ref[SUBLANES - 1:SUBLANES, :], 0.0)
        x_next = jnp.where(has_next, xn_ref[0:1, :], 0.0)
        y = _silu(_conv3(x_ref[...], x_prev, x_next, cw_ref) + cb_ref[...])
        xs, bm, cm = y[:, :GROUP_W], y[:, GROUP_W:GROUP_W + 2 * SSM_N], y[:, GROUP_W + 2 * SSM_N:]

        dt = _softplus(g_ref[...] + bias_ref[...])
        a = -jnp.exp(alog_ref[...]) * dt
        incl = (row >= col) if d == 0 else (row <= col)
        cum, cumt = _cumulative(a, _ones_where(incl, BF16))
        cum2, cumt2 = cum * LOG2_E, cumt * LOG2_E
        edge = tt - 1 if d == 0 else 0
        c0 = SM_DT + N_HEADS * d

        dt_l = _head_lanes(head_l, lambda h: dt[:, c0 + h:c0 + h + 1])
        cum_l = _head_lanes(head_l, lambda h: cum[:, c0 + h:c0 + h + 1])
        tot_l = _head_lanes(head_l[0:1, :], lambda h: cum[edge:edge + 1, c0 + h:c0 + h + 1])
        xdt = xs * dt_l
        cb_scores = [_mm_nt(jnp.where(group_l == g, cm, 0.0), bm) for g in range(2)]
        out = jnp.zeros((tt, GROUP_W), F32)
        for h in range(N_HEADS):
            diff = cum2[:, c0 + h:c0 + h + 1] - cumt2[c0 + h:c0 + h + 1, :]
            lmat = jnp.where(incl, jnp.exp2(diff), 0.0)
            out = jnp.where(head_l == h, _mm(cb_scores[h // 2] * lmat, xdt), out)
        state = s_scr[d]
        out = out + _mm_nt(cm, state) * jnp.exp(cum_l)
        if d == 0:
            out = out + dvec_ref[...] * xs
        y_ref[...] = out
        tot_rows = _head_lanes(state_head, lambda h: cum[edge:edge + 1, c0 + h:c0 + h + 1])
        s_scr[d] = (state * jnp.exp(tot_rows)
                    + jnp.where(state_mask, _mm_tn(xdt * jnp.exp(tot_l - cum_l), bm), 0.0))

    @pl.when((pos == nt - 1) & is_ctx)
    def _():
        for d in range(2):
            for h in range(N_HEADS):
                g = h // 2
                sfin_ref[0, d, h] = s_scr[d, h * HEAD_DIM:(h + 1) * HEAD_DIM, g * SSM_N:(g + 1) * SSM_N]


def _ssd(geom, xbc, small, s0, conv_w, conv_b, alog_row, bias_row, dvec, l):
    tt = SEQ_TILE
    wx = GROUP_W + 4 * SSM_N
    s0_spec, sfin_spec = _state_specs(geom, ((2, GROUP_W, 2 * SSM_N), (2, N_HEADS, HEAD_DIM, SSM_N)), l)
    in_specs = (_seq_tile_specs(geom, wx, False) + _seq_tile_specs(geom, wx, True)
                + [pl.BlockSpec((tt, LANES), lambda t: (t, 0)),
                   pl.BlockSpec((tt, LANES), lambda t: (geom.mirror(t), 0)),
                   s0_spec, _layer_spec(conv_w, l), _layer_spec(conv_b, l), _layer_spec(alog_row, l),
                   _layer_spec(bias_row, l), _layer_spec(dvec, l)])
    return pl.pallas_call(
        functools.partial(_ssm_kernel, geom=geom),
        out_shape=[jax.ShapeDtypeStruct((geom.n_tok, GROUP_W), F32),
                   jax.ShapeDtypeStruct((geom.n_tok, GROUP_W), F32),
                   jax.ShapeDtypeStruct((geom.n_ctx_seq, 2, N_HEADS, HEAD_DIM, SSM_N), F32)],
        grid=(geom.n_tiles,),
        in_specs=in_specs,
        out_specs=[pl.BlockSpec((tt, GROUP_W), lambda t: (t, 0)),
                   pl.BlockSpec((tt, GROUP_W), lambda t: (geom.mirror(t), 0)),
                   sfin_spec],
        scratch_shapes=[pltpu.VMEM((2, GROUP_W, 2 * SSM_N), F32)],
        compiler_params=_cparams(("arbitrary",)),
        name="ssd",
    )(xbc, xbc, xbc, xbc, xbc, xbc, small, small, s0, conv_w, conv_b, alog_row, bias_row, dvec)


def _rope_slab(x, cos, sin_a, sin_b, half):
    w = x.shape[-1]
    return x * cos + pltpu.roll(x, w - half, 1) * sin_a + pltpu.roll(x, half, 1) * sin_b


def _mla_proj_kernel(ql_ref, kvl_ref, sm_ref, cos_ref, sa_ref, sb_ref, qnw_ref, kvnw_ref,
                     wuq_ref, wk_ref, wv_ref, qh_ref, kh_ref, vh_ref, ckv_ref, *, geom):
    cos, sa, sb = cos_ref[...], sa_ref[...], sb_ref[...]
    half = MLA_ROPE // 2
    qp = _mm(_rms(ql_ref[...], qnw_ref[...]), wuq_ref[...]) * (MLA_SCALE * LOG2_E)
    ckv = _rms(kvl_ref[...], kvnw_ref[...])

    @pl.when(geom.is_ctx_tile(pl.program_id(0)))
    def _():
        ckv_ref[...] = ckv

    lane = lax.broadcasted_iota(jnp.int32, cos.shape, 1)
    is_pe = (lane >= MLA_NOPE) & (lane < MLA_NOPE + MLA_ROPE)
    kpe = jnp.where(is_pe, _rope_slab(sm_ref[...], cos, sa, sb, half), 0.0)
    kp = _mm(ckv, wk_ref[...])
    for h in range(N_HEADS):
        sl = slice(h * LANES, (h + 1) * LANES)
        qh_ref[:, sl] = _rope_slab(qp[:, sl], cos, sa, sb, half).astype(BF16)
        kh_ref[:, sl] = (kp[:, sl] + kpe).astype(BF16)
    vh_ref[...] = _mm(ckv, wv_ref[...]).astype(BF16)


def _mla_proj(geom, q_lat, kv_lat, small, cos, sa, sb, qnw, kvnw, wuq, wk, wv, l):
    tm = ROW_TILE
    tok = lambda w: pl.BlockSpec((tm, w), lambda i: (i, 0))
    full = lambda a: _layer_spec(a, l)

    def tab_block(i):
        r = i * tm
        return jnp.where(r >= geom.n_ctx, 1 + ((r - geom.n_ctx) % geom.lat_len) // tm, 0), 0

    tab = pl.BlockSpec((tm, LANES), tab_block)
    return pl.pallas_call(
        functools.partial(_mla_proj_kernel, geom=geom),
        out_shape=[jax.ShapeDtypeStruct((geom.n_tok, N_HEADS * LANES), BF16),
                   jax.ShapeDtypeStruct((geom.n_tok, N_HEADS * LANES), BF16),
                   jax.ShapeDtypeStruct((geom.n_tok, GROUP_W), BF16),
                   jax.ShapeDtypeStruct((geom.n_ctx, MLA_KV_LORA), F32)],
        grid=(geom.n_tok // tm,),
        in_specs=[tok(MLA_Q_LORA), tok(MLA_KV_LORA), tok(LANES), tab, tab, tab,
                  full(qnw), full(kvnw), full(wuq), full(wk), full(wv)],
        out_specs=[tok(N_HEADS * LANES), tok(N_HEADS * LANES), tok(GROUP_W),
                   geom.split_specs(MLA_KV_LORA)[0]],
        compiler_params=_cparams(("arbitrary",)),
        name="mla_proj",
    )(q_lat, kv_lat, small, cos, sa, sb, qnw, kvnw, wuq, wk, wv)


def _mla_attn_kernel(*refs, has_cache):
    if has_cache:
        q_ref, k_ref, v_ref, ckv_ref, kpe_ref, wk_ref, wv_ref, o_ref = refs
        ckv_c = ckv_ref[0]
        v_c = _mm(ckv_c, wv_ref[...])
    else:
        q_ref, k_ref, v_ref, o_ref = refs
    heads = (slice(0, LANES), slice(LANES, 2 * LANES))
    n_keys = k_ref.shape[0]
    chunk = n_keys // KEY_CHUNKS if n_keys % (KEY_CHUNKS * LANES) == 0 else n_keys
    def own_lanes(v, j):
        lane = lax.broadcasted_iota(jnp.int32, v.shape, 1)
        return jnp.where((lane >= HEAD_DIM) == (j == 1), v, 1.0).astype(BF16)

    qs = [q_ref[:, sl] for sl in heads]
    if has_cache:
        scores = [_mm_nt(q, _mm(ckv_c, wk_ref[:, sl]) + kpe_ref[0]) for q, sl in zip(qs, heads)]
        ms = [jnp.max(s, axis=-1, keepdims=True) for s in scores]
        accs = [_mm(jnp.exp2(s - m), own_lanes(v_c, j)) for j, (s, m) in enumerate(zip(scores, ms))]
    for c in range(n_keys // chunk):
        rows = slice(c * chunk, (c + 1) * chunk)
        scores = [_mm_nt(q, k_ref[rows, sl]) for q, sl in zip(qs, heads)]
        v = v_ref[rows, :]
        if c == 0 and not has_cache:
            ms = [jnp.max(s, axis=-1, keepdims=True) for s in scores]
            accs = [_mm(jnp.exp2(s - m), own_lanes(v, j)) for j, (s, m) in enumerate(zip(scores, ms))]
            continue
        for j, s in enumerate(scores):
            m_new = jnp.maximum(ms[j], jnp.max(s, axis=-1, keepdims=True))
            accs[j] = accs[j] * jnp.exp2(ms[j] - m_new) + _mm(jnp.exp2(s - m_new), own_lanes(v, j))
            ms[j] = m_new
    outs = [acc / pltpu.roll(acc, HEAD_DIM, 1) for acc in accs]
    lane = lax.broadcasted_iota(jnp.int32, outs[0].shape, 1)
    o_ref[...] = jnp.where(lane < HEAD_DIM, outs[0], outs[1])


def _mla_attn(qh, kh, vh, n_seq, seq_len, tok0, cache=None, l=0):
    tq = min(ROW_TILE, seq_len)
    nq = seq_len // tq
    q0, k0 = tok0 // tq, tok0 // seq_len
    assert tok0 % seq_len == 0
    in_specs = [pl.BlockSpec((tq, 2 * LANES), lambda b, hp, i: (q0 + b * nq + i, hp)),
                pl.BlockSpec((seq_len, 2 * LANES), lambda b, hp, i: (k0 + b, hp)),
                pl.BlockSpec((seq_len, LANES), lambda b, hp, i: (k0 + b, hp))]
    args = [qh, kh, vh]
    if cache is not None:
        ckv_c, kpe_c, wk, wv = cache
        past = ckv_c.shape[2]
        in_specs += [pl.BlockSpec((1, None, past, MLA_KV_LORA), lambda b, hp, i: (b, l, 0, 0)),
                     pl.BlockSpec((1, None, past, LANES), lambda b, hp, i: (b, l, 0, 0)),
                     pl.BlockSpec((None, MLA_KV_LORA, 2 * LANES), lambda b, hp, i: (l, 0, hp)),
                     pl.BlockSpec((None, MLA_KV_LORA, LANES), lambda b, hp, i: (l, 0, hp))]
        args += [ckv_c, kpe_c, wk, wv]
    return pl.pallas_call(
        functools.partial(_mla_attn_kernel, has_cache=cache is not None),
        out_shape=jax.ShapeDtypeStruct((n_seq * seq_len, GROUP_W), F32),
        grid=(n_seq, 2, nq),
        in_specs=in_specs,
        out_specs=pl.BlockSpec((tq, LANES), lambda b, hp, i: (b * nq + i, hp)),
        compiler_params=_cparams(("parallel", "parallel", "arbitrary")),
        name="mla_attn_lat" if cache is not None else "mla_attn_ctx",
    )(*args)


def _swa_core(q, k_all, v_all, et_ref, sink_ref, valid_t):
    et = et_ref[...]
    kx = _mm_nt(k_all, et).astype(BF16)
    vxt = _mm_nt(et, v_all).astype(BF16)
    head_l = lax.broadcasted_iota(jnp.int32, q.shape, 1) // HEAD_DIM
    heads = range(N_HEADS)
    st = [_mm_nt(kx, jnp.where(head_l == h, q, 0.0)) for h in heads]
    if valid_t is not None:
        st = [jnp.where(valid_t, x, NEG_INF) for x in st]
    sink = [sink_ref[:, h * HEAD_DIM:h * HEAD_DIM + 1] * LOG2_E for h in heads]
    m = [jnp.maximum(jnp.max(st[h], axis=0, keepdims=True), sink[h]) for h in heads]
    pt = [jnp.exp2(st[h] - m[h]) for h in heads]
    den = [jnp.sum(pt[h], axis=0, keepdims=True) + jnp.exp2(sink[h] - m[h]) for h in heads]
    ot = [jnp.dot(vxt, pt[h].astype(BF16), preferred_element_type=F32) / den[h] for h in heads]
    row_head = lax.broadcasted_iota(jnp.int32, ot[0].shape, 0) // HEAD_DIM
    out_t = ot[N_HEADS - 1]
    for h in range(N_HEADS - 2, -1, -1):
        out_t = jnp.where(row_head == h, ot[h], out_t)
    return out_t.T


def _swa_ctx_kernel(x_ref, e_ref, sink_ref, o_ref):
    x = x_ref[...]
    q = x[:, :GROUP_W] * (SWA_SCALE * LOG2_E)
    k, v = x[:, GROUP_W:GROUP_W + LANES], x[:, GROUP_W + LANES:]
    o_ref[...] = _swa_core(q, k, v, e_ref, sink_ref, None)


def _swa_ctx(geom, swa, e_mat, sink_l, l):
    t = geom.ctx_len
    return pl.pallas_call(
        _swa_ctx_kernel,
        out_shape=jax.ShapeDtypeStruct((geom.n_ctx, GROUP_W), F32),
        grid=(geom.n_ctx_seq,),
        in_specs=[pl.BlockSpec((t, 2 * GROUP_W), lambda b: (b, 0)),
                  pl.BlockSpec(e_mat.shape, lambda b: (0, 0)),
                  _layer_spec(sink_l, l)],
        out_specs=pl.BlockSpec((t, GROUP_W), lambda b: (b, 0)),
        compiler_params=_cparams(("parallel",)),
        name="swa_ctx",
    )(swa, e_mat, sink_l)


def _swa_lat_kernel(xc_ref, xp_ref, xn_ref, cc_ref, ac_ref, bc_ref, cp_ref, ap_ref, bp_ref,
                    cn_ref, an_ref, bn_ref, kc_ref, vc_ref, e_ref, sink_ref, o_ref, *, n_tiles):
    i = pl.program_id(1)
    win = SWA_BLOCK
    tq = xc_ref.shape[0]
    half = HEAD_DIM // 2

    def rope(x, c_ref, a_ref, b_ref):
        reps = x.shape[1] // LANES
        wide = lambda t_ref: jnp.concatenate([t_ref[...]] * reps, axis=1) if reps > 1 else t_ref[...]
        return _rope_slab(x, wide(c_ref), wide(a_ref), wide(b_ref), half)

    ksl, vsl = slice(GROUP_W, GROUP_W + LANES), slice(GROUP_W + LANES, 2 * GROUP_W)
    q = rope(xc_ref[:, :GROUP_W], cc_ref, ac_ref, bc_ref) * (SWA_SCALE * LOG2_E)
    k_all = jnp.concatenate([rope(xp_ref[:, ksl], cp_ref, ap_ref, bp_ref),
                             rope(xc_ref[:, ksl], cc_ref, ac_ref, bc_ref),
                             rope(xn_ref[:, ksl], cn_ref, an_ref, bn_ref),
                             kc_ref[0]], axis=0)
    v_all = jnp.concatenate([xp_ref[:, vsl], xc_ref[:, vsl], xn_ref[:, vsl], vc_ref[0]], axis=0)
    nk = k_all.shape[0]
    n_local = tq + 2 * win
    c = lax.broadcasted_iota(jnp.int32, (nk, tq), 0)
    r = lax.broadcasted_iota(jnp.int32, (nk, tq), 1)
    in_seq = ((c >= win) | (i > 0)) & ((c < win + tq) | (i < n_tiles - 1))
    valid_t = ((c >= r) & (c <= r + 2 * win) & in_seq) | (c >= n_local)
    o_ref[...] = _swa_core(q, k_all, v_all, e_ref, sink_ref, valid_t)


def _swa_lat(geom, swa, cos, sa, sb, k_cache, v_cache, e_mat, sink_l, l):
    win = SWA_BLOCK
    tq = 2 * win
    n_tiles = geom.lat_len // tq
    nblk = geom.lat_len // win
    t0, b0 = geom.n_ctx // tq, geom.n_ctx // win
    past = k_cache.shape[2]
    prv = lambda i: jnp.maximum(2 * i - 1, 0)
    nxt = lambda i: jnp.minimum(2 * i + 2, nblk - 1)
    tok = lambda w: [pl.BlockSpec((tq, w), lambda b, i: (t0 + b * n_tiles + i, 0)),
                     pl.BlockSpec((win, w), lambda b, i: (b0 + b * nblk + prv(i), 0)),
                     pl.BlockSpec((win, w), lambda b, i: (b0 + b * nblk + nxt(i), 0))]
    tab = lambda f, rows: [pl.BlockSpec((rows, LANES), lambda b, i: (f(i), 0))] * 3
    return pl.pallas_call(
        functools.partial(_swa_lat_kernel, n_tiles=n_tiles),
        out_shape=jax.ShapeDtypeStruct((geom.n_lat, GROUP_W), F32),
        grid=(geom.n_lat_seq, n_tiles),
        in_specs=tok(2 * GROUP_W) + tab(lambda i: i, tq) + tab(prv, win) + tab(nxt, win)
        + [pl.BlockSpec((1, None, past, LANES), lambda b, i: (b, l, 0, 0)),
           pl.BlockSpec((1, None, past, LANES), lambda b, i: (b, l, 0, 0)),
           pl.BlockSpec(e_mat.shape, lambda b, i: (0, 0)),
           _layer_spec(sink_l, l)],
        out_specs=pl.BlockSpec((tq, GROUP_W), lambda b, i: (b * n_tiles + i, 0)),
        compiler_params=_cparams(("parallel", "arbitrary")),
        name="swa_lat",
    )(swa, swa, swa, cos, sa, sb, cos, sa, sb, cos, sa, sb, k_cache, v_cache, e_mat, sink_l)


def _mix_ffn_kernel(*refs, geom, n_x, final):
    x_refs, refs = refs[:n_x], refs[n_x:]
    (ada_ref, dof_ref, dob_ref, dz_ref, syf_ref, syb_ref, sz_ref, omc_ref, oml_ref, osc_ref, osl_ref,
     dnw_ref, snw_ref, wo_ref, nw_ref, wgu_ref, wd_ref, fw_ref), out_refs = refs[:18], refs[18:]
    is_ctx = geom.is_ctx_tile(pl.program_id(0))
    x = x_refs[0][...] if n_x == 1 else _read_split(*x_refs, is_ctx)

    row = lax.broadcasted_iota(jnp.int32, (GROUP_W, GROUP_W), 0)
    col = lax.broadcasted_iota(jnp.int32, (GROUP_W, GROUP_W), 1)
    gones = _ones_where((row // HEAD_DIM) == (col // HEAD_DIM), BF16)
    o = dof_ref[...] + dob_ref[...]
    ms = _mm(o * o, gones) * (1.0 / HEAD_DIM)
    dn = o * lax.rsqrt(ms + EPS) * dnw_ref[...] * _silu(dz_ref[...])
    acc = _mm(dn, wo_ref[0:GROUP_W, :])
    acc = acc + _mm(_read_split(omc_ref, oml_ref, is_ctx), wo_ref[GROUP_W:2 * GROUP_W, :])
    y = (syf_ref[...] + syb_ref[...]) * _silu(sz_ref[...])
    for g in range(2):
        sl = slice(g * LANES, (g + 1) * LANES)
        acc = acc + _mm(_rms(y[:, sl], snw_ref[:, sl]),
                        wo_ref[2 * GROUP_W + g * LANES:2 * GROUP_W + (g + 1) * LANES, :])
    acc = acc + _mm(_read_split(osc_ref, osl_ref, is_ctx), wo_ref[3 * GROUP_W:, :])
    x = x + ada_ref[0, 2:3, :] * acc

    h = (_rms(x, nw_ref[...]) * (1.0 + ada_ref[0, 4:5, :]) + ada_ref[0, 3:4, :]).astype(BF16)
    acts = []
    for c in range(FF_DIM // FF_CHUNK):
        g = jnp.dot(h, wgu_ref[:, c * FF_CHUNK:(c + 1) * FF_CHUNK], preferred_element_type=F32)
        u = jnp.dot(h, wgu_ref[:, FF_DIM + c * FF_CHUNK:FF_DIM + (c + 1) * FF_CHUNK],
                    preferred_element_type=F32)
        acts.append((_silu(g) * u).astype(BF16))
    acc = jnp.dot(jnp.concatenate(acts, axis=1), wd_ref[...], preferred_element_type=F32)
    y = x + ada_ref[0, 5:6, :] * acc
    if not final:
        out_refs[0][...] = y
        return
    y = _rms(y, fw_ref[...])

    @pl.when(is_ctx)
    def _():
        out_refs[0][...] = y

    @pl.when(jnp.logical_not(is_ctx))
    def _():
        out_refs[1][...] = y


def _mix_ffn(geom, xs, ada, parts, dnw, snw, w_out, norm_w, wgu, w_down, final_w, l, final):
    tm = ROW_TILE
    d = xs[0].shape[1]
    tok = lambda w: pl.BlockSpec((tm, w), lambda i: (i, 0))
    resident = lambda a: _layer_spec(a, l, pipeline_mode=pl.Buffered(1))
    x_specs = [tok(d)] if len(xs) == 1 else geom.split_specs(d)
    if final:
        out_shape = [jax.ShapeDtypeStruct((geom.n_ctx, d), F32), jax.ShapeDtypeStruct((geom.n_lat, d), F32)]
        out_specs = geom.split_specs(d)
    else:
        out_shape = [jax.ShapeDtypeStruct((geom.n_tok, d), F32)]
        out_specs = [tok(d)]
    return pl.pallas_call(
        functools.partial(_mix_ffn_kernel, geom=geom, n_x=len(xs), final=final),
        out_shape=out_shape,
        grid=(geom.n_tok // tm,),
        in_specs=x_specs + [_ada_spec(geom, ada, l)]
        + [tok(GROUP_W)] * 6 + geom.split_specs(GROUP_W) + geom.split_specs(GROUP_W)
        + [resident(dnw), resident(snw), resident(w_out), resident(norm_w), resident(wgu),
           resident(w_down), pl.BlockSpec(final_w.shape, lambda i: (0, 0))],
        out_specs=out_specs,
        compiler_params=_cparams(("arbitrary",)),
        name="mix_ffn",
    )(*xs, ada, *parts, dnw, snw, w_out, norm_w, wgu, w_down, final_w)


def _w_in_layout(w):
    dn, mla, ssm, swa = 0, 1040, 1456, 2232
    zeros = lambda n: jnp.zeros(w.shape[:-1] + (n,), w.dtype)
    small = jnp.concatenate([
        w[..., dn + 1024:dn + 1040],
        w[..., ssm + 768:ssm + 776],
        zeros(SM_KPE - 24),
        w[..., mla + 384:mla + 416],
        zeros(LANES - SM_KPE - MLA_ROPE)], axis=-1)
    return jnp.concatenate([
        w[..., dn:dn + 768], w[..., dn + 768:dn + 1024],
        w[..., mla:mla + 256], w[..., mla + 256:mla + 384],
        w[..., ssm:ssm + 256], w[..., ssm + 256:ssm + 768],
        w[..., swa:swa + 512], small], axis=-1).astype(BF16)


def _gate_rows(dn_vec, ssm_vec):
    depth = dn_vec.shape[0]
    rows = jnp.zeros((depth, 1, LANES), F32)
    rows = rows.at[:, 0, SM_ALPHA:SM_ALPHA + 8].set(dn_vec.reshape(depth, 8))
    return rows.at[:, 0, SM_DT:SM_DT + 8].set(ssm_vec.reshape(depth, 8))


def _axial_angles(rows, rot_dim):
    row_ids = jnp.broadcast_to(jnp.arange(rows)[:, None], (rows, GRID_W)).reshape(-1).astype(F32)
    col_ids = jnp.broadcast_to(jnp.arange(GRID_W)[None, :], (rows, GRID_W)).reshape(-1).astype(F32)
    n_freq = rot_dim // 4
    inv_freq = ROPE_THETA ** (-jnp.arange(n_freq, dtype=F32) / n_freq)
    return jnp.concatenate([row_ids[:, None] * inv_freq, col_ids[:, None] * inv_freq], axis=-1)


def _rope_tables(ang, lane0, reps, n_ident):
    n, half = ang.shape
    cos, sin = jnp.cos(ang), jnp.sin(ang)
    zeros = jnp.zeros_like(sin)
    period = LANES // reps

    def table(first, second, fill):
        one = jnp.concatenate([jnp.full((n, lane0), fill, F32), first, second,
                               jnp.full((n, period - lane0 - 2 * half), fill, F32)], axis=1)
        tab = jnp.concatenate([one] * reps, axis=1)
        ident = jnp.full((n_ident, LANES), fill, F32)
        return jnp.concatenate([ident, tab], axis=0)

    return table(cos, cos, 1.0), table(-sin, zeros, 0.0), table(zeros, sin, 0.0)


def kernel(x_prompt, x_sample, c, state_dn, cache_mla_ckv, cache_mla_kpe, state_ssm, cache_swa_k,
           cache_swa_v, c_ctx, norm1_w, norm2_w, w_ada, b_ada, w_in, w_out, dn_conv_w, dn_a_log,
           dn_dt_bias, dn_norm_w, mla_q_norm_w, mla_w_uq, mla_kv_norm_w, mla_w_ukv, ssm_conv_w,
           ssm_conv_b, ssm_a_log, ssm_dt_bias, ssm_d, ssm_norm_w, swa_sinks, w_gate_up, w_down,
           final_norm_w):
    batch, seq, d = x_prompt.shape
    dec_batch, dec_seq, _ = x_sample.shape
    depth = w_in.shape[0]
    geom = _Geom(batch, seq, dec_batch, dec_seq)
    n_ctx = geom.n_ctx

    xs = (x_prompt.reshape(n_ctx, d), x_sample.reshape(geom.n_lat, d))
    n_mod = -(-(1 + dec_batch) // SUBLANES) * SUBLANES
    cc = jnp.concatenate([c_ctx[None], c, jnp.zeros((n_mod - 1 - dec_batch, d), F32)], axis=0)
    ada = _ada(cc, w_ada, b_ada).reshape(depth, n_mod, 6, d)

    mla_tabs = _rope_tables(_axial_angles(dec_seq // GRID_W, MLA_ROPE), MLA_NOPE, 1, ROW_TILE)
    swa_tabs = _rope_tables(_axial_angles(dec_seq // GRID_W, HEAD_DIM), 0, 2, 0)

    lane = jnp.arange(GROUP_W)
    e_mat = ((lane // LANES) * HEAD_DIM + lane % HEAD_DIM)[:, None] == jnp.arange(LANES)[None, :]
    e_mat = e_mat.astype(BF16)
    eye_h = jnp.eye(N_HEADS, dtype=F32)
    grp_h = (jnp.arange(N_HEADS)[:, None] // 2 == jnp.arange(2)[None, :]).astype(F32)
    row = lambda a: a.reshape(depth, 1, -1)
    w_pad = _w_in_layout(w_in)
    alog_rows = _gate_rows(dn_a_log, ssm_a_log)
    bias_rows = _gate_rows(dn_dt_bias, ssm_dt_bias)
    s0_dn = state_dn[:, :, :, :, :, None, :] * eye_h[None, None, None, :, None, :, None]
    s0_dn = s0_dn.reshape(dec_batch, depth, 2, GROUP_W, GROUP_W)
    s0_ssm = state_ssm[:, :, :, :, :, None, :] * grp_h[None, None, None, :, None, :, None]
    s0_ssm = s0_ssm.reshape(dec_batch, depth, 2, GROUP_W, 2 * SSM_N)
    ssm_dvec = row(jnp.repeat(ssm_d, HEAD_DIM, axis=-1))
    uq = mla_w_uq.reshape(depth, MLA_Q_LORA, N_HEADS, MLA_NOPE + MLA_ROPE)
    wuq = jnp.pad(uq, ((0, 0), (0, 0), (0, 0), (0, LANES - MLA_NOPE - MLA_ROPE)))
    wuq = wuq.reshape(depth, MLA_Q_LORA, N_HEADS * LANES).astype(BF16)
    ukv = mla_w_ukv.reshape(depth, MLA_KV_LORA, N_HEADS, MLA_NOPE + HEAD_DIM)
    wk = jnp.pad(ukv[..., :MLA_NOPE], ((0, 0), (0, 0), (0, 0), (0, LANES - MLA_NOPE)))
    wk = wk.reshape(depth, MLA_KV_LORA, N_HEADS * LANES).astype(BF16)
    wv = ukv[..., MLA_NOPE:].reshape(depth, MLA_KV_LORA, GROUP_W).astype(BF16)
    kpe_c = jnp.pad(cache_mla_kpe, ((0, 0), (0, 0), (0, 0), (MLA_NOPE, LANES - MLA_NOPE - MLA_ROPE)))
    past = cache_swa_k.shape[2]
    swa_kc = cache_swa_k.reshape(dec_batch, depth, past, LANES)
    swa_vc = cache_swa_v.reshape(dec_batch, depth, past, LANES)
    sinks = row(jnp.repeat(swa_sinks, HEAD_DIM, axis=-1))
    dnw = row(jnp.tile(dn_norm_w, (1, N_HEADS)))
    w_out_b, wgu_b, w_down_b = w_out.astype(BF16), w_gate_up.astype(BF16), w_down.astype(BF16)

    st_dn, st_ckv, st_kpe, st_ssm, st_k, st_v = [], [], [], [], [], []
    for l in range(depth):
        segs = _inproj(geom, xs, ada, row(norm1_w), w_pad, l)
        dn_qkv, dn_z, mla_q, mla_kv, ssm_z, ssm_xbc, swa, small = segs

        dn_of, dn_ob, dn_fin = _deltanet(geom, dn_qkv, small, s0_dn, dn_conv_w, alog_rows, bias_rows, l)
        ssm_yf, ssm_yb, ssm_fin = _ssd(geom, ssm_xbc, small, s0_ssm, ssm_conv_w, row(ssm_conv_b),
                                       alog_rows, bias_rows, ssm_dvec, l)
        qh, kh, vh, ckv = _mla_proj(geom, mla_q, mla_kv, small, *mla_tabs, row(mla_q_norm_w),
                                    row(mla_kv_norm_w), wuq, wk, wv, l)
        o_mla_ctx = _mla_attn(qh, kh, vh, batch, seq, 0)
        o_mla_lat = _mla_attn(qh, kh, vh, dec_batch, dec_seq, n_ctx,
                              cache=(cache_mla_ckv, kpe_c, wk, wv), l=l)
        o_swa_ctx = _swa_ctx(geom, swa, e_mat, sinks, l)
        o_swa_lat = _swa_lat(geom, swa, *swa_tabs, swa_kc, swa_vc, e_mat, sinks, l)

        st_dn.append(dn_fin)
        st_ssm.append(ssm_fin)
        st_ckv.append(ckv.reshape(batch, seq, MLA_KV_LORA))
        st_kpe.append(small[:n_ctx, SM_KPE:SM_KPE + MLA_ROPE].reshape(batch, seq, MLA_ROPE))
        st_k.append(swa[:n_ctx, GROUP_W:GROUP_W + LANES].reshape(batch, seq, 2, HEAD_DIM))
        st_v.append(swa[:n_ctx, GROUP_W + LANES:].reshape(batch, seq, 2, HEAD_DIM))

        parts = (dn_of, dn_ob, dn_z, ssm_yf, ssm_yb, ssm_z, o_mla_ctx, o_mla_lat, o_swa_ctx, o_swa_lat)
        xs = _mix_ffn(geom, xs, ada, parts, dnw, row(ssm_norm_w), w_out_b, row(norm2_w), wgu_b, w_down_b,
                      final_norm_w.reshape(1, d), l, l == depth - 1)

    return (xs[0].reshape(batch, seq, d), xs[1].reshape(dec_batch, dec_seq, d),
            jnp.stack(st_dn, axis=1), jnp.stack(st_ckv, axis=1), jnp.stack(st_kpe, axis=1),
            jnp.stack(st_ssm, axis=1), jnp.stack(st_k, axis=1), jnp.stack(st_v, axis=1))
```

```python
import functools

import jax
import jax.numpy as jnp
from jax import lax
from jax.experimental import pallas as pl
from jax.experimental.pallas import tpu as pltpu

F32 = jnp.float32
BF16 = jnp.bfloat16

D_MODEL = 1024
GRID_W = 64
HEAD_DIM = 64
GROUP_W = 256
EPS = 1e-6
ROPE_THETA = 10000.0
NEG_INF = -1e30
N_HEADS = 4
MLA_NOPE = 64
MLA_ROPE = 32
MLA_Q_LORA = 256
MLA_KV_LORA = 128
MLA_SCALE = (MLA_NOPE + MLA_ROPE) ** -0.5
SSM_N = 64
SWA_SCALE = HEAD_DIM ** -0.5
LOG2_E = 1.4426950408889634
SWA_BLOCK = 128
FF_DIM = 2816
FF_CHUNK = 256
KEY_CHUNKS = 4

LANES = 128
SUBLANES = 8
SEQ_TILE = 256
ROW_TILE = 512
VMEM_LIMIT = 56 * 1024 * 1024

SEG_WIDTHS = (768, 256, 256, 128, 256, 512, 512, 128)
IN_PAD = sum(SEG_WIDTHS)
SM_BETA, SM_ALPHA, SM_DT, SM_KPE = 0, 8, 16, 64


def _sigmoid(x):
    return 1.0 / (1.0 + jnp.exp(-x))


def _silu(x):
    half = 0.5 * x
    return half + half * jnp.tanh(half)


def _softplus(x):
    return jnp.maximum(x, 0.0) + jnp.log1p(jnp.exp(-jnp.abs(x)))


def _mm(a, b):
    return jnp.dot(a.astype(BF16), b.astype(BF16), preferred_element_type=F32)


def _mm_nt(a, b):
    return lax.dot_general(a.astype(BF16), b.astype(BF16), (((1,), (1,)), ((), ())),
                           preferred_element_type=F32)


def _mm_tn(a, b):
    return lax.dot_general(a.astype(BF16), b.astype(BF16), (((0,), (0,)), ((), ())),
                           preferred_element_type=F32)


def _split(a, parts):
    out = []
    for _ in range(parts):
        hi = a.astype(BF16)
        out.append(hi)
        a = a - hi.astype(F32)
    return out


def _mm_split_lhs(a, b_exact, parts, nt=False):
    dims = (((1,), (1,)), ((), ())) if nt else (((1,), (0,)), ((), ()))
    acc = None
    for piece in _split(a, parts):
        r = lax.dot_general(piece, b_exact, dims, preferred_element_type=F32)
        acc = r if acc is None else acc + r
    return acc


def _mm_split_rhs(a_exact, b, parts):
    acc = None
    for piece in _split(b, parts):
        r = jnp.dot(a_exact, piece, preferred_element_type=F32)
        acc = r if acc is None else acc + r
    return acc


def _ones_where(mask, dtype):
    return jnp.where(mask, 1.0, 0.0).astype(dtype)


def _rms(x, w):
    return x * lax.rsqrt(jnp.mean(x * x, axis=-1, keepdims=True) + EPS) * w


def _cparams(sem):
    return pltpu.CompilerParams(dimension_semantics=sem, vmem_limit_bytes=VMEM_LIMIT)


def _ada_kernel(c_ref, w_ref, b_ref, o_ref):
    o_ref[0] = _mm(_silu(c_ref[...]), w_ref[0]) + b_ref[0]


def _ada(cc, w_ada, b_ada):
    depth, d, n = w_ada.shape
    tn = 1536
    return pl.pallas_call(
        _ada_kernel,
        out_shape=jax.ShapeDtypeStruct((depth, cc.shape[0], n), F32),
        grid=(depth, n // tn),
        in_specs=[pl.BlockSpec(cc.shape, lambda l, j: (0, 0)),
                  pl.BlockSpec((1, d, tn), lambda l, j: (l, 0, j)),
                  pl.BlockSpec((1, 1, tn), lambda l, j: (l, 0, j))],
        out_specs=pl.BlockSpec((1, cc.shape[0], tn), lambda l, j: (l, 0, j)),
        compiler_params=_cparams(("arbitrary", "arbitrary")),
        name="ada",
    )(cc, w_ada, b_ada.reshape(depth, 1, n))


class _Geom:
    def __init__(self, n_ctx_seq, ctx_len, n_lat_seq, lat_len):
        self.n_ctx_seq, self.ctx_len = n_ctx_seq, ctx_len
        self.n_lat_seq, self.lat_len = n_lat_seq, lat_len
        self.n_ctx = n_ctx_seq * ctx_len
        self.n_lat = n_lat_seq * lat_len
        self.n_tok = self.n_ctx + self.n_lat
        assert ctx_len % SEQ_TILE == 0 and lat_len % ROW_TILE == 0 and self.n_ctx % ROW_TILE == 0
        self.cps = ctx_len // SEQ_TILE
        self.lps = lat_len // SEQ_TILE
        self.n_ctx_tiles = n_ctx_seq * self.cps
        self.n_tiles = self.n_ctx_tiles + n_lat_seq * self.lps
        self.n_seq = n_ctx_seq + n_lat_seq

    def mod_row(self, i):
        r = i * ROW_TILE
        return jnp.where(r >= self.n_ctx, 1 + (r - self.n_ctx) // self.lat_len, 0)

    def tile_pos(self, t):
        is_lat = t >= self.n_ctx_tiles
        u = t - self.n_ctx_tiles
        seq = jnp.where(is_lat, self.n_ctx_seq + u // self.lps, t // self.cps)
        pos = jnp.where(is_lat, u % self.lps, t % self.cps)
        nt = jnp.where(is_lat, self.lps, self.cps)
        return seq, pos, nt

    def mirror(self, t):
        _, pos, nt = self.tile_pos(t)
        return t - pos + (nt - 1 - pos)

    def split_specs(self, width):
        nc = self.n_ctx // ROW_TILE
        return [pl.BlockSpec((ROW_TILE, width), lambda i: (jnp.minimum(i, nc - 1), 0)),
                pl.BlockSpec((ROW_TILE, width), lambda i: (jnp.maximum(i - nc, 0), 0))]

    def is_ctx_tile(self, i):
        return i < self.n_ctx // ROW_TILE


def _layer_spec(a, l, **kw):
    return pl.BlockSpec((None,) + a.shape[1:], lambda *_: (l,) + (0,) * (a.ndim - 1), **kw)


def _ada_spec(geom, ada, l):
    return pl.BlockSpec((None, 1) + ada.shape[2:], lambda i: (l, geom.mod_row(i), 0, 0))


def _read_split(ctx_ref, lat_ref, is_ctx):
    return jnp.where(is_ctx, ctx_ref[...], lat_ref[...])


def _inproj_kernel(*refs, geom, n_x):
    x_refs, (ada_ref, nw_ref, w_ref), out_refs = refs[:n_x], refs[n_x:n_x + 3], refs[n_x + 3:]
    x = x_refs[0][...] if n_x == 1 else _read_split(*x_refs, geom.is_ctx_tile(pl.program_id(0)))
    h = _rms(x, nw_ref[...]) * (1.0 + ada_ref[0, 1:2, :]) + ada_ref[0, 0:1, :]
    h = h.astype(BF16)
    off = 0
    for o_ref in out_refs:
        wd = o_ref.shape[-1]
        o_ref[...] = jnp.dot(h, w_ref[:, off:off + wd], preferred_element_type=F32)
        off += wd


def _inproj(geom, xs, ada, norm_w, w_pad, l):
    d = xs[0].shape[1]
    x_specs = [pl.BlockSpec((ROW_TILE, d), lambda i: (i, 0))] if len(xs) == 1 else geom.split_specs(d)
    return pl.pallas_call(
        functools.partial(_inproj_kernel, geom=geom, n_x=len(xs)),
        out_shape=[jax.ShapeDtypeStruct((geom.n_tok, wd), F32) for wd in SEG_WIDTHS],
        grid=(geom.n_tok // ROW_TILE,),
        in_specs=x_specs + [_ada_spec(geom, ada, l), _layer_spec(norm_w, l), _layer_spec(w_pad, l)],
        out_specs=[pl.BlockSpec((ROW_TILE, wd), lambda i: (i, 0)) for wd in SEG_WIDTHS],
        compiler_params=_cparams(("arbitrary",)),
        name="inproj",
    )(*xs, ada, norm_w, w_pad)


def _conv3(x, x_prev, x_next, w_ref):
    n, s = x.shape[0], SUBLANES
    r = lax.broadcasted_iota(jnp.int32, (s, x.shape[1]), 0)
    x_dn, x_up = pltpu.roll(x, 1, 0), pltpu.roll(x, n - 1, 0)
    x_dn = jnp.concatenate([jnp.where(r == 0, x_prev, x_dn[:s]), x_dn[s:]], axis=0)
    x_up = jnp.concatenate([x_up[:n - s], jnp.where(r == s - 1, x_next, x_up[n - s:])], axis=0)
    return w_ref[0:1, :] * x_dn + w_ref[1:2, :] * x + w_ref[2:3, :] * x_up


def _head_lanes(head_l, colfn):
    out = colfn(N_HEADS - 1)
    for h in range(N_HEADS - 2, -1, -1):
        out = jnp.where(head_l == h, colfn(h), out)
    return out


def _cumulative(z, incl_b, parts=2):
    cum = _mm_split_rhs(incl_b, z, parts)
    cumt = _mm_split_lhs(z.T, incl_b, parts, nt=True)
    return cum, cumt


def _unit_tri_inverses(a_list, lvl, eye, out):
    dot = functools.partial(jnp.dot, preferred_element_type=F32)
    a0 = [jnp.where(lvl < 3, a, 0.0) for a in a_list]
    a0b = [a.astype(BF16) for a in a0]
    x = [eye - a for a in a0]
    pb = [dot(a, a).astype(BF16) for a in a0b]
    yield
    x = [xi + dot(xi.astype(BF16), pi) for xi, pi in zip(x, pb)]
    yield
    pb = [dot(pi, pi).astype(BF16) for pi in pb]
    yield
    xb = [(xi + dot(xi.astype(BF16), pi)).astype(BF16) for xi, pi in zip(x, pb)]
    yield
    ab = [a.astype(BF16) for a in a_list]
    for m in range(3, a_list[0].shape[0].bit_length() - 1):
        mask = _ones_where(lvl == m, BF16)
        y = [dot(ai * mask, xi).astype(BF16) for ai, xi in zip(ab, xb)]
        yield
        xb = [xi - dot(xi, yi).astype(BF16) for xi, yi in zip(xb, y)]
        yield
    out.extend(xb)


def _interleave(*gens):
    live = list(gens)
    while live:
        for g in list(live):
            if next(g, StopIteration) is StopIteration:
                live.remove(g)


def _seq_tile_specs(geom, width, mirror):
    rows8 = SEQ_TILE // SUBLANES
    last8 = geom.n_tok // SUBLANES - 1
    tile = (lambda t: geom.mirror(t)) if mirror else (lambda t: t)
    return [pl.BlockSpec((SEQ_TILE, width), lambda t: (tile(t), 0)),
            pl.BlockSpec((SUBLANES, width), lambda t: (jnp.maximum(tile(t) * rows8 - 1, 0), 0)),
            pl.BlockSpec((SUBLANES, width), lambda t: (jnp.minimum((tile(t) + 1) * rows8, last8), 0))]


def _dn_kernel(xf_ref, xfp_ref, xfn_ref, xb_ref, xbp_ref, xbn_ref, gf_ref, gb_ref, s0_ref,
               cw_ref, alog_ref, bias_ref, of_ref, ob_ref, sfin_ref, s_scr, *, geom):
    t = pl.program_id(0)
    seq, pos, nt = geom.tile_pos(t)
    is_ctx = seq < geom.n_ctx_seq
    tt = SEQ_TILE

    @pl.when((pos == 0) & is_ctx)
    def _():
        s_scr[...] = jnp.zeros(s_scr.shape, F32)

    @pl.when((pos == 0) & jnp.logical_not(is_ctx))
    def _():
        s_scr[...] = s0_ref[0]

    row = lax.broadcasted_iota(jnp.int32, (tt, tt), 0)
    col = lax.broadcasted_iota(jnp.int32, (tt, tt), 1)
    xr = row ^ col
    lvl = jnp.where(xr >= 2, 1, 0)
    for kbit in range(2, tt.bit_length() - 1):
        lvl = lvl + jnp.where(xr >= (1 << kbit), 1, 0)
    head_l = lax.broadcasted_iota(jnp.int32, (tt, GROUP_W), 1) // HEAD_DIM
    blockdiag = (lax.broadcasted_iota(jnp.int32, (GROUP_W, GROUP_W), 0) // HEAD_DIM
                 == lax.broadcasted_iota(jnp.int32, (GROUP_W, GROUP_W), 1) // HEAD_DIM)
    eye = _ones_where(row == col, F32)
    gones = _ones_where(blockdiag, BF16)
    head_mask = [_ones_where(head_l == h, BF16) for h in range(N_HEADS)]
    lane_g = lax.broadcasted_iota(jnp.int32, (tt, LANES), 1)
    not_first, not_last = pos > 0, pos < nt - 1

    dirs = ((xf_ref, xfp_ref, xfn_ref, gf_ref, of_ref, not_first, not_last),
            (xb_ref, xbp_ref, xbn_ref, gb_ref, ob_ref, not_last, not_first))
    a_lists, x_lists, pre_out = ([], []), ([], []), [None, None]

    def pre(d):
        x_ref, xp_ref, xn_ref, g_ref, o_ref, has_prev, has_next = dirs[d]
        x_prev = jnp.where(has_prev, xp_ref[SUBLANES - 1:SUBLANES, :], 0.0)
        x_next = jnp.where(has_next, xn_ref[0:1, :], 0.0)
        y = _silu(_conv3(x_ref[...], x_prev, x_next, cw_ref))
        yield
        q, k, v = y[:, :GROUP_W], y[:, GROUP_W:2 * GROUP_W], y[:, 2 * GROUP_W:]
        q = q * lax.rsqrt(_mm(q * q, gones) + EPS) * (HEAD_DIM ** -0.5)
        k = k * lax.rsqrt(_mm(k * k, gones) + EPS)
        yield

        s = g_ref[...]
        gate = -jnp.exp(alog_ref[...]) * _softplus(s + bias_ref[...])
        z = jnp.where(lane_g < SM_ALPHA, _sigmoid(s), gate)
        incl = (row >= col) if d == 0 else (row <= col)
        strict = (row > col) if d == 0 else (row < col)
        cum, cumt = _cumulative(z, _ones_where(incl, BF16))
        cum2, cumt2 = cum * LOG2_E, cumt * LOG2_E
        edge = tt - 1 if d == 0 else 0
        c_beta = SM_BETA + N_HEADS * d
        c_g = SM_ALPHA + N_HEADS * d
        yield

        beta_l = _head_lanes(head_l, lambda h: z[:, c_beta + h:c_beta + h + 1])
        cum_l = _head_lanes(head_l, lambda h: cum[:, c_g + h:c_g + h + 1])
        tot_l = _head_lanes(head_l[0:1, :], lambda h: cum[edge:edge + 1, c_g + h:c_g + h + 1])
        eg = jnp.exp(cum_l)
        kb = k * beta_l
        rhs = jnp.concatenate([v * beta_l, kb * eg], axis=1).astype(BF16)
        kbf, kbb, qb = k.astype(BF16), kb.astype(BF16), q.astype(BF16)
        yield
        qk_heads = []
        for h in range(N_HEADS):
            diff = cum2[:, c_g + h:c_g + h + 1] - cumt2[c_g + h:c_g + h + 1, :]
            decay = jnp.where(incl, jnp.exp2(diff), 0.0)
            a_lists[d].append(jnp.where(strict, _mm_nt(kbb * head_mask[h], kbf) * decay, 0.0))
            qk_heads.append((_mm_nt(qb * head_mask[h], kbf) * decay).astype(BF16))
            yield
        pre_out[d] = (rhs, qk_heads, (q * eg).astype(BF16), k * jnp.exp(tot_l - cum_l),
                      jnp.exp(tot_l), o_ref)

    def tail(d):
        rhs, qk_heads, qg, kd, e_tot, o_ref = pre_out[d]
        mask2 = [jnp.concatenate([m, m], axis=1) for m in head_mask]
        uw = jnp.dot(jnp.concatenate(x_lists[d], axis=1),
                     jnp.concatenate([rhs * m for m in mask2], axis=0), preferred_element_type=F32)
        yield
        state = s_scr[d]
        sb = state.astype(BF16)
        v_new = uw[:, :GROUP_W] - _mm(uw[:, GROUP_W:], sb)
        vb = v_new.astype(BF16)
        yield
        o_ref[...] = jnp.dot(jnp.concatenate([qg] + qk_heads, axis=1),
                             jnp.concatenate([sb] + [vb * m for m in head_mask], axis=0),
                             preferred_element_type=F32)
        yield
        s_scr[d] = state * e_tot + jnp.where(blockdiag, _mm_tn(kd, vb), 0.0)

    _interleave(pre(0), pre(1))
    x_all = []
    _interleave(_unit_tri_inverses(a_lists[0] + a_lists[1], lvl, eye, x_all))
    x_lists[0].extend(x_all[:N_HEADS])
    x_lists[1].extend(x_all[N_HEADS:])
    _interleave(tail(0), tail(1))

    @pl.when((pos == nt - 1) & is_ctx)
    def _():
        for d in range(2):
            for h in range(N_HEADS):
                sl = slice(h * HEAD_DIM, (h + 1) * HEAD_DIM)
                sfin_ref[0, d, h] = s_scr[d, sl, sl]


def _state_specs(geom, block, l):
    seq_of = lambda t: geom.tile_pos(t)[0]
    lat_seq = lambda t: (jnp.maximum(seq_of(t) - geom.n_ctx_seq, 0), l) + (0,) * len(block[0])
    ctx_seq = lambda t: (jnp.minimum(seq_of(t), geom.n_ctx_seq - 1),) + (0,) * len(block[1])
    return pl.BlockSpec((1, None) + block[0], lat_seq), pl.BlockSpec((1,) + block[1], ctx_seq)


def _deltanet(geom, qkv, small, s0, conv_w, alog_row, bias_row, l):
    tt = SEQ_TILE
    s0_spec, sfin_spec = _state_specs(geom, ((2, GROUP_W, GROUP_W), (2, N_HEADS, HEAD_DIM, HEAD_DIM)), l)
    in_specs = (_seq_tile_specs(geom, 3 * GROUP_W, False) + _seq_tile_specs(geom, 3 * GROUP_W, True)
                + [pl.BlockSpec((tt, LANES), lambda t: (t, 0)),
                   pl.BlockSpec((tt, LANES), lambda t: (geom.mirror(t), 0)),
                   s0_spec, _layer_spec(conv_w, l), _layer_spec(alog_row, l), _layer_spec(bias_row, l)])
    return pl.pallas_call(
        functools.partial(_dn_kernel, geom=geom),
        out_shape=[jax.ShapeDtypeStruct((geom.n_tok, GROUP_W), F32),
                   jax.ShapeDtypeStruct((geom.n_tok, GROUP_W), F32),
                   jax.ShapeDtypeStruct((geom.n_ctx_seq, 2, N_HEADS, HEAD_DIM, HEAD_DIM), F32)],
        grid=(geom.n_tiles,),
        in_specs=in_specs,
        out_specs=[pl.BlockSpec((tt, GROUP_W), lambda t: (t, 0)),
                   pl.BlockSpec((tt, GROUP_W), lambda t: (geom.mirror(t), 0)),
                   sfin_spec],
        scratch_shapes=[pltpu.VMEM((2, GROUP_W, GROUP_W), F32)],
        compiler_params=_cparams(("arbitrary",)),
        name="deltanet",
    )(qkv, qkv, qkv, qkv, qkv, qkv, small, small, s0, conv_w, alog_row, bias_row)


def _ssm_kernel(xf_ref, xfp_ref, xfn_ref, xb_ref, xbp_ref, xbn_ref, gf_ref, gb_ref, s0_ref,
                cw_ref, cb_ref, alog_ref, bias_ref, dvec_ref, yf_ref, yb_ref, sfin_ref, s_scr,
                *, geom):
    t = pl.program_id(0)
    seq, pos, nt = geom.tile_pos(t)
    is_ctx = seq < geom.n_ctx_seq
    tt = SEQ_TILE

    @pl.when((pos == 0) & is_ctx)
    def _():
        s_scr[...] = jnp.zeros(s_scr.shape, F32)

    @pl.when((pos == 0) & jnp.logical_not(is_ctx))
    def _():
        s_scr[...] = s0_ref[0]

    row = lax.broadcasted_iota(jnp.int32, (tt, tt), 0)
    col = lax.broadcasted_iota(jnp.int32, (tt, tt), 1)
    head_l = lax.broadcasted_iota(jnp.int32, (tt, GROUP_W), 1) // HEAD_DIM
    lane_g = lax.broadcasted_iota(jnp.int32, (tt, LANES), 1)
    group_l = lane_g // SSM_N
    state_head = lax.broadcasted_iota(jnp.int32, (GROUP_W, 2 * SSM_N), 0) // HEAD_DIM
    state_group = lax.broadcasted_iota(jnp.int32, (GROUP_W, 2 * SSM_N), 1) // SSM_N
    state_mask = state_head // 2 == state_group
    not_first, not_last = pos > 0, pos < nt - 1

    dirs = ((xf_ref, xfp_ref, xfn_ref, gf_ref, yf_ref, not_first, not_last),
            (xb_ref, xbp_ref, xbn_ref, gb_ref, yb_ref, not_last, not_first))
    for d, (x_ref, xp_ref, xn_ref, g_ref, y_ref, has_prev, has_next) in enumerate(dirs):
        x_prev = jnp.where(has_prev, xp_ref[SUBLANES - 1:SUBLANES, :], 0.0)
        x_next = jnp.where(has_next, xn_ref[0:1, :], 0.0)
        y = _silu(_conv3(x_ref[...], x_prev, x_next, cw_ref) + cb_ref[...])
        xs, bm, cm = y[:, :GROUP_W], y[:, GROUP_W:GROUP_W + 2 * SSM_N], y[:, GROUP_W + 2 * SSM_N:]

        dt = _softplus(g_ref[...] + bias_ref[...])
        a = -jnp.exp(alog_ref[...]) * dt
        incl = (row >= col) if d == 0 else (row <= col)
        cum, cumt = _cumulative(a, _ones_where(incl, BF16), parts=3)
        cum2, cumt2 = cum * LOG2_E, cumt * LOG2_E
        edge = tt - 1 if d == 0 else 0
        c0 = SM_DT + N_HEADS * d

        dt_l = _head_lanes(head_l, lambda h: dt[:, c0 + h:c0 + h + 1])
        cum_l = _head_lanes(head_l, lambda h: cum[:, c0 + h:c0 + h + 1])
        tot_l = _head_lanes(head_l[0:1, :], lambda h: cum[edge:edge + 1, c0 + h:c0 + h + 1])
        xdt = xs * dt_l
        cb_scores = [_mm_nt(jnp.where(group_l == g, cm, 0.0), bm) for g in range(2)]
        out = jnp.zeros((tt, GROUP_W), F32)
        for h in range(N_HEADS):
            diff = cum2[:, c0 + h:c0 + h + 1] - cumt2[c0 + h:c0 + h + 1, :]
            lmat = jnp.where(incl, jnp.exp2(diff), 0.0)
            out = jnp.where(head_l == h, _mm(cb_scores[h // 2] * lmat, xdt), out)
        state = s_scr[d]
        out = out + _mm_nt(cm, state) * jnp.exp(cum_l)
        if d == 0:
            out = out + dvec_ref[...] * xs
        y_ref[...] = out
        tot_rows = _head_lanes(state_head, lambda h: cum[edge:edge + 1, c0 + h:c0 + h + 1])
        s_scr[d] = (state * jnp.exp(tot_rows)
                    + jnp.where(state_mask, _mm_tn(xdt * jnp.exp(tot_l - cum_l), bm), 0.0))

    @pl.when((pos == nt - 1) & is_ctx)
    def _():
        for d in range(2):
            for h in range(N_HEADS):
                g = h // 2
                sfin_ref[0, d, h] = s_scr[d, h * HEAD_DIM:(h + 1) * HEAD_DIM, g * SSM_N:(g + 1) * SSM_N]


def _ssd(geom, xbc, small, s0, conv_w, conv_b, alog_row, bias_row, dvec, l):
    tt = SEQ_TILE
    wx = GROUP_W + 4 * SSM_N
    s0_spec, sfin_spec = _state_specs(geom, ((2, GROUP_W, 2 * SSM_N), (2, N_HEADS, HEAD_DIM, SSM_N)), l)
    in_specs = (_seq_tile_specs(geom, wx, False) + _seq_tile_specs(geom, wx, True)
                + [pl.BlockSpec((tt, LANES), lambda t: (t, 0)),
                   pl.BlockSpec((tt, LANES), lambda t: (geom.mirror(t), 0)),
                   s0_spec, _layer_spec(conv_w, l), _layer_spec(conv_b, l), _layer_spec(alog_row, l),
                   _layer_spec(bias_row, l), _layer_spec(dvec, l)])
    return pl.pallas_call(
        functools.partial(_ssm_kernel, geom=geom),
        out_shape=[jax.ShapeDtypeStruct((geom.n_tok, GROUP_W), F32),
                   jax.ShapeDtypeStruct((geom.n_tok, GROUP_W), F32),
                   jax.ShapeDtypeStruct((geom.n_ctx_seq, 2, N_HEADS, HEAD_DIM, SSM_N), F32)],
        grid=(geom.n_tiles,),
        in_specs=in_specs,
        out_specs=[pl.BlockSpec((tt, GROUP_W), lambda t: (t, 0)),
                   pl.BlockSpec((tt, GROUP_W), lambda t: (geom.mirror(t), 0)),
                   sfin_spec],
        scratch_shapes=[pltpu.VMEM((2, GROUP_W, 2 * SSM_N), F32)],
        compiler_params=_cparams(("arbitrary",)),
        name="ssd",
    )(xbc, xbc, xbc, xbc, xbc, xbc, small, small, s0, conv_w, conv_b, alog_row, bias_row, dvec)


def _rope_slab(x, cos, sin_a, sin_b, half):
    w = x.shape[-1]
    return x * cos + pltpu.roll(x, w - half, 1) * sin_a + pltpu.roll(x, half, 1) * sin_b


def _mla_proj_kernel(ql_ref, kvl_ref, sm_ref, cos_ref, sa_ref, sb_ref, qnw_ref, kvnw_ref,
                     wuq_ref, wk_ref, wv_ref, qh_ref, kh_ref, vh_ref, ckv_ref, *, geom):
    cos, sa, sb = cos_ref[...], sa_ref[...], sb_ref[...]
    half = MLA_ROPE // 2
    qp = _mm(_rms(ql_ref[...], qnw_ref[...]), wuq_ref[...]) * (MLA_SCALE * LOG2_E)
    ckv = _rms(kvl_ref[...], kvnw_ref[...])

    @pl.when(geom.is_ctx_tile(pl.program_id(0)))
    def _():
        ckv_ref[...] = ckv

    lane = lax.broadcasted_iota(jnp.int32, cos.shape, 1)
    is_pe = (lane >= MLA_NOPE) & (lane < MLA_NOPE + MLA_ROPE)
    kpe = jnp.where(is_pe, _rope_slab(sm_ref[...], cos, sa, sb, half), 0.0)
    kp = _mm(ckv, wk_ref[...])
    for h in range(N_HEADS):
        sl = slice(h * LANES, (h + 1) * LANES)
        qh_ref[:, sl] = _rope_slab(qp[:, sl], cos, sa, sb, half).astype(BF16)
        kh_ref[:, sl] = (kp[:, sl] + kpe).astype(BF16)
    vh_ref[...] = _mm(ckv, wv_ref[...]).astype(BF16)


def _mla_proj(geom, q_lat, kv_lat, small, cos, sa, sb, qnw, kvnw, wuq, wk, wv, l):
    tm = ROW_TILE
    tok = lambda w: pl.BlockSpec((tm, w), lambda i: (i, 0))
    full = lambda a: _layer_spec(a, l)

    def tab_block(i):
        r = i * tm
        return jnp.where(r >= geom.n_ctx, 1 + ((r - geom.n_ctx) % geom.lat_len) // tm, 0), 0

    tab = pl.BlockSpec((tm, LANES), tab_block)
    return pl.pallas_call(
        functools.partial(_mla_proj_kernel, geom=geom),
        out_shape=[jax.ShapeDtypeStruct((geom.n_tok, N_HEADS * LANES), BF16),
                   jax.ShapeDtypeStruct((geom.n_tok, N_HEADS * LANES), BF16),
                   jax.ShapeDtypeStruct((geom.n_tok, GROUP_W), BF16),
                   jax.ShapeDtypeStruct((geom.n_ctx, MLA_KV_LORA), F32)],
        grid=(geom.n_tok // tm,),
        in_specs=[tok(MLA_Q_LORA), tok(MLA_KV_LORA), tok(LANES), tab, tab, tab,
                  full(qnw), full(kvnw), full(wuq), full(wk), full(wv)],
        out_specs=[tok(N_HEADS * LANES), tok(N_HEADS * LANES), tok(GROUP_W),
                   geom.split_specs(MLA_KV_LORA)[0]],
        compiler_params=_cparams(("arbitrary",)),
        name="mla_proj",
    )(q_lat, kv_lat, small, cos, sa, sb, qnw, kvnw, wuq, wk, wv)


def _mla_attn_kernel(*refs, has_cache):
    if has_cache:
        q_ref, k_ref, v_ref, ckv_ref, kpe_ref, wk_ref, wv_ref, o_ref = refs
        ckv_c = ckv_ref[0]
        v_c = _mm(ckv_c, wv_ref[...])
    else:
        q_ref, k_ref, v_ref, o_ref = refs
    heads = (slice(0, LANES), slice(LANES, 2 * LANES))
    n_keys = k_ref.shape[0]
    chunk = n_keys // KEY_CHUNKS if n_keys % (KEY_CHUNKS * LANES) == 0 else n_keys
    def own_lanes(v, j):
        lane = lax.broadcasted_iota(jnp.int32, v.shape, 1)
        return jnp.where((lane >= HEAD_DIM) == (j == 1), v, 1.0).astype(BF16)

    qs = [q_ref[:, sl] for sl in heads]
    if has_cache:
        scores = [_mm_nt(q, _mm(ckv_c, wk_ref[:, sl]) + kpe_ref[0]) for q, sl in zip(qs, heads)]
        ms = [jnp.max(s, axis=-1, keepdims=True) for s in scores]
        accs = [_mm(jnp.exp2(s - m), own_lanes(v_c, j)) for j, (s, m) in enumerate(zip(scores, ms))]
    for c in range(n_keys // chunk):
        rows = slice(c * chunk, (c + 1) * chunk)
        scores = [_mm_nt(q, k_ref[rows, sl]) for q, sl in zip(qs, heads)]
        v = v_ref[rows, :]
        if c == 0 and not has_cache:
            ms = [jnp.max(s, axis=-1, keepdims=True) for s in scores]
            accs = [_mm(jnp.exp2(s - m), own_lanes(v, j)) for j, (s, m) in enumerate(zip(scores, ms))]
            continue
        for j, s in enumerate(scores):
            m_new = jnp.maximum(ms[j], jnp.max(s, axis=-1, keepdims=True))
            accs[j] = accs[j] * jnp.exp2(ms[j] - m_new) + _mm(jnp.exp2(s - m_new), own_lanes(v, j))
            ms[j] = m_new
    outs = [acc / pltpu.roll(acc, HEAD_DIM, 1) for acc in accs]
    lane = lax.broadcasted_iota(jnp.int32, outs[0].shape, 1)
    o_ref[...] = jnp.where(lane < HEAD_DIM, outs[0], outs[1])


def _mla_attn(qh, kh, vh, n_seq, seq_len, tok0, cache=None, l=0):
    tq = min(2 * ROW_TILE, seq_len)
    nq = seq_len // tq
    q0, k0 = tok0 // tq, tok0 // seq_len
    assert tok0 % seq_len == 0
    in_specs = [pl.BlockSpec((tq, 2 * LANES), lambda b, hp, i: (q0 + b * nq + i, hp)),
                pl.BlockSpec((seq_len, 2 * LANES), lambda b, hp, i: (k0 + b, hp)),
                pl.BlockSpec((seq_len, LANES), lambda b, hp, i: (k0 + b, hp))]
    args = [qh, kh, vh]
    if cache is not None:
        ckv_c, kpe_c, wk, wv = cache
        past = ckv_c.shape[2]
        in_specs += [pl.BlockSpec((1, None, past, MLA_KV_LORA), lambda b, hp, i: (b, l, 0, 0)),
                     pl.BlockSpec((1, None, past, LANES), lambda b, hp, i: (b, l, 0, 0)),
                     pl.BlockSpec((None, MLA_KV_LORA, 2 * LANES), lambda b, hp, i: (l, 0, hp)),
                     pl.BlockSpec((None, MLA_KV_LORA, LANES), lambda b, hp, i: (l, 0, hp))]
        args += [ckv_c, kpe_c, wk, wv]
    return pl.pallas_call(
        functools.partial(_mla_attn_kernel, has_cache=cache is not None),
        out_shape=jax.ShapeDtypeStruct((n_seq * seq_len, GROUP_W), F32),
        grid=(n_seq, 2, nq),
        in_specs=in_specs,
        out_specs=pl.BlockSpec((tq, LANES), lambda b, hp, i: (b * nq + i, hp)),
        compiler_params=_cparams(("parallel", "parallel", "arbitrary")),
        name="mla_attn_lat" if cache is not None else "mla_attn_ctx",
    )(*args)


def _swa_core(q, k_all, v_all, et_ref, sink_ref, valid_t):
    et = et_ref[...]
    kx = _mm_nt(k_all, et).astype(BF16)
    vxt = _mm_nt(et, v_all).astype(BF16)
    head_l = lax.broadcasted_iota(jnp.int32, q.shape, 1) // HEAD_DIM
    heads = range(N_HEADS)
    st = [_mm_nt(kx, jnp.where(head_l == h, q, 0.0)) for h in heads]
    if valid_t is not None:
        st = [jnp.where(valid_t, x, NEG_INF) for x in st]
    sink = [sink_ref[:, h * HEAD_DIM:h * HEAD_DIM + 1] * LOG2_E for h in heads]
    m = [jnp.maximum(jnp.max(st[h], axis=0, keepdims=True), sink[h]) for h in heads]
    pt = [jnp.exp2(st[h] - m[h]) for h in heads]
    den = [jnp.sum(pt[h], axis=0, keepdims=True) + jnp.exp2(sink[h] - m[h]) for h in heads]
    ot = [jnp.dot(vxt, pt[h].astype(BF16), preferred_element_type=F32) / den[h] for h in heads]
    row_head = lax.broadcasted_iota(jnp.int32, ot[0].shape, 0) // HEAD_DIM
    out_t = ot[N_HEADS - 1]
    for h in range(N_HEADS - 2, -1, -1):
        out_t = jnp.where(row_head == h, ot[h], out_t)
    return out_t.T


def _swa_ctx_kernel(x_ref, e_ref, sink_ref, o_ref):
    x = x_ref[...]
    q = x[:, :GROUP_W] * (SWA_SCALE * LOG2_E)
    k, v = x[:, GROUP_W:GROUP_W + LANES], x[:, GROUP_W + LANES:]
    o_ref[...] = _swa_core(q, k, v, e_ref, sink_ref, None)


def _swa_ctx(geom, swa, e_mat, sink_l, l):
    t = geom.ctx_len
    return pl.pallas_call(
        _swa_ctx_kernel,
        out_shape=jax.ShapeDtypeStruct((geom.n_ctx, GROUP_W), F32),
        grid=(geom.n_ctx_seq,),
        in_specs=[pl.BlockSpec((t, 2 * GROUP_W), lambda b: (b, 0)),
                  pl.BlockSpec(e_mat.shape, lambda b: (0, 0)),
                  _layer_spec(sink_l, l)],
        out_specs=pl.BlockSpec((t, GROUP_W), lambda b: (b, 0)),
        compiler_params=_cparams(("parallel",)),
        name="swa_ctx",
    )(swa, e_mat, sink_l)


def _swa_lat_kernel(xc_ref, xp_ref, xn_ref, cc_ref, ac_ref, bc_ref, cp_ref, ap_ref, bp_ref,
                    cn_ref, an_ref, bn_ref, kc_ref, vc_ref, e_ref, sink_ref, o_ref, *, n_tiles):
    i = pl.program_id(1)
    win = SWA_BLOCK
    tq = xc_ref.shape[0]
    half = HEAD_DIM // 2

    def rope(x, c_ref, a_ref, b_ref):
        reps = x.shape[1] // LANES
        wide = lambda t_ref: jnp.concatenate([t_ref[...]] * reps, axis=1) if reps > 1 else t_ref[...]
        return _rope_slab(x, wide(c_ref), wide(a_ref), wide(b_ref), half)

    ksl, vsl = slice(GROUP_W, GROUP_W + LANES), slice(GROUP_W + LANES, 2 * GROUP_W)
    q = rope(xc_ref[:, :GROUP_W], cc_ref, ac_ref, bc_ref) * (SWA_SCALE * LOG2_E)
    k_all = jnp.concatenate([rope(xp_ref[:, ksl], cp_ref, ap_ref, bp_ref),
                             rope(xc_ref[:, ksl], cc_ref, ac_ref, bc_ref),
                             rope(xn_ref[:, ksl], cn_ref, an_ref, bn_ref),
                             kc_ref[0]], axis=0)
    v_all = jnp.concatenate([xp_ref[:, vsl], xc_ref[:, vsl], xn_ref[:, vsl], vc_ref[0]], axis=0)
    nk = k_all.shape[0]
    n_local = tq + 2 * win
    c = lax.broadcasted_iota(jnp.int32, (nk, tq), 0)
    r = lax.broadcasted_iota(jnp.int32, (nk, tq), 1)
    in_seq = ((c >= win) | (i > 0)) & ((c < win + tq) | (i < n_tiles - 1))
    valid_t = ((c >= r) & (c <= r + 2 * win) & in_seq) | (c >= n_local)
    o_ref[...] = _swa_core(q, k_all, v_all, e_ref, sink_ref, valid_t)


def _swa_lat(geom, swa, cos, sa, sb, k_cache, v_cache, e_mat, sink_l, l):
    win = SWA_BLOCK
    tq = 2 * win
    n_tiles = geom.lat_len // tq
    nblk = geom.lat_len // win
    t0, b0 = geom.n_ctx // tq, geom.n_ctx // win
    past = k_cache.shape[2]
    prv = lambda i: jnp.maximum(2 * i - 1, 0)
    nxt = lambda i: jnp.minimum(2 * i + 2, nblk - 1)
    tok = lambda w: [pl.BlockSpec((tq, w), lambda b, i: (t0 + b * n_tiles + i, 0)),
                     pl.BlockSpec((win, w), lambda b, i: (b0 + b * nblk + prv(i), 0)),
                     pl.BlockSpec((win, w), lambda b, i: (b0 + b * nblk + nxt(i), 0))]
    tab = lambda f, rows: [pl.BlockSpec((rows, LANES), lambda b, i: (f(i), 0))] * 3
    return pl.pallas_call(
        functools.partial(_swa_lat_kernel, n_tiles=n_tiles),
        out_shape=jax.ShapeDtypeStruct((geom.n_lat, GROUP_W), F32),
        grid=(geom.n_lat_seq, n_tiles),
        in_specs=tok(2 * GROUP_W) + tab(lambda i: i, tq) + tab(prv, win) + tab(nxt, win)
        + [pl.BlockSpec((1, None, past, LANES), lambda b, i: (b, l, 0, 0)),
           pl.BlockSpec((1, None, past, LANES), lambda b, i: (b, l, 0, 0)),
           pl.BlockSpec(e_mat.shape, lambda b, i: (0, 0)),
           _layer_spec(sink_l, l)],
        out_specs=pl.BlockSpec((tq, GROUP_W), lambda b, i: (b * n_tiles + i, 0)),
        compiler_params=_cparams(("parallel", "arbitrary")),
        name="swa_lat",
    )(swa, swa, swa, cos, sa, sb, cos, sa, sb, cos, sa, sb, k_cache, v_cache, e_mat, sink_l)


def _mix_ffn_kernel(*refs, geom, n_x, final):
    x_refs, refs = refs[:n_x], refs[n_x:]
    (ada_ref, dof_ref, dob_ref, dz_ref, syf_ref, syb_ref, sz_ref, omc_ref, oml_ref, osc_ref, osl_ref,
     dnw_ref, snw_ref, wo_ref, nw_ref, wgu_ref, wd_ref, fw_ref), out_refs = refs[:18], refs[18:]
    is_ctx = geom.is_ctx_tile(pl.program_id(0))
    x = x_refs[0][...] if n_x == 1 else _read_split(*x_refs, is_ctx)

    row = lax.broadcasted_iota(jnp.int32, (GROUP_W, GROUP_W), 0)
    col = lax.broadcasted_iota(jnp.int32, (GROUP_W, GROUP_W), 1)
    gones = _ones_where((row // HEAD_DIM) == (col // HEAD_DIM), BF16)
    o = dof_ref[...] + dob_ref[...]
    ms = _mm(o * o, gones) * (1.0 / HEAD_DIM)
    dn = o * lax.rsqrt(ms + EPS) * dnw_ref[...] * _silu(dz_ref[...])
    acc = _mm(dn, wo_ref[0:GROUP_W, :])
    acc = acc + _mm(_read_split(omc_ref, oml_ref, is_ctx), wo_ref[GROUP_W:2 * GROUP_W, :])
    y = (syf_ref[...] + syb_ref[...]) * _silu(sz_ref[...])
    for g in range(2):
        sl = slice(g * LANES, (g + 1) * LANES)
        acc = acc + _mm(_rms(y[:, sl], snw_ref[:, sl]),
                        wo_ref[2 * GROUP_W + g * LANES:2 * GROUP_W + (g + 1) * LANES, :])
    acc = acc + _mm(_read_split(osc_ref, osl_ref, is_ctx), wo_ref[3 * GROUP_W:, :])
    x = x + ada_ref[0, 2:3, :] * acc

    h = (_rms(x, nw_ref[...]) * (1.0 + ada_ref[0, 4:5, :]) + ada_ref[0, 3:4, :]).astype(BF16)
    acts = []
    for c in range(FF_DIM // FF_CHUNK):
        g = jnp.dot(h, wgu_ref[:, c * FF_CHUNK:(c + 1) * FF_CHUNK], preferred_element_type=F32)
        u = jnp.dot(h, wgu_ref[:, FF_DIM + c * FF_CHUNK:FF_DIM + (c + 1) * FF_CHUNK],
                    preferred_element_type=F32)
        acts.append((_silu(g) * u).astype(BF16))
    acc = jnp.dot(jnp.concatenate(acts, axis=1), wd_ref[...], preferred_element_type=F32)
    y = x + ada_ref[0, 5:6, :] * acc
    if not final:
        out_refs[0][...] = y
        return
    y = _rms(y, fw_ref[...])

    @pl.when(is_ctx)
    def _():
        out_refs[0][...] = y

    @pl.when(jnp.logical_not(is_ctx))
    def _():
        out_refs[1][...] = y


def _mix_ffn(geom, xs, ada, parts, dnw, snw, w_out, norm_w, wgu, w_down, final_w, l, final):
    tm = ROW_TILE
    d = xs[0].shape[1]
    tok = lambda w: pl.BlockSpec((tm, w), lambda i: (i, 0))
    resident = lambda a: _layer_spec(a, l, pipeline_mode=pl.Buffered(1))
    x_specs = [tok(d)] if len(xs) == 1 else geom.split_specs(d)
    if final:
        out_shape = [jax.ShapeDtypeStruct((geom.n_ctx, d), F32), jax.ShapeDtypeStruct((geom.n_lat, d), F32)]
        out_specs = geom.split_specs(d)
    else:
        out_shape = [jax.ShapeDtypeStruct((geom.n_tok, d), F32)]
        out_specs = [tok(d)]
    return pl.pallas_call(
        functools.partial(_mix_ffn_kernel, geom=geom, n_x=len(xs), final=final),
        out_shape=out_shape,
        grid=(geom.n_tok // tm,),
        in_specs=x_specs + [_ada_spec(geom, ada, l)]
        + [tok(GROUP_W)] * 6 + geom.split_specs(GROUP_W) + geom.split_specs(GROUP_W)
        + [resident(dnw), resident(snw), resident(w_out), resident(norm_w), resident(wgu),
           resident(w_down), pl.BlockSpec(final_w.shape, lambda i: (0, 0))],
        out_specs=out_specs,
        compiler_params=_cparams(("arbitrary",)),
        name="mix_ffn",
    )(*xs, ada, *parts, dnw, snw, w_out, norm_w, wgu, w_down, final_w)


def _w_in_layout(w):
    dn, mla, ssm, swa = 0, 1040, 1456, 2232
    zeros = lambda n: jnp.zeros(w.shape[:-1] + (n,), w.dtype)
    small = jnp.concatenate([
        w[..., dn + 1024:dn + 1040],
        w[..., ssm + 768:ssm + 776],
        zeros(SM_KPE - 24),
        w[..., mla + 384:mla + 416],
        zeros(LANES - SM_KPE - MLA_ROPE)], axis=-1)
    return jnp.concatenate([
        w[..., dn:dn + 768], w[..., dn + 768:dn + 1024],
        w[..., mla:mla + 256], w[..., mla + 256:mla + 384],
        w[..., ssm:ssm + 256], w[..., ssm + 256:ssm + 768],
        w[..., swa:swa + 512], small], axis=-1).astype(BF16)


def _gate_rows(dn_vec, ssm_vec):
    depth = dn_vec.shape[0]
    rows = jnp.zeros((depth, 1, LANES), F32)
    rows = rows.at[:, 0, SM_ALPHA:SM_ALPHA + 8].set(dn_vec.reshape(depth, 8))
    return rows.at[:, 0, SM_DT:SM_DT + 8].set(ssm_vec.reshape(depth, 8))


def _axial_angles(rows, rot_dim):
    row_ids = jnp.broadcast_to(jnp.arange(rows)[:, None], (rows, GRID_W)).reshape(-1).astype(F32)
    col_ids = jnp.broadcast_to(jnp.arange(GRID_W)[None, :], (rows, GRID_W)).reshape(-1).astype(F32)
    n_freq = rot_dim // 4
    inv_freq = ROPE_THETA ** (-jnp.arange(n_freq, dtype=F32) / n_freq)
    return jnp.concatenate([row_ids[:, None] * inv_freq, col_ids[:, None] * inv_freq], axis=-1)


def _rope_tables(ang, lane0, reps, n_ident):
    n, half = ang.shape
    cos, sin = jnp.cos(ang), jnp.sin(ang)
    zeros = jnp.zeros_like(sin)
    period = LANES // reps

    def table(first, second, fill):
        one = jnp.concatenate([jnp.full((n, lane0), fill, F32), first, second,
                               jnp.full((n, period - lane0 - 2 * half), fill, F32)], axis=1)
        tab = jnp.concatenate([one] * reps, axis=1)
        ident = jnp.full((n_ident, LANES), fill, F32)
        return jnp.concatenate([ident, tab], axis=0)

    return table(cos, cos, 1.0), table(-sin, zeros, 0.0), table(zeros, sin, 0.0)


def kernel(x_prompt, x_sample, c, state_dn, cache_mla_ckv, cache_mla_kpe, state_ssm, cache_swa_k,
           cache_swa_v, c_ctx, norm1_w, norm2_w, w_ada, b_ada, w_in, w_out, dn_conv_w, dn_a_log,
           dn_dt_bias, dn_norm_w, mla_q_norm_w, mla_w_uq, mla_kv_norm_w, mla_w_ukv, ssm_conv_w,
           ssm_conv_b, ssm_a_log, ssm_dt_bias, ssm_d, ssm_norm_w, swa_sinks, w_gate_up, w_down,
           final_norm_w):
    batch, seq, d = x_prompt.shape
    dec_batch, dec_seq, _ = x_sample.shape
    depth = w_in.shape[0]
    geom = _Geom(batch, seq, dec_batch, dec_seq)
    n_ctx = geom.n_ctx

    xs = (x_prompt.reshape(n_ctx, d), x_sample.reshape(geom.n_lat, d))
    n_mod = -(-(1 + dec_batch) // SUBLANES) * SUBLANES
    cc = jnp.concatenate([c_ctx[None], c, jnp.zeros((n_mod - 1 - dec_batch, d), F32)], axis=0)
    ada = _ada(cc, w_ada, b_ada).reshape(depth, n_mod, 6, d)

    mla_tabs = _rope_tables(_axial_angles(dec_seq // GRID_W, MLA_ROPE), MLA_NOPE, 1, ROW_TILE)
    swa_tabs = _rope_tables(_axial_angles(dec_seq // GRID_W, HEAD_DIM), 0, 2, 0)

    lane = jnp.arange(GROUP_W)
    e_mat = ((lane // LANES) * HEAD_DIM + lane % HEAD_DIM)[:, None] == jnp.arange(LANES)[None, :]
    e_mat = e_mat.astype(BF16)
    eye_h = jnp.eye(N_HEADS, dtype=F32)
    grp_h = (jnp.arange(N_HEADS)[:, None] // 2 == jnp.arange(2)[None, :]).astype(F32)
    row = lambda a: a.reshape(depth, 1, -1)
    w_pad = _w_in_layout(w_in)
    alog_rows = _gate_rows(dn_a_log, ssm_a_log)
    bias_rows = _gate_rows(dn_dt_bias, ssm_dt_bias)
    s0_dn = state_dn[:, :, :, :, :, None, :] * eye_h[None, None, None, :, None, :, None]
    s0_dn = s0_dn.reshape(dec_batch, depth, 2, GROUP_W, GROUP_W)
    s0_ssm = state_ssm[:, :, :, :, :, None, :] * grp_h[None, None, None, :, None, :, None]
    s0_ssm = s0_ssm.reshape(dec_batch, depth, 2, GROUP_W, 2 * SSM_N)
    ssm_dvec = row(jnp.repeat(ssm_d, HEAD_DIM, axis=-1))
    uq = mla_w_uq.reshape(depth, MLA_Q_LORA, N_HEADS, MLA_NOPE + MLA_ROPE)
    wuq = jnp.pad(uq, ((0, 0), (0, 0), (0, 0), (0, LANES - MLA_NOPE - MLA_ROPE)))
    wuq = wuq.reshape(depth, MLA_Q_LORA, N_HEADS * LANES).astype(BF16)
    ukv = mla_w_ukv.reshape(depth, MLA_KV_LORA, N_HEADS, MLA_NOPE + HEAD_DIM)
    wk = jnp.pad(ukv[..., :MLA_NOPE], ((0, 0), (0, 0), (0, 0), (0, LANES - MLA_NOPE)))
    wk = wk.reshape(depth, MLA_KV_LORA, N_HEADS * LANES).astype(BF16)
    wv = ukv[..., MLA_NOPE:].reshape(depth, MLA_KV_LORA, GROUP_W).astype(BF16)
    kpe_c = jnp.pad(cache_mla_kpe, ((0, 0), (0, 0), (0, 0), (MLA_NOPE, LANES - MLA_NOPE - MLA_ROPE)))
    past = cache_swa_k.shape[2]
    swa_kc = cache_swa_k.reshape(dec_batch, depth, past, LANES)
    swa_vc = cache_swa_v.reshape(dec_batch, depth, past, LANES)
    sinks = row(jnp.repeat(swa_sinks, HEAD_DIM, axis=-1))
    dnw = row(jnp.tile(dn_norm_w, (1, N_HEADS)))
    w_out_b, wgu_b, w_down_b = w_out.astype(BF16), w_gate_up.astype(BF16), w_down.astype(BF16)

    st_dn, st_ckv, st_kpe, st_ssm, st_k, st_v = [], [], [], [], [], []
    for l in range(depth):
        segs = _inproj(geom, xs, ada, row(norm1_w), w_pad, l)
        dn_qkv, dn_z, mla_q, mla_kv, ssm_z, ssm_xbc, swa, small = segs

        dn_of, dn_ob, dn_fin = _deltanet(geom, dn_qkv, small, s0_dn, dn_conv_w, alog_rows, bias_rows, l)
        ssm_yf, ssm_yb, ssm_fin = _ssd(geom, ssm_xbc, small, s0_ssm, ssm_conv_w, row(ssm_conv_b),
                                       alog_rows, bias_rows, ssm_dvec, l)
        qh, kh, vh, ckv = _mla_proj(geom, mla_q, mla_kv, small, *mla_tabs, row(mla_q_norm_w),
                                    row(mla_kv_norm_w), wuq, wk, wv, l)
        o_mla_ctx = _mla_attn(qh, kh, vh, batch, seq, 0)
        o_mla_lat = _mla_attn(qh, kh, vh, dec_batch, dec_seq, n_ctx,
                              cache=(cache_mla_ckv, kpe_c, wk, wv), l=l)
        o_swa_ctx = _swa_ctx(geom, swa, e_mat, sinks, l)
        o_swa_lat = _swa_lat(geom, swa, *swa_tabs, swa_kc, swa_vc, e_mat, sinks, l)

        st_dn.append(dn_fin)
        st_ssm.append(ssm_fin)
        st_ckv.append(ckv.reshape(batch, seq, MLA_KV_LORA))
        st_kpe.append(small[:n_ctx, SM_KPE:SM_KPE + MLA_ROPE].reshape(batch, seq, MLA_ROPE))
        st_k.append(swa[:n_ctx, GROUP_W:GROUP_W + LANES].reshape(batch, seq, 2, HEAD_DIM))
        st_v.append(swa[:n_ctx, GROUP_W + LANES:].reshape(batch, seq, 2, HEAD_DIM))

        parts = (dn_of, dn_ob, dn_z, ssm_yf, ssm_yb, ssm_z, o_mla_ctx, o_mla_lat, o_swa_ctx, o_swa_lat)
        xs = _mix_ffn(geom, xs, ada, parts, dnw, row(ssm_norm_w), w_out_b, row(norm2_w), wgu_b, w_down_b,
                      final_norm_w.reshape(1, d), l, l == depth - 1)

    return (xs[0].reshape(batch, seq, d), xs[1].reshape(dec_batch, dec_seq, d),
            jnp.stack(st_dn, axis=1), jnp.stack(st_ckv, axis=1), jnp.stack(st_kpe, axis=1),
            jnp.stack(st_ssm, axis=1), jnp.stack(st_k, axis=1), jnp.stack(st_v, axis=1))
```

```python
import functools

import jax
import jax.numpy as jnp
from jax import lax
from jax.experimental import pallas as pl
from jax.experimental.pallas import tpu as pltpu

F32 = jnp.float32
BF16 = jnp.bfloat16

D_MODEL = 1024
GRID_W = 64
HEAD_DIM = 64
GROUP_W = 256
EPS = 1e-6
ROPE_THETA = 10000.0
NEG_INF = -1e30
N_HEADS = 4
MLA_NOPE = 64
MLA_ROPE = 32
MLA_Q_LORA = 256
MLA_KV_LORA = 128
MLA_SCALE = (MLA_NOPE + MLA_ROPE) ** -0.5
SSM_N = 64
SWA_SCALE = HEAD_DIM ** -0.5
LOG2_E = 1.4426950408889634
SWA_BLOCK = 128
FF_DIM = 2816
FF_CHUNK = 256
KEY_CHUNKS = 4

LANES = 128
SUBLANES = 8
BF16_ROWS = 16
SEQ_TILE = 256
ROW_TILE = 512
VMEM_LIMIT = 56 * 1024 * 1024

SEG_WIDTHS = (768, 256, 256, 128, 256, 512, 512, 128)
IN_PAD = sum(SEG_WIDTHS)
SM_BETA, SM_ALPHA, SM_DT, SM_KPE = 0, 8, 16, 64


def _sigmoid(x):
    return 1.0 / (1.0 + jnp.exp(-x))


def _silu(x):
    half = 0.5 * x
    return half + half * jnp.tanh(half)


def _softplus(x):
    return jnp.maximum(x, 0.0) + jnp.log1p(jnp.exp(-jnp.abs(x)))


def _mm(a, b):
    return jnp.dot(a.astype(BF16), b.astype(BF16), preferred_element_type=F32)


def _mm_nt(a, b):
    return lax.dot_general(a.astype(BF16), b.astype(BF16), (((1,), (1,)), ((), ())),
                           preferred_element_type=F32)


def _mm_tn(a, b):
    return lax.dot_general(a.astype(BF16), b.astype(BF16), (((0,), (0,)), ((), ())),
                           preferred_element_type=F32)


def _split(a, parts):
    out = []
    for _ in range(parts):
        hi = a.astype(BF16)
        out.append(hi)
        a = a - hi.astype(F32)
    return out


def _mm_split_lhs(a, b_exact, parts, nt=False):
    dims = (((1,), (1,)), ((), ())) if nt else (((1,), (0,)), ((), ()))
    acc = None
    for piece in _split(a, parts):
        r = lax.dot_general(piece, b_exact, dims, preferred_element_type=F32)
        acc = r if acc is None else acc + r
    return acc


def _mm_split_rhs(a_exact, b, parts):
    acc = None
    for piece in _split(b, parts):
        r = jnp.dot(a_exact, piece, preferred_element_type=F32)
        acc = r if acc is None else acc + r
    return acc


def _ones_where(mask, dtype):
    return jnp.where(mask, 1.0, 0.0).astype(dtype)


def _rms(x, w):
    return x * lax.rsqrt(jnp.mean(x * x, axis=-1, keepdims=True) + EPS) * w


def _cparams(sem):
    return pltpu.CompilerParams(dimension_semantics=sem, vmem_limit_bytes=VMEM_LIMIT)


def _ada_kernel(c_ref, w_ref, b_ref, o_ref):
    o_ref[0] = _mm(_silu(c_ref[...]), w_ref[0]) + b_ref[0]


def _ada(cc, w_ada, b_ada):
    depth, d, n = w_ada.shape
    tn = 1536
    return pl.pallas_call(
        _ada_kernel,
        out_shape=jax.ShapeDtypeStruct((depth, cc.shape[0], n), F32),
        grid=(depth, n // tn),
        in_specs=[pl.BlockSpec(cc.shape, lambda l, j: (0, 0)),
                  pl.BlockSpec((1, d, tn), lambda l, j: (l, 0, j)),
                  pl.BlockSpec((1, 1, tn), lambda l, j: (l, 0, j))],
        out_specs=pl.BlockSpec((1, cc.shape[0], tn), lambda l, j: (l, 0, j)),
        compiler_params=_cparams(("arbitrary", "arbitrary")),
        name="ada",
    )(cc, w_ada, b_ada.reshape(depth, 1, n))


class _Geom:
    def __init__(self, n_ctx_seq, ctx_len, n_lat_seq, lat_len):
        self.n_ctx_seq, self.ctx_len = n_ctx_seq, ctx_len
        self.n_lat_seq, self.lat_len = n_lat_seq, lat_len
        self.n_ctx = n_ctx_seq * ctx_len
        self.n_lat = n_lat_seq * lat_len
        self.n_tok = self.n_ctx + self.n_lat
        assert ctx_len % SEQ_TILE == 0 and lat_len % ROW_TILE == 0 and self.n_ctx % ROW_TILE == 0
        self.cps = ctx_len // SEQ_TILE
        self.lps = lat_len // SEQ_TILE
        self.n_ctx_tiles = n_ctx_seq * self.cps
        self.n_tiles = self.n_ctx_tiles + n_lat_seq * self.lps
        self.n_seq = n_ctx_seq + n_lat_seq

    def mod_row(self, i):
        r = i * ROW_TILE
        return jnp.where(r >= self.n_ctx, 1 + (r - self.n_ctx) // self.lat_len, 0)

    def tile_pos(self, t):
        is_lat = t >= self.n_ctx_tiles
        u = t - self.n_ctx_tiles
        seq = jnp.where(is_lat, self.n_ctx_seq + u // self.lps, t // self.cps)
        pos = jnp.where(is_lat, u % self.lps, t % self.cps)
        nt = jnp.where(is_lat, self.lps, self.cps)
        return seq, pos, nt

    def mirror(self, t):
        _, pos, nt = self.tile_pos(t)
        return t - pos + (nt - 1 - pos)

    def split_specs(self, width):
        nc = self.n_ctx // ROW_TILE
        return [pl.BlockSpec((ROW_TILE, width), lambda i: (jnp.minimum(i, nc - 1), 0)),
                pl.BlockSpec((ROW_TILE, width), lambda i: (jnp.maximum(i - nc, 0), 0))]

    def is_ctx_tile(self, i):
        return i < self.n_ctx // ROW_TILE


def _layer_spec(a, l, **kw):
    return pl.BlockSpec((None,) + a.shape[1:], lambda *_: (l,) + (0,) * (a.ndim - 1), **kw)


def _ada_spec(geom, ada, l):
    return pl.BlockSpec((None, 1) + ada.shape[2:], lambda i: (l, geom.mod_row(i), 0, 0))


def _read_split(ctx_ref, lat_ref, is_ctx):
    return jnp.where(is_ctx, ctx_ref[...], lat_ref[...])


def _inproj_kernel(*refs, geom, n_x):
    x_refs, (ada_ref, nw_ref, w_ref), out_refs = refs[:n_x], refs[n_x:n_x + 3], refs[n_x + 3:]
    x = x_refs[0][...] if n_x == 1 else _read_split(*x_refs, geom.is_ctx_tile(pl.program_id(0)))
    h = _rms(x, nw_ref[...]) * (1.0 + ada_ref[0, 1:2, :]) + ada_ref[0, 0:1, :]
    h = h.astype(BF16)
    off = 0
    for o_ref in out_refs:
        wd = o_ref.shape[-1]
        o_ref[...] = jnp.dot(h, w_ref[:, off:off + wd], preferred_element_type=F32)
        off += wd


def _inproj(geom, xs, ada, norm_w, w_pad, l):
    d = xs[0].shape[1]
    x_specs = [pl.BlockSpec((ROW_TILE, d), lambda i: (i, 0))] if len(xs) == 1 else geom.split_specs(d)
    return pl.pallas_call(
        functools.partial(_inproj_kernel, geom=geom, n_x=len(xs)),
        out_shape=[jax.ShapeDtypeStruct((geom.n_tok, wd), F32) for wd in SEG_WIDTHS],
        grid=(geom.n_tok // ROW_TILE,),
        in_specs=x_specs + [_ada_spec(geom, ada, l), _layer_spec(norm_w, l), _layer_spec(w_pad, l)],
        out_specs=[pl.BlockSpec((ROW_TILE, wd), lambda i: (i, 0)) for wd in SEG_WIDTHS],
        compiler_params=_cparams(("arbitrary",)),
        name="inproj",
    )(*xs, ada, norm_w, w_pad)


def _conv3(x, x_prev, x_next, w_ref):
    n, s = x.shape[0], SUBLANES
    r = lax.broadcasted_iota(jnp.int32, (s, x.shape[1]), 0)
    x_dn, x_up = pltpu.roll(x, 1, 0), pltpu.roll(x, n - 1, 0)
    x_dn = jnp.concatenate([jnp.where(r == 0, x_prev, x_dn[:s]), x_dn[s:]], axis=0)
    x_up = jnp.concatenate([x_up[:n - s], jnp.where(r == s - 1, x_next, x_up[n - s:])], axis=0)
    return w_ref[0:1, :] * x_dn + w_ref[1:2, :] * x + w_ref[2:3, :] * x_up


def _head_lanes(head_l, colfn):
    out = colfn(N_HEADS - 1)
    for h in range(N_HEADS - 2, -1, -1):
        out = jnp.where(head_l == h, colfn(h), out)
    return out


def _cumulative(z, incl_b, parts=2):
    cum = _mm_split_rhs(incl_b, z, parts)
    cumt = _mm_split_lhs(z.T, incl_b, parts, nt=True)
    return cum, cumt


def _unit_tri_inverses(a_list, lower, lvl, eye, out):
    dot = functools.partial(jnp.dot, preferred_element_type=F32)
    a0 = [jnp.where(lvl < 3, a, 0.0) for a in a_list]
    a0b = [a.astype(BF16) for a in a0]
    x = [eye - a for a in a0]
    pb = [dot(a, a).astype(BF16) for a in a0b]
    yield
    x = [xi + dot(xi.astype(BF16), pi) for xi, pi in zip(x, pb)]
    yield
    pb = [dot(pi, pi).astype(BF16) for pi in pb]
    yield
    xb = [(xi + dot(xi.astype(BF16), pi)).astype(BF16) for xi, pi in zip(x, pb)]
    yield
    ab = [a.astype(BF16) for a in a_list]
    n = a_list[0].shape[0]
    for m in range(3, n.bit_length() - 1):
        mask = _ones_where(lvl == m, BF16)
        s = 1 << m
        dt = F32 if s < BF16_ROWS else BF16
        blocks = range(0, n, 2 * s)
        rows = {True: [slice(b + s, b + 2 * s) for b in blocks], False: [slice(b, b + s) for b in blocks]}
        zeros = jnp.zeros((s, n), dt)

        def take(v, low):
            v = v.astype(dt)
            return jnp.concatenate([v[sl] for sl in rows[low]], axis=0).astype(BF16)

        def spread(vc, low):
            vc = vc.astype(dt)
            pieces = []
            for j in range(len(blocks)):
                piece = vc[j * s:(j + 1) * s]
                pieces += [zeros, piece] if low else [piece, zeros]
            return jnp.concatenate(pieces, axis=0).astype(BF16)

        masks = {low: take(mask, low) for low in (True, False)}
        y = [spread(dot(take(ai, low) * masks[low], xi), low) for ai, xi, low in zip(ab, xb, lower)]
        yield
        xb = [xi - spread(dot(take(xi, low), yi), low) for xi, yi, low in zip(xb, y, lower)]
        yield
    out.extend(xb)


def _interleave(*gens):
    live = list(gens)
    while live:
        for g in list(live):
            if next(g, StopIteration) is StopIteration:
                live.remove(g)


def _seq_tile_specs(geom, width, mirror):
    rows8 = SEQ_TILE // SUBLANES
    last8 = geom.n_tok // SUBLANES - 1
    tile = (lambda t: geom.mirror(t)) if mirror else (lambda t: t)
    return [pl.BlockSpec((SEQ_TILE, width), lambda t: (tile(t), 0)),
            pl.BlockSpec((SUBLANES, width), lambda t: (jnp.maximum(tile(t) * rows8 - 1, 0), 0)),
            pl.BlockSpec((SUBLANES, width), lambda t: (jnp.minimum((tile(t) + 1) * rows8, last8), 0))]


def _dn_kernel(xf_ref, xfp_ref, xfn_ref, xb_ref, xbp_ref, xbn_ref, gf_ref, gb_ref, s0_ref,
               cw_ref, alog_ref, bias_ref, of_ref, ob_ref, sfin_ref, s_scr, *, geom):
    t = pl.program_id(0)
    seq, pos, nt = geom.tile_pos(t)
    is_ctx = seq < geom.n_ctx_seq
    tt = SEQ_TILE

    @pl.when((pos == 0) & is_ctx)
    def _():
        s_scr[...] = jnp.zeros(s_scr.shape, F32)

    @pl.when((pos == 0) & jnp.logical_not(is_ctx))
    def _():
        s_scr[...] = s0_ref[0]

    row = lax.broadcasted_iota(jnp.int32, (tt, tt), 0)
    col = lax.broadcasted_iota(jnp.int32, (tt, tt), 1)
    xr = row ^ col
    lvl = jnp.where(xr >= 2, 1, 0)
    for kbit in range(2, tt.bit_length() - 1):
        lvl = lvl + jnp.where(xr >= (1 << kbit), 1, 0)
    head_l = lax.broadcasted_iota(jnp.int32, (tt, GROUP_W), 1) // HEAD_DIM
    blockdiag = (lax.broadcasted_iota(jnp.int32, (GROUP_W, GROUP_W), 0) // HEAD_DIM
                 == lax.broadcasted_iota(jnp.int32, (GROUP_W, GROUP_W), 1) // HEAD_DIM)
    eye = _ones_where(row == col, F32)
    gones = _ones_where(blockdiag, BF16)
    head_mask = [_ones_where(head_l == h, BF16) for h in range(N_HEADS)]
    lane_g = lax.broadcasted_iota(jnp.int32, (tt, LANES), 1)
    not_first, not_last = pos > 0, pos < nt - 1

    dirs = ((xf_ref, xfp_ref, xfn_ref, gf_ref, of_ref, not_first, not_last),
            (xb_ref, xbp_ref, xbn_ref, gb_ref, ob_ref, not_last, not_first))
    a_lists, x_lists, pre_out = ([], []), ([], []), [None, None]

    def pre(d):
        x_ref, xp_ref, xn_ref, g_ref, o_ref, has_prev, has_next = dirs[d]
        x_prev = jnp.where(has_prev, xp_ref[SUBLANES - 1:SUBLANES, :], 0.0)
        x_next = jnp.where(has_next, xn_ref[0:1, :], 0.0)
        y = _silu(_conv3(x_ref[...], x_prev, x_next, cw_ref))
        yield
        q, k, v = y[:, :GROUP_W], y[:, GROUP_W:2 * GROUP_W], y[:, 2 * GROUP_W:]
        q = q * lax.rsqrt(_mm(q * q, gones) + EPS) * (HEAD_DIM ** -0.5)
        k = k * lax.rsqrt(_mm(k * k, gones) + EPS)
        yield

        s = g_ref[...]
        gate = -jnp.exp(alog_ref[...]) * _softplus(s + bias_ref[...])
        z = jnp.where(lane_g < SM_ALPHA, _sigmoid(s), gate)
        incl = (row >= col) if d == 0 else (row <= col)
        strict = (row > col) if d == 0 else (row < col)
        cum, cumt = _cumulative(z, _ones_where(incl, BF16))
        cum2, cumt2 = cum * LOG2_E, cumt * LOG2_E
        edge = tt - 1 if d == 0 else 0
        c_beta = SM_BETA + N_HEADS * d
        c_g = SM_ALPHA + N_HEADS * d
        yield

        beta_l = _head_lanes(head_l, lambda h: z[:, c_beta + h:c_beta + h + 1])
        cum_l = _head_lanes(head_l, lambda h: cum[:, c_g + h:c_g + h + 1])
        tot_l = _head_lanes(head_l[0:1, :], lambda h: cum[edge:edge + 1, c_g + h:c_g + h + 1])
        eg = jnp.exp(cum_l)
        kb = k * beta_l
        rhs = jnp.concatenate([v * beta_l, kb * eg], axis=1).astype(BF16)
        kbf, kbb, qb = k.astype(BF16), kb.astype(BF16), q.astype(BF16)
        yield
        qk_heads = []
        for h in range(N_HEADS):
            diff = cum2[:, c_g + h:c_g + h + 1] - cumt2[c_g + h:c_g + h + 1, :]
            decay = jnp.where(incl, jnp.exp2(diff), 0.0)
            a_lists[d].append(jnp.where(strict, _mm_nt(kbb * head_mask[h], kbf) * decay, 0.0))
            qk_heads.append((_mm_nt(qb * head_mask[h], kbf) * decay).astype(BF16))
            yield
        pre_out[d] = (rhs, qk_heads, (q * eg).astype(BF16), k * jnp.exp(tot_l - cum_l),
                      jnp.exp(tot_l), o_ref)

    def tail(d):
        rhs, qk_heads, qg, kd, e_tot, o_ref = pre_out[d]
        mask2 = [jnp.concatenate([m, m], axis=1) for m in head_mask]
        uw = jnp.dot(jnp.concatenate(x_lists[d], axis=1),
                     jnp.concatenate([rhs * m for m in mask2], axis=0), preferred_element_type=F32)
        yield
        state = s_scr[d]
        sb = state.astype(BF16)
        v_new = uw[:, :GROUP_W] - _mm(uw[:, GROUP_W:], sb)
        vb = v_new.astype(BF16)
        yield
        o_ref[...] = jnp.dot(jnp.concatenate([qg] + qk_heads, axis=1),
                             jnp.concatenate([sb] + [vb * m for m in head_mask], axis=0),
                             preferred_element_type=F32)
        yield
        s_scr[d] = state * e_tot + jnp.where(blockdiag, _mm_tn(kd, vb), 0.0)

    _interleave(pre(0), pre(1))
    x_all = []
    _interleave(_unit_tri_inverses(a_lists[0] + a_lists[1], [True] * N_HEADS + [False] * N_HEADS, lvl, eye,
                                   x_all))
    x_lists[0].extend(x_all[:N_HEADS])
    x_lists[1].extend(x_all[N_HEADS:])
    _interleave(tail(0), tail(1))

    @pl.when((pos == nt - 1) & is_ctx)
    def _():
        for d in range(2):
            for h in range(N_HEADS):
                sl = slice(h * HEAD_DIM, (h + 1) * HEAD_DIM)
                sfin_ref[0, d, h] = s_scr[d, sl, sl]


def _state_specs(geom, block, l):
    seq_of = lambda t: geom.tile_pos(t)[0]
    lat_seq = lambda t: (jnp.maximum(seq_of(t) - geom.n_ctx_seq, 0), l) + (0,) * len(block[0])
    ctx_seq = lambda t: (jnp.minimum(seq_of(t), geom.n_ctx_seq - 1),) + (0,) * len(block[1])
    return pl.BlockSpec((1, None) + block[0], lat_seq), pl.BlockSpec((1,) + block[1], ctx_seq)


def _deltanet(geom, qkv, small, s0, conv_w, alog_row, bias_row, l):
    tt = SEQ_TILE
    s0_spec, sfin_spec = _state_specs(geom, ((2, GROUP_W, GROUP_W), (2, N_HEADS, HEAD_DIM, HEAD_DIM)), l)
    in_specs = (_seq_tile_specs(geom, 3 * GROUP_W, False) + _seq_tile_specs(geom, 3 * GROUP_W, True)
                + [pl.BlockSpec((tt, LANES), lambda t: (t, 0)),
                   pl.BlockSpec((tt, LANES), lambda t: (geom.mirror(t), 0)),
                   s0_spec, _layer_spec(conv_w, l), _layer_spec(alog_row, l), _layer_spec(bias_row, l)])
    return pl.pallas_call(
        functools.partial(_dn_kernel, geom=geom),
        out_shape=[jax.ShapeDtypeStruct((geom.n_tok, GROUP_W), F32),
                   jax.ShapeDtypeStruct((geom.n_tok, GROUP_W), F32),
                   jax.ShapeDtypeStruct((geom.n_ctx_seq, 2, N_HEADS, HEAD_DIM, HEAD_DIM), F32)],
        grid=(geom.n_tiles,),
        in_specs=in_specs,
        out_specs=[pl.BlockSpec((tt, GROUP_W), lambda t: (t, 0)),
                   pl.BlockSpec((tt, GROUP_W), lambda t: (geom.mirror(t), 0)),
                   sfin_spec],
        scratch_shapes=[pltpu.VMEM((2, GROUP_W, GROUP_W), F32)],
        compiler_params=_cparams(("arbitrary",)),
        name="deltanet",
    )(qkv, qkv, qkv, qkv, qkv, qkv, small, small, s0, conv_w, alog_row, bias_row)


def _ssm_kernel(xf_ref, xfp_ref, xfn_ref, xb_ref, xbp_ref, xbn_ref, gf_ref, gb_ref, s0_ref,
                cw_ref, cb_ref, alog_ref, bias_ref, dvec_ref, yf_ref, yb_ref, sfin_ref, s_scr,
                *, geom):
    t = pl.program_id(0)
    seq, pos, nt = geom.tile_pos(t)
    is_ctx = seq < geom.n_ctx_seq
    tt = SEQ_TILE

    @pl.when((pos == 0) & is_ctx)
    def _():
        s_scr[...] = jnp.zeros(s_scr.shape, F32)

    @pl.when((pos == 0) & jnp.logical_not(is_ctx))
    def _():
        s_scr[...] = s0_ref[0]

    row = lax.broadcasted_iota(jnp.int32, (tt, tt), 0)
    col = lax.broadcasted_iota(jnp.int32, (tt, tt), 1)
    head_l = lax.broadcasted_iota(jnp.int32, (tt, GROUP_W), 1) // HEAD_DIM
    lane_g = lax.broadcasted_iota(jnp.int32, (tt, LANES), 1)
    group_l = lane_g // SSM_N
    state_head = lax.broadcasted_iota(jnp.int32, (GROUP_W, 2 * SSM_N), 0) // HEAD_DIM
    state_group = lax.broadcasted_iota(jnp.int32, (GROUP_W, 2 * SSM_N), 1) // SSM_N
    state_mask = state_head // 2 == state_group
    not_first, not_last = pos > 0, pos < nt - 1

    dirs = ((xf_ref, xfp_ref, xfn_ref, gf_ref, yf_ref, not_first, not_last),
            (xb_ref, xbp_ref, xbn_ref, gb_ref, yb_ref, not_last, not_first))
    for d, (x_ref, xp_ref, xn_ref, g_ref, y_ref, has_prev, has_next) in enumerate(dirs):
        x_prev = jnp.where(has_prev, xp_ref[SUBLANES - 1:SUBLANES, :], 0.0)
        x_next = jnp.where(has_next, xn_ref[0:1, :], 0.0)
        y = _silu(_conv3(x_ref[...], x_prev, x_next, cw_ref) + cb_ref[...])
        xs, bm, cm = y[:, :GROUP_W], y[:, GROUP_W:GROUP_W + 2 * SSM_N], y[:, GROUP_W + 2 * SSM_N:]

        dt = _softplus(g_ref[...] + bias_ref[...])
        a = -jnp.exp(alog_ref[...]) * dt
        incl = (row >= col) if d == 0 else (row <= col)
        cum, cumt = _cumulative(a, _ones_where(incl, BF16), parts=3)
        cum2, cumt2 = cum * LOG2_E, cumt * LOG2_E
        edge = tt - 1 if d == 0 else 0
        c0 = SM_DT + N_HEADS * d

        dt_l = _head_lanes(head_l, lambda h: dt[:, c0 + h:c0 + h + 1])
        cum_l = _head_lanes(head_l, lambda h: cum[:, c0 + h:c0 + h + 1])
        tot_l = _head_lanes(head_l[0:1, :], lambda h: cum[edge:edge + 1, c0 + h:c0 + h + 1])
        xdt = xs * dt_l
        cb_scores = [_mm_nt(jnp.where(group_l == g, cm, 0.0), bm) for g in range(2)]
        out = jnp.zeros((tt, GROUP_W), F32)
        for h in range(N_HEADS):
            diff = cum2[:, c0 + h:c0 + h + 1] - cumt2[c0 + h:c0 + h + 1, :]
            lmat = jnp.where(incl, jnp.exp2(diff), 0.0)
            out = jnp.where(head_l == h, _mm(cb_scores[h // 2] * lmat, xdt), out)
        state = s_scr[d]
        out = out + _mm_nt(cm, state) * jnp.exp(cum_l)
        if d == 0:
            out = out + dvec_ref[...] * xs
        y_ref[...] = out
        tot_rows = _head_lanes(state_head, lambda h: cum[edge:edge + 1, c0 + h:c0 + h + 1])
        s_scr[d] = (state * jnp.exp(tot_rows)
                    + jnp.where(state_mask, _mm_tn(xdt * jnp.exp(tot_l - cum_l), bm), 0.0))

    @pl.when((pos == nt - 1) & is_ctx)
    def _():
        for d in range(2):
            for h in range(N_HEADS):
                g = h // 2
                sfin_ref[0, d, h] = s_scr[d, h * HEAD_DIM:(h + 1) * HEAD_DIM, g * SSM_N:(g + 1) * SSM_N]


def _ssd(geom, xbc, small, s0, conv_w, conv_b, alog_row, bias_row, dvec, l):
    tt = SEQ_TILE
    wx = GROUP_W + 4 * SSM_N
    s0_spec, sfin_spec = _state_specs(geom, ((2, GROUP_W, 2 * SSM_N), (2, N_HEADS, HEAD_DIM, SSM_N)), l)
    in_specs = (_seq_tile_specs(geom, wx, False) + _seq_tile_specs(geom, wx, True)
                + [pl.BlockSpec((tt, LANES), lambda t: (t, 0)),
                   pl.BlockSpec((tt, LANES), lambda t: (geom.mirror(t), 0)),
                   s0_spec, _layer_spec(conv_w, l), _layer_spec(conv_b, l), _layer_spec(alog_row, l),
                   _layer_spec(bias_row, l), _layer_spec(dvec, l)])
    return pl.pallas_call(
        functools.partial(_ssm_kernel, geom=geom),
        out_shape=[jax.ShapeDtypeStruct((geom.n_tok, GROUP_W), F32),
                   jax.ShapeDtypeStruct((geom.n_tok, GROUP_W), F32),
                   jax.ShapeDtypeStruct((geom.n_ctx_seq, 2, N_HEADS, HEAD_DIM, SSM_N), F32)],
        grid=(geom.n_tiles,),
        in_specs=in_specs,
        out_specs=[pl.BlockSpec((tt, GROUP_W), lambda t: (t, 0)),
                   pl.BlockSpec((tt, GROUP_W), lambda t: (geom.mirror(t), 0)),
                   sfin_spec],
        scratch_shapes=[pltpu.VMEM((2, GROUP_W, 2 * SSM_N), F32)],
        compiler_params=_cparams(("arbitrary",)),
        name="ssd",
    )(xbc, xbc, xbc, xbc, xbc, xbc, small, small, s0, conv_w, conv_b, alog_row, bias_row, dvec)


def _rope_slab(x, cos, sin_a, sin_b, half):
    w = x.shape[-1]
    return x * cos + pltpu.roll(x, w - half, 1) * sin_a + pltpu.roll(x, half, 1) * sin_b


def _mla_proj_kernel(ql_ref, kvl_ref, sm_ref, cos_ref, sa_ref, sb_ref, qnw_ref, kvnw_ref,
                     wuq_ref, wk_ref, wv_ref, qh_ref, kh_ref, vh_ref, ckv_ref, *, geom):
    cos, sa, sb = cos_ref[...], sa_ref[...], sb_ref[...]
    half = MLA_ROPE // 2
    qp = _mm(_rms(ql_ref[...], qnw_ref[...]), wuq_ref[...]) * (MLA_SCALE * LOG2_E)
    ckv = _rms(kvl_ref[...], kvnw_ref[...])

    @pl.when(geom.is_ctx_tile(pl.program_id(0)))
    def _():
        ckv_ref[...] = ckv

    lane = lax.broadcasted_iota(jnp.int32, cos.shape, 1)
    is_pe = (lane >= MLA_NOPE) & (lane < MLA_NOPE + MLA_ROPE)
    kpe = jnp.where(is_pe, _rope_slab(sm_ref[...], cos, sa, sb, half), 0.0)
    kp = _mm(ckv, wk_ref[...])
    for h in range(N_HEADS):
        sl = slice(h * LANES, (h + 1) * LANES)
        qh_ref[:, sl] = _rope_slab(qp[:, sl], cos, sa, sb, half).astype(BF16)
        kh_ref[:, sl] = (kp[:, sl] + kpe).astype(BF16)
    vh_ref[...] = _mm(ckv, wv_ref[...]).astype(BF16)


def _mla_proj(geom, q_lat, kv_lat, small, cos, sa, sb, qnw, kvnw, wuq, wk, wv, l):
    tm = ROW_TILE
    tok = lambda w: pl.BlockSpec((tm, w), lambda i: (i, 0))
    full = lambda a: _layer_spec(a, l)

    def tab_block(i):
        r = i * tm
        return jnp.where(r >= geom.n_ctx, 1 + ((r - geom.n_ctx) % geom.lat_len) // tm, 0), 0

    tab = pl.BlockSpec((tm, LANES), tab_block)
    return pl.pallas_call(
        functools.partial(_mla_proj_kernel, geom=geom),
        out_shape=[jax.ShapeDtypeStruct((geom.n_tok, N_HEADS * LANES), BF16),
                   jax.ShapeDtypeStruct((geom.n_tok, N_HEADS * LANES), BF16),
                   jax.ShapeDtypeStruct((geom.n_tok, GROUP_W), BF16),
                   jax.ShapeDtypeStruct((geom.n_ctx, MLA_KV_LORA), F32)],
        grid=(geom.n_tok // tm,),
        in_specs=[tok(MLA_Q_LORA), tok(MLA_KV_LORA), tok(LANES), tab, tab, tab,
                  full(qnw), full(kvnw), full(wuq), full(wk), full(wv)],
        out_specs=[tok(N_HEADS * LANES), tok(N_HEADS * LANES), tok(GROUP_W),
                   geom.split_specs(MLA_KV_LORA)[0]],
        compiler_params=_cparams(("arbitrary",)),
        name="mla_proj",
    )(q_lat, kv_lat, small, cos, sa, sb, qnw, kvnw, wuq, wk, wv)


def _mla_attn_kernel(*refs, has_cache):
    if has_cache:
        q_ref, k_ref, v_ref, ckv_ref, kpe_ref, wk_ref, wv_ref, o_ref = refs
        ckv_c = ckv_ref[0]
        v_c = _mm(ckv_c, wv_ref[...])
    else:
        q_ref, k_ref, v_ref, o_ref = refs
    heads = (slice(0, LANES), slice(LANES, 2 * LANES))
    n_keys = k_ref.shape[0]
    chunk = n_keys // KEY_CHUNKS if n_keys % (KEY_CHUNKS * LANES) == 0 else n_keys
    def own_lanes(v, j):
        lane = lax.broadcasted_iota(jnp.int32, v.shape, 1)
        return jnp.where((lane >= HEAD_DIM) == (j == 1), v, 1.0).astype(BF16)

    qs = [q_ref[:, sl] for sl in heads]
    if has_cache:
        scores = [_mm_nt(q, _mm(ckv_c, wk_ref[:, sl]) + kpe_ref[0]) for q, sl in zip(qs, heads)]
        ms = [jnp.max(s, axis=-1, keepdims=True) for s in scores]
        accs = [_mm(jnp.exp2(s - m), own_lanes(v_c, j)) for j, (s, m) in enumerate(zip(scores, ms))]
    for c in range(n_keys // chunk):
        rows = slice(c * chunk, (c + 1) * chunk)
        scores = [_mm_nt(q, k_ref[rows, sl]) for q, sl in zip(qs, heads)]
        v = v_ref[rows, :]
        if c == 0 and not has_cache:
            ms = [jnp.max(s, axis=-1, keepdims=True) for s in scores]
            accs = [_mm(jnp.exp2(s - m), own_lanes(v, j)) for j, (s, m) in enumerate(zip(scores, ms))]
            continue
        for j, s in enumerate(scores):
            m_new = jnp.maximum(ms[j], jnp.max(s, axis=-1, keepdims=True))
            accs[j] = accs[j] * jnp.exp2(ms[j] - m_new) + _mm(jnp.exp2(s - m_new), own_lanes(v, j))
            ms[j] = m_new
    outs = [acc / pltpu.roll(acc, HEAD_DIM, 1) for acc in accs]
    lane = lax.broadcasted_iota(jnp.int32, outs[0].shape, 1)
    o_ref[...] = jnp.where(lane < HEAD_DIM, outs[0], outs[1])


def _mla_attn(qh, kh, vh, n_seq, seq_len, tok0, cache=None, l=0):
    tq = min(2 * ROW_TILE, seq_len)
    nq = seq_len // tq
    q0, k0 = tok0 // tq, tok0 // seq_len
    assert tok0 % seq_len == 0
    in_specs = [pl.BlockSpec((tq, 2 * LANES), lambda b, hp, i: (q0 + b * nq + i, hp)),
                pl.BlockSpec((seq_len, 2 * LANES), lambda b, hp, i: (k0 + b, hp)),
                pl.BlockSpec((seq_len, LANES), lambda b, hp, i: (k0 + b, hp))]
    args = [qh, kh, vh]
    if cache is not None:
        ckv_c, kpe_c, wk, wv = cache
        past = ckv_c.shape[2]
        in_specs += [pl.BlockSpec((1, None, past, MLA_KV_LORA), lambda b, hp, i: (b, l, 0, 0)),
                     pl.BlockSpec((1, None, past, LANES), lambda b, hp, i: (b, l, 0, 0)),
                     pl.BlockSpec((None, MLA_KV_LORA, 2 * LANES), lambda b, hp, i: (l, 0, hp)),
                     pl.BlockSpec((None, MLA_KV_LORA, LANES), lambda b, hp, i: (l, 0, hp))]
        args += [ckv_c, kpe_c, wk, wv]
    return pl.pallas_call(
        functools.partial(_mla_attn_kernel, has_cache=cache is not None),
        out_shape=jax.ShapeDtypeStruct((n_seq * seq_len, GROUP_W), F32),
        grid=(n_seq, 2, nq),
        in_specs=in_specs,
        out_specs=pl.BlockSpec((tq, LANES), lambda b, hp, i: (b * nq + i, hp)),
        compiler_params=_cparams(("parallel", "parallel", "arbitrary")),
        name="mla_attn_lat" if cache is not None else "mla_attn_ctx",
    )(*args)


def _swa_core(q, k_all, v_all, et_ref, sink_ref, valid_t):
    et = et_ref[...]
    kx = _mm_nt(k_all, et).astype(BF16)
    vxt = _mm_nt(et, v_all).astype(BF16)
    head_l = lax.broadcasted_iota(jnp.int32, q.shape, 1) // HEAD_DIM
    heads = range(N_HEADS)
    st = [_mm_nt(kx, jnp.where(head_l == h, q, 0.0)) for h in heads]
    if valid_t is not None:
        st = [jnp.where(valid_t, x, NEG_INF) for x in st]
    sink = [sink_ref[:, h * HEAD_DIM:h * HEAD_DIM + 1] * LOG2_E for h in heads]
    m = [jnp.maximum(jnp.max(st[h], axis=0, keepdims=True), sink[h]) for h in heads]
    pt = [jnp.exp2(st[h] - m[h]) for h in heads]
    den = [jnp.sum(pt[h], axis=0, keepdims=True) + jnp.exp2(sink[h] - m[h]) for h in heads]
    ot = [jnp.dot(vxt, pt[h].astype(BF16), preferred_element_type=F32) / den[h] for h in heads]
    row_head = lax.broadcasted_iota(jnp.int32, ot[0].shape, 0) // HEAD_DIM
    out_t = ot[N_HEADS - 1]
    for h in range(N_HEADS - 2, -1, -1):
        out_t = jnp.where(row_head == h, ot[h], out_t)
    return out_t.T


def _swa_ctx_kernel(x_ref, e_ref, sink_ref, o_ref):
    x = x_ref[...]
    q = x[:, :GROUP_W] * (SWA_SCALE * LOG2_E)
    k, v = x[:, GROUP_W:GROUP_W + LANES], x[:, GROUP_W + LANES:]
    o_ref[...] = _swa_core(q, k, v, e_ref, sink_ref, None)


def _swa_ctx(geom, swa, e_mat, sink_l, l):
    t = geom.ctx_len
    return pl.pallas_call(
        _swa_ctx_kernel,
        out_shape=jax.ShapeDtypeStruct((geom.n_ctx, GROUP_W), F32),
        grid=(geom.n_ctx_seq,),
        in_specs=[pl.BlockSpec((t, 2 * GROUP_W), lambda b: (b, 0)),
                  pl.BlockSpec(e_mat.shape, lambda b: (0, 0)),
                  _layer_spec(sink_l, l)],
        out_specs=pl.BlockSpec((t, GROUP_W), lambda b: (b, 0)),
        compiler_params=_cparams(("parallel",)),
        name="swa_ctx",
    )(swa, e_mat, sink_l)


def _swa_lat_kernel(xc_ref, xp_ref, xn_ref, cc_ref, ac_ref, bc_ref, cp_ref, ap_ref, bp_ref,
                    cn_ref, an_ref, bn_ref, kc_ref, vc_ref, e_ref, sink_ref, o_ref, *, n_tiles):
    i = pl.program_id(1)
    win = SWA_BLOCK
    tq = xc_ref.shape[0]
    half = HEAD_DIM // 2

    def rope(x, c_ref, a_ref, b_ref):
        reps = x.shape[1] // LANES
        wide = lambda t_ref: jnp.concatenate([t_ref[...]] * reps, axis=1) if reps > 1 else t_ref[...]
        return _rope_slab(x, wide(c_ref), wide(a_ref), wide(b_ref), half)

    ksl, vsl = slice(GROUP_W, GROUP_W + LANES), slice(GROUP_W + LANES, 2 * GROUP_W)
    q = rope(xc_ref[:, :GROUP_W], cc_ref, ac_ref, bc_ref) * (SWA_SCALE * LOG2_E)
    k_all = jnp.concatenate([rope(xp_ref[:, ksl], cp_ref, ap_ref, bp_ref),
                             rope(xc_ref[:, ksl], cc_ref, ac_ref, bc_ref),
                             rope(xn_ref[:, ksl], cn_ref, an_ref, bn_ref),
                             kc_ref[0]], axis=0)
    v_all = jnp.concatenate([xp_ref[:, vsl], xc_ref[:, vsl], xn_ref[:, vsl], vc_ref[0]], axis=0)
    nk = k_all.shape[0]
    n_local = tq + 2 * win
    c = lax.broadcasted_iota(jnp.int32, (nk, tq), 0)
    r = lax.broadcasted_iota(jnp.int32, (nk, tq), 1)
    in_seq = ((c >= win) | (i > 0)) & ((c < win + tq) | (i < n_tiles - 1))
    valid_t = ((c >= r) & (c <= r + 2 * win) & in_seq) | (c >= n_local)
    o_ref[...] = _swa_core(q, k_all, v_all, e_ref, sink_ref, valid_t)


def _swa_lat(geom, swa, cos, sa, sb, k_cache, v_cache, e_mat, sink_l, l):
    win = SWA_BLOCK
    tq = 2 * win
    n_tiles = geom.lat_len // tq
    nblk = geom.lat_len // win
    t0, b0 = geom.n_ctx // tq, geom.n_ctx // win
    past = k_cache.shape[2]
    prv = lambda i: jnp.maximum(2 * i - 1, 0)
    nxt = lambda i: jnp.minimum(2 * i + 2, nblk - 1)
    tok = lambda w: [pl.BlockSpec((tq, w), lambda b, i: (t0 + b * n_tiles + i, 0)),
                     pl.BlockSpec((win, w), lambda b, i: (b0 + b * nblk + prv(i), 0)),
                     pl.BlockSpec((win, w), lambda b, i: (b0 + b * nblk + nxt(i), 0))]
    tab = lambda f, rows: [pl.BlockSpec((rows, LANES), lambda b, i: (f(i), 0))] * 3
    return pl.pallas_call(
        functools.partial(_swa_lat_kernel, n_tiles=n_tiles),
        out_shape=jax.ShapeDtypeStruct((geom.n_lat, GROUP_W), F32),
        grid=(geom.n_lat_seq, n_tiles),
        in_specs=tok(2 * GROUP_W) + tab(lambda i: i, tq) + tab(prv, win) + tab(nxt, win)
        + [pl.BlockSpec((1, None, past, LANES), lambda b, i: (b, l, 0, 0)),
           pl.BlockSpec((1, None, past, LANES), lambda b, i: (b, l, 0, 0)),
           pl.BlockSpec(e_mat.shape, lambda b, i: (0, 0)),
           _layer_spec(sink_l, l)],
        out_specs=pl.BlockSpec((tq, GROUP_W), lambda b, i: (b * n_tiles + i, 0)),
        compiler_params=_cparams(("parallel", "arbitrary")),
        name="swa_lat",
    )(swa, swa, swa, cos, sa, sb, cos, sa, sb, cos, sa, sb, k_cache, v_cache, e_mat, sink_l)


def _mix_ffn_kernel(*refs, geom, n_x, final):
    x_refs, refs = refs[:n_x], refs[n_x:]
    (ada_ref, dof_ref, dob_ref, dz_ref, syf_ref, syb_ref, sz_ref, omc_ref, oml_ref, osc_ref, osl_ref,
     dnw_ref, snw_ref, wo_ref, nw_ref, wgu_ref, wd_ref, fw_ref), out_refs = refs[:18], refs[18:]
    is_ctx = geom.is_ctx_tile(pl.program_id(0))
    x = x_refs[0][...] if n_x == 1 else _read_split(*x_refs, is_ctx)

    row = lax.broadcasted_iota(jnp.int32, (GROUP_W, GROUP_W), 0)
    col = lax.broadcasted_iota(jnp.int32, (GROUP_W, GROUP_W), 1)
    gones = _ones_where((row // HEAD_DIM) == (col // HEAD_DIM), BF16)
    o = dof_ref[...] + dob_ref[...]
    ms = _mm(o * o, gones) * (1.0 / HEAD_DIM)
    dn = o * lax.rsqrt(ms + EPS) * dnw_ref[...] * _silu(dz_ref[...])
    acc = _mm(dn, wo_ref[0:GROUP_W, :])
    acc = acc + _mm(_read_split(omc_ref, oml_ref, is_ctx), wo_ref[GROUP_W:2 * GROUP_W, :])
    y = (syf_ref[...] + syb_ref[...]) * _silu(sz_ref[...])
    for g in range(2):
        sl = slice(g * LANES, (g + 1) * LANES)
        acc = acc + _mm(_rms(y[:, sl], snw_ref[:, sl]),
                        wo_ref[2 * GROUP_W + g * LANES:2 * GROUP_W + (g + 1) * LANES, :])
    acc = acc + _mm(_read_split(osc_ref, osl_ref, is_ctx), wo_ref[3 * GROUP_W:, :])
    x = x + ada_ref[0, 2:3, :] * acc

    h = (_rms(x, nw_ref[...]) * (1.0 + ada_ref[0, 4:5, :]) + ada_ref[0, 3:4, :]).astype(BF16)
    acts = []
    for c in range(FF_DIM // FF_CHUNK):
        g = jnp.dot(h, wgu_ref[:, c * FF_CHUNK:(c + 1) * FF_CHUNK], preferred_element_type=F32)
        u = jnp.dot(h, wgu_ref[:, FF_DIM + c * FF_CHUNK:FF_DIM + (c + 1) * FF_CHUNK],
                    preferred_element_type=F32)
        acts.append((_silu(g) * u).astype(BF16))
    acc = jnp.dot(jnp.concatenate(acts, axis=1), wd_ref[...], preferred_element_type=F32)
    y = x + ada_ref[0, 5:6, :] * acc
    if not final:
        out_refs[0][...] = y
        return
    y = _rms(y, fw_ref[...])

    @pl.when(is_ctx)
    def _():
        out_refs[0][...] = y

    @pl.when(jnp.logical_not(is_ctx))
    def _():
        out_refs[1][...] = y


def _mix_ffn(geom, xs, ada, parts, dnw, snw, w_out, norm_w, wgu, w_down, final_w, l, final):
    tm = ROW_TILE
    d = xs[0].shape[1]
    tok = lambda w: pl.BlockSpec((tm, w), lambda i: (i, 0))
    resident = lambda a: _layer_spec(a, l, pipeline_mode=pl.Buffered(1))
    x_specs = [tok(d)] if len(xs) == 1 else geom.split_specs(d)
    if final:
        out_shape = [jax.ShapeDtypeStruct((geom.n_ctx, d), F32), jax.ShapeDtypeStruct((geom.n_lat, d), F32)]
        out_specs = geom.split_specs(d)
    else:
        out_shape = [jax.ShapeDtypeStruct((geom.n_tok, d), F32)]
        out_specs = [tok(d)]
    return pl.pallas_call(
        functools.partial(_mix_ffn_kernel, geom=geom, n_x=len(xs), final=final),
        out_shape=out_shape,
        grid=(geom.n_tok // tm,),
        in_specs=x_specs + [_ada_spec(geom, ada, l)]
        + [tok(GROUP_W)] * 6 + geom.split_specs(GROUP_W) + geom.split_specs(GROUP_W)
        + [resident(dnw), resident(snw), resident(w_out), resident(norm_w), resident(wgu),
           resident(w_down), pl.BlockSpec(final_w.shape, lambda i: (0, 0))],
        out_specs=out_specs,
        compiler_params=_cparams(("arbitrary",)),
        name="mix_ffn",
    )(*xs, ada, *parts, dnw, snw, w_out, norm_w, wgu, w_down, final_w)


def _w_in_layout(w):
    dn, mla, ssm, swa = 0, 1040, 1456, 2232
    zeros = lambda n: jnp.zeros(w.shape[:-1] + (n,), w.dtype)
    small = jnp.concatenate([
        w[..., dn + 1024:dn + 1040],
        w[..., ssm + 768:ssm + 776],
        zeros(SM_KPE - 24),
        w[..., mla + 384:mla + 416],
        zeros(LANES - SM_KPE - MLA_ROPE)], axis=-1)
    return jnp.concatenate([
        w[..., dn:dn + 768], w[..., dn + 768:dn + 1024],
        w[..., mla:mla + 256], w[..., mla + 256:mla + 384],
        w[..., ssm:ssm + 256], w[..., ssm + 256:ssm + 768],
        w[..., swa:swa + 512], small], axis=-1).astype(BF16)


def _gate_rows(dn_vec, ssm_vec):
    depth = dn_vec.shape[0]
    rows = jnp.zeros((depth, 1, LANES), F32)
    rows = rows.at[:, 0, SM_ALPHA:SM_ALPHA + 8].set(dn_vec.reshape(depth, 8))
    return rows.at[:, 0, SM_DT:SM_DT + 8].set(ssm_vec.reshape(depth, 8))


def _axial_angles(rows, rot_dim):
    row_ids = jnp.broadcast_to(jnp.arange(rows)[:, None], (rows, GRID_W)).reshape(-1).astype(F32)
    col_ids = jnp.broadcast_to(jnp.arange(GRID_W)[None, :], (rows, GRID_W)).reshape(-1).astype(F32)
    n_freq = rot_dim // 4
    inv_freq = ROPE_THETA ** (-jnp.arange(n_freq, dtype=F32) / n_freq)
    return jnp.concatenate([row_ids[:, None] * inv_freq, col_ids[:, None] * inv_freq], axis=-1)


def _rope_tables(ang, lane0, reps, n_ident):
    n, half = ang.shape
    cos, sin = jnp.cos(ang), jnp.sin(ang)
    zeros = jnp.zeros_like(sin)
    period = LANES // reps

    def table(first, second, fill):
        one = jnp.concatenate([jnp.full((n, lane0), fill, F32), first, second,
                               jnp.full((n, period - lane0 - 2 * half), fill, F32)], axis=1)
        tab = jnp.concatenate([one] * reps, axis=1)
        ident = jnp.full((n_ident, LANES), fill, F32)
        return jnp.concatenate([ident, tab], axis=0)

    return table(cos, cos, 1.0), table(-sin, zeros, 0.0), table(zeros, sin, 0.0)


def kernel(x_prompt, x_sample, c, state_dn, cache_mla_ckv, cache_mla_kpe, state_ssm, cache_swa_k,
           cache_swa_v, c_ctx, norm1_w, norm2_w, w_ada, b_ada, w_in, w_out, dn_conv_w, dn_a_log,
           dn_dt_bias, dn_norm_w, mla_q_norm_w, mla_w_uq, mla_kv_norm_w, mla_w_ukv, ssm_conv_w,
           ssm_conv_b, ssm_a_log, ssm_dt_bias, ssm_d, ssm_norm_w, swa_sinks, w_gate_up, w_down,
           final_norm_w):
    batch, seq, d = x_prompt.shape
    dec_batch, dec_seq, _ = x_sample.shape
    depth = w_in.shape[0]
    geom = _Geom(batch, seq, dec_batch, dec_seq)
    n_ctx = geom.n_ctx

    xs = (x_prompt.reshape(n_ctx, d), x_sample.reshape(geom.n_lat, d))
    n_mod = -(-(1 + dec_batch) // SUBLANES) * SUBLANES
    cc = jnp.concatenate([c_ctx[None], c, jnp.zeros((n_mod - 1 - dec_batch, d), F32)], axis=0)
    ada = _ada(cc, w_ada, b_ada).reshape(depth, n_mod, 6, d)

    mla_tabs = _rope_tables(_axial_angles(dec_seq // GRID_W, MLA_ROPE), MLA_NOPE, 1, ROW_TILE)
    swa_tabs = _rope_tables(_axial_angles(dec_seq // GRID_W, HEAD_DIM), 0, 2, 0)

    lane = jnp.arange(GROUP_W)
    e_mat = ((lane // LANES) * HEAD_DIM + lane % HEAD_DIM)[:, None] == jnp.arange(LANES)[None, :]
    e_mat = e_mat.astype(BF16)
    eye_h = jnp.eye(N_HEADS, dtype=F32)
    grp_h = (jnp.arange(N_HEADS)[:, None] // 2 == jnp.arange(2)[None, :]).astype(F32)
    row = lambda a: a.reshape(depth, 1, -1)
    w_pad = _w_in_layout(w_in)
    alog_rows = _gate_rows(dn_a_log, ssm_a_log)
    bias_rows = _gate_rows(dn_dt_bias, ssm_dt_bias)
    s0_dn = state_dn[:, :, :, :, :, None, :] * eye_h[None, None, None, :, None, :, None]
    s0_dn = s0_dn.reshape(dec_batch, depth, 2, GROUP_W, GROUP_W)
    s0_ssm = state_ssm[:, :, :, :, :, None, :] * grp_h[None, None, None, :, None, :, None]
    s0_ssm = s0_ssm.reshape(dec_batch, depth, 2, GROUP_W, 2 * SSM_N)
    ssm_dvec = row(jnp.repeat(ssm_d, HEAD_DIM, axis=-1))
    uq = mla_w_uq.reshape(depth, MLA_Q_LORA, N_HEADS, MLA_NOPE + MLA_ROPE)
    wuq = jnp.pad(uq, ((0, 0), (0, 0), (0, 0), (0, LANES - MLA_NOPE - MLA_ROPE)))
    wuq = wuq.reshape(depth, MLA_Q_LORA, N_HEADS * LANES).astype(BF16)
    ukv = mla_w_ukv.reshape(depth, MLA_KV_LORA, N_HEADS, MLA_NOPE + HEAD_DIM)
    wk = jnp.pad(ukv[..., :MLA_NOPE], ((0, 0), (0, 0), (0, 0), (0, LANES - MLA_NOPE)))
    wk = wk.reshape(depth, MLA_KV_LORA, N_HEADS * LANES).astype(BF16)
    wv = ukv[..., MLA_NOPE:].reshape(depth, MLA_KV_LORA, GROUP_W).astype(BF16)
    kpe_c = jnp.pad(cache_mla_kpe, ((0, 0), (0, 0), (0, 0), (MLA_NOPE, LANES - MLA_NOPE - MLA_ROPE)))
    past = cache_swa_k.shape[2]
    swa_kc = cache_swa_k.reshape(dec_batch, depth, past, LANES)
    swa_vc = cache_swa_v.reshape(dec_batch, depth, past, LANES)
    sinks = row(jnp.repeat(swa_sinks, HEAD_DIM, axis=-1))
    dnw = row(jnp.tile(dn_norm_w, (1, N_HEADS)))
    w_out_b, wgu_b, w_down_b = w_out.astype(BF16), w_gate_up.astype(BF16), w_down.astype(BF16)

    st_dn, st_ckv, st_kpe, st_ssm, st_k, st_v = [], [], [], [], [], []
    for l in range(depth):
        segs = _inproj(geom, xs, ada, row(norm1_w), w_pad, l)
        dn_qkv, dn_z, mla_q, mla_kv, ssm_z, ssm_xbc, swa, small = segs

        dn_of, dn_ob, dn_fin = _deltanet(geom, dn_qkv, small, s0_dn, dn_conv_w, alog_rows, bias_rows, l)
        ssm_yf, ssm_yb, ssm_fin = _ssd(geom, ssm_xbc, small, s0_ssm, ssm_conv_w, row(ssm_conv_b),
                                       alog_rows, bias_rows, ssm_dvec, l)
        qh, kh, vh, ckv = _mla_proj(geom, mla_q, mla_kv, small, *mla_tabs, row(mla_q_norm_w),
                                    row(mla_kv_norm_w), wuq, wk, wv, l)
        o_mla_ctx = _mla_attn(qh, kh, vh, batch, seq, 0)
        o_mla_lat = _mla_attn(qh, kh, vh, dec_batch, dec_seq, n_ctx,
                              cache=(cache_mla_ckv, kpe_c, wk, wv), l=l)
        o_swa_ctx = _swa_ctx(geom, swa, e_mat, sinks, l)
        o_swa_lat = _swa_lat(geom, swa, *swa_tabs, swa_kc, swa_vc, e_mat, sinks, l)

        st_dn.append(dn_fin)
        st_ssm.append(ssm_fin)
        st_ckv.append(ckv.reshape(batch, seq, MLA_KV_LORA))
        st_kpe.append(small[:n_ctx, SM_KPE:SM_KPE + MLA_ROPE].reshape(batch, seq, MLA_ROPE))
        st_k.append(swa[:n_ctx, GROUP_W:GROUP_W + LANES].reshape(batch, seq, 2, HEAD_DIM))
        st_v.append(swa[:n_ctx, GROUP_W + LANES:].reshape(batch, seq, 2, HEAD_DIM))

        parts = (dn_of, dn_ob, dn_z, ssm_yf, ssm_yb, ssm_z, o_mla_ctx, o_mla_lat, o_swa_ctx, o_swa_lat)
        xs = _mix_ffn(geom, xs, ada, parts, dnw, row(ssm_norm_w), w_out_b, row(norm2_w), wgu_b, w_down_b,
                      final_norm_w.reshape(1, d), l, l == depth - 1)

    return (xs[0].reshape(batch, seq, d), xs[1].reshape(dec_batch, dec_seq, d),
            jnp.stack(st_dn, axis=1), jnp.stack(st_ckv, axis=1), jnp.stack(st_kpe, axis=1),
            jnp.stack(st_ssm, axis=1), jnp.stack(st_k, axis=1), jnp.stack(st_v, axis=1))
```

```python
import functools

import jax
import jax.numpy as jnp
from jax import lax
from jax.experimental import pallas as pl
from jax.experimental.pallas import tpu as pltpu

F32 = jnp.float32
BF16 = jnp.bfloat16

D_MODEL = 1024
GRID_W = 64
HEAD_DIM = 64
GROUP_W = 256
EPS = 1e-6
ROPE_THETA = 10000.0
NEG_INF = -1e30
N_HEADS = 4
MLA_NOPE = 64
MLA_ROPE = 32
MLA_Q_LORA = 256
MLA_KV_LORA = 128
MLA_SCALE = (MLA_NOPE + MLA_ROPE) ** -0.5
SSM_N = 64
SWA_SCALE = HEAD_DIM ** -0.5
LOG2_E = 1.4426950408889634
SWA_BLOCK = 128
SWA_TILE_BLOCKS = 4
FF_DIM = 2816
FF_CHUNK = 256
KEY_CHUNKS = 4

LANES = 128
SUBLANES = 8
BF16_ROWS = 16
SEQ_TILE = 256
ROW_TILE = 512
VMEM_LIMIT = 56 * 1024 * 1024

SEG_WIDTHS = (768, 256, 256, 128, 256, 512, 512, 128)
IN_PAD = sum(SEG_WIDTHS)
SM_BETA, SM_ALPHA, SM_DT, SM_KPE = 0, 8, 16, 64


def _sigmoid(x):
    return 1.0 / (1.0 + jnp.exp(-x))


def _silu(x):
    half = 0.5 * x
    return half + half * jnp.tanh(half)


def _softplus(x):
    return jnp.maximum(x, 0.0) + jnp.log1p(jnp.exp(-jnp.abs(x)))


def _mm(a, b):
    return jnp.dot(a.astype(BF16), b.astype(BF16), preferred_element_type=F32)


def _mm_nt(a, b):
    return lax.dot_general(a.astype(BF16), b.astype(BF16), (((1,), (1,)), ((), ())),
                           preferred_element_type=F32)


def _mm_tn(a, b):
    return lax.dot_general(a.astype(BF16), b.astype(BF16), (((0,), (0,)), ((), ())),
                           preferred_element_type=F32)


def _split(a, parts):
    out = []
    for _ in range(parts):
        hi = a.astype(BF16)
        out.append(hi)
        a = a - hi.astype(F32)
    return out


def _mm_split_lhs(a, b_exact, parts, nt=False):
    dims = (((1,), (1,)), ((), ())) if nt else (((1,), (0,)), ((), ()))
    acc = None
    for piece in _split(a, parts):
        r = lax.dot_general(piece, b_exact, dims, preferred_element_type=F32)
        acc = r if acc is None else acc + r
    return acc


def _mm_split_rhs(a_exact, b, parts):
    acc = None
    for piece in _split(b, parts):
        r = jnp.dot(a_exact, piece, preferred_element_type=F32)
        acc = r if acc is None else acc + r
    return acc


def _ones_where(mask, dtype):
    return jnp.where(mask, 1.0, 0.0).astype(dtype)


def _rms(x, w):
    return x * lax.rsqrt(jnp.mean(x * x, axis=-1, keepdims=True) + EPS) * w


def _cparams(sem):
    return pltpu.CompilerParams(dimension_semantics=sem, vmem_limit_bytes=VMEM_LIMIT)


def _ada_kernel(c_ref, w_ref, b_ref, o_ref):
    o_ref[0] = _mm(_silu(c_ref[...]), w_ref[0]) + b_ref[0]


def _ada(cc, w_ada, b_ada):
    depth, d, n = w_ada.shape
    tn = 1536
    return pl.pallas_call(
        _ada_kernel,
        out_shape=jax.ShapeDtypeStruct((depth, cc.shape[0], n), F32),
        grid=(depth, n // tn),
        in_specs=[pl.BlockSpec(cc.shape, lambda l, j: (0, 0)),
                  pl.BlockSpec((1, d, tn), lambda l, j: (l, 0, j)),
                  pl.BlockSpec((1, 1, tn), lambda l, j: (l, 0, j))],
        out_specs=pl.BlockSpec((1, cc.shape[0], tn), lambda l, j: (l, 0, j)),
        compiler_params=_cparams(("arbitrary", "arbitrary")),
        name="ada",
    )(cc, w_ada, b_ada.reshape(depth, 1, n))


class _Geom:
    def __init__(self, n_ctx_seq, ctx_len, n_lat_seq, lat_len):
        self.n_ctx_seq, self.ctx_len = n_ctx_seq, ctx_len
        self.n_lat_seq, self.lat_len = n_lat_seq, lat_len
        self.n_ctx = n_ctx_seq * ctx_len
        self.n_lat = n_lat_seq * lat_len
        self.n_tok = self.n_ctx + self.n_lat
        assert ctx_len % SEQ_TILE == 0 and lat_len % ROW_TILE == 0 and self.n_ctx % ROW_TILE == 0
        self.cps = ctx_len // SEQ_TILE
        self.lps = lat_len // SEQ_TILE
        self.n_ctx_tiles = n_ctx_seq * self.cps
        self.n_tiles = self.n_ctx_tiles + n_lat_seq * self.lps
        self.n_seq = n_ctx_seq + n_lat_seq

    def mod_row(self, i):
        r = i * ROW_TILE
        return jnp.where(r >= self.n_ctx, 1 + (r - self.n_ctx) // self.lat_len, 0)

    def tile_pos(self, t):
        is_lat = t >= self.n_ctx_tiles
        u = t - self.n_ctx_tiles
        seq = jnp.where(is_lat, self.n_ctx_seq + u // self.lps, t // self.cps)
        pos = jnp.where(is_lat, u % self.lps, t % self.cps)
        nt = jnp.where(is_lat, self.lps, self.cps)
        return seq, pos, nt

    def mirror(self, t):
        _, pos, nt = self.tile_pos(t)
        return t - pos + (nt - 1 - pos)

    def split_specs(self, width):
        nc = self.n_ctx // ROW_TILE
        return [pl.BlockSpec((ROW_TILE, width), lambda i: (jnp.minimum(i, nc - 1), 0)),
                pl.BlockSpec((ROW_TILE, width), lambda i: (jnp.maximum(i - nc, 0), 0))]

    def is_ctx_tile(self, i):
        return i < self.n_ctx // ROW_TILE


def _layer_spec(a, l, **kw):
    return pl.BlockSpec((None,) + a.shape[1:], lambda *_: (l,) + (0,) * (a.ndim - 1), **kw)


def _ada_spec(geom, ada, l):
    return pl.BlockSpec((None, 1) + ada.shape[2:], lambda i: (l, geom.mod_row(i), 0, 0))


def _read_split(ctx_ref, lat_ref, is_ctx):
    return jnp.where(is_ctx, ctx_ref[...], lat_ref[...])


def _inproj_kernel(*refs, geom, n_x):
    x_refs, (ada_ref, nw_ref, w_ref), out_refs = refs[:n_x], refs[n_x:n_x + 3], refs[n_x + 3:]
    x = x_refs[0][...] if n_x == 1 else _read_split(*x_refs, geom.is_ctx_tile(pl.program_id(0)))
    h = _rms(x, nw_ref[...]) * (1.0 + ada_ref[0, 1:2, :]) + ada_ref[0, 0:1, :]
    h = h.astype(BF16)
    off = 0
    for o_ref in out_refs:
        wd = o_ref.shape[-1]
        o_ref[...] = jnp.dot(h, w_ref[:, off:off + wd], preferred_element_type=F32)
        off += wd


def _inproj(geom, xs, ada, norm_w, w_pad, l):
    d = xs[0].shape[1]
    x_specs = [pl.BlockSpec((ROW_TILE, d), lambda i: (i, 0))] if len(xs) == 1 else geom.split_specs(d)
    return pl.pallas_call(
        functools.partial(_inproj_kernel, geom=geom, n_x=len(xs)),
        out_shape=[jax.ShapeDtypeStruct((geom.n_tok, wd), F32) for wd in SEG_WIDTHS],
        grid=(geom.n_tok // ROW_TILE,),
        in_specs=x_specs + [_ada_spec(geom, ada, l), _layer_spec(norm_w, l), _layer_spec(w_pad, l)],
        out_specs=[pl.BlockSpec((ROW_TILE, wd), lambda i: (i, 0)) for wd in SEG_WIDTHS],
        compiler_params=_cparams(("arbitrary",)),
        name="inproj",
    )(*xs, ada, norm_w, w_pad)


def _conv3(x, x_prev, x_next, w_ref):
    n, s = x.shape[0], SUBLANES
    r = lax.broadcasted_iota(jnp.int32, (s, x.shape[1]), 0)
    x_dn, x_up = pltpu.roll(x, 1, 0), pltpu.roll(x, n - 1, 0)
    x_dn = jnp.concatenate([jnp.where(r == 0, x_prev, x_dn[:s]), x_dn[s:]], axis=0)
    x_up = jnp.concatenate([x_up[:n - s], jnp.where(r == s - 1, x_next, x_up[n - s:])], axis=0)
    return w_ref[0:1, :] * x_dn + w_ref[1:2, :] * x + w_ref[2:3, :] * x_up


def _head_lanes(head_l, colfn):
    out = colfn(N_HEADS - 1)
    for h in range(N_HEADS - 2, -1, -1):
        out = jnp.where(head_l == h, colfn(h), out)
    return out


def _cumulative(z, incl_b, parts=2):
    cum = _mm_split_rhs(incl_b, z, parts)
    cumt = _mm_split_lhs(z.T, incl_b, parts, nt=True)
    return cum, cumt


def _unit_tri_inverses(a_list, lower, lvl, eye, out):
    dot = functools.partial(jnp.dot, preferred_element_type=F32)
    a0 = [jnp.where(lvl < 3, a, 0.0) for a in a_list]
    a0b = [a.astype(BF16) for a in a0]
    x = [eye - a for a in a0]
    pb = [dot(a, a).astype(BF16) for a in a0b]
    yield
    x = [xi + dot(xi.astype(BF16), pi) for xi, pi in zip(x, pb)]
    yield
    pb = [dot(pi, pi).astype(BF16) for pi in pb]
    yield
    xb = [(xi + dot(xi.astype(BF16), pi)).astype(BF16) for xi, pi in zip(x, pb)]
    yield
    ab = [a.astype(BF16) for a in a_list]
    n = a_list[0].shape[0]
    for m in range(3, n.bit_length() - 1):
        mask = _ones_where(lvl == m, BF16)
        s = 1 << m
        dt = F32 if s < BF16_ROWS else BF16
        blocks = range(0, n, 2 * s)
        rows = {True: [slice(b + s, b + 2 * s) for b in blocks], False: [slice(b, b + s) for b in blocks]}
        zeros = jnp.zeros((s, n), dt)

        def take(v, low):
            v = v.astype(dt)
            return jnp.concatenate([v[sl] for sl in rows[low]], axis=0).astype(BF16)

        def spread(vc, low):
            vc = vc.astype(dt)
            pieces = []
            for j in range(len(blocks)):
                piece = vc[j * s:(j + 1) * s]
                pieces += [zeros, piece] if low else [piece, zeros]
            return jnp.concatenate(pieces, axis=0).astype(BF16)

        masks = {low: take(mask, low) for low in (True, False)}
        y = [spread(dot(take(ai, low) * masks[low], xi), low) for ai, xi, low in zip(ab, xb, lower)]
        yield
        xb = [xi - spread(dot(take(xi, low), yi), low) for xi, yi, low in zip(xb, y, lower)]
        yield
    out.extend(xb)


def _interleave(*gens):
    live = list(gens)
    while live:
        for g in list(live):
            if next(g, StopIteration) is StopIteration:
                live.remove(g)


def _seq_tile_specs(geom, width, mirror):
    rows8 = SEQ_TILE // SUBLANES
    last8 = geom.n_tok // SUBLANES - 1
    tile = (lambda t: geom.mirror(t)) if mirror else (lambda t: t)
    return [pl.BlockSpec((SEQ_TILE, width), lambda t: (tile(t), 0)),
            pl.BlockSpec((SUBLANES, width), lambda t: (jnp.maximum(tile(t) * rows8 - 1, 0), 0)),
            pl.BlockSpec((SUBLANES, width), lambda t: (jnp.minimum((tile(t) + 1) * rows8, last8), 0))]


def _dn_kernel(xf_ref, xfp_ref, xfn_ref, xb_ref, xbp_ref, xbn_ref, gf_ref, gb_ref, s0_ref,
               cw_ref, alog_ref, bias_ref, of_ref, ob_ref, sfin_ref, s_scr, *, geom):
    t = pl.program_id(0)
    seq, pos, nt = geom.tile_pos(t)
    is_ctx = seq < geom.n_ctx_seq
    tt = SEQ_TILE

    @pl.when((pos == 0) & is_ctx)
    def _():
        s_scr[...] = jnp.zeros(s_scr.shape, F32)

    @pl.when((pos == 0) & jnp.logical_not(is_ctx))
    def _():
        s_scr[...] = s0_ref[0]

    row = lax.broadcasted_iota(jnp.int32, (tt, tt), 0)
    col = lax.broadcasted_iota(jnp.int32, (tt, tt), 1)
    xr = row ^ col
    lvl = jnp.where(xr >= 2, 1, 0)
    for kbit in range(2, tt.bit_length() - 1):
        lvl = lvl + jnp.where(xr >= (1 << kbit), 1, 0)
    head_l = lax.broadcasted_iota(jnp.int32, (tt, GROUP_W), 1) // HEAD_DIM
    blockdiag = (lax.broadcasted_iota(jnp.int32, (GROUP_W, GROUP_W), 0) // HEAD_DIM
                 == lax.broadcasted_iota(jnp.int32, (GROUP_W, GROUP_W), 1) // HEAD_DIM)
    eye = _ones_where(row == col, F32)
    gones = _ones_where(blockdiag, BF16)
    head_mask = [_ones_where(head_l == h, BF16) for h in range(N_HEADS)]
    lane_g = lax.broadcasted_iota(jnp.int32, (tt, LANES), 1)
    not_first, not_last = pos > 0, pos < nt - 1

    dirs = ((xf_ref, xfp_ref, xfn_ref, gf_ref, of_ref, not_first, not_last),
            (xb_ref, xbp_ref, xbn_ref, gb_ref, ob_ref, not_last, not_first))
    a_lists, x_lists, pre_out = ([], []), ([], []), [None, None]

    def pre(d):
        x_ref, xp_ref, xn_ref, g_ref, o_ref, has_prev, has_next = dirs[d]
        x_prev = jnp.where(has_prev, xp_ref[SUBLANES - 1:SUBLANES, :], 0.0)
        x_next = jnp.where(has_next, xn_ref[0:1, :], 0.0)
        y = _silu(_conv3(x_ref[...], x_prev, x_next, cw_ref))
        yield
        q, k, v = y[:, :GROUP_W], y[:, GROUP_W:2 * GROUP_W], y[:, 2 * GROUP_W:]
        q = q * lax.rsqrt(_mm(q * q, gones) + EPS) * (HEAD_DIM ** -0.5)
        k = k * lax.rsqrt(_mm(k * k, gones) + EPS)
        yield

        s = g_ref[...]
        gate = -jnp.exp(alog_ref[...]) * _softplus(s + bias_ref[...])
        z = jnp.where(lane_g < SM_ALPHA, _sigmoid(s), gate)
        incl = (row >= col) if d == 0 else (row <= col)
        strict = (row > col) if d == 0 else (row < col)
        cum, cumt = _cumulative(z, _ones_where(incl, BF16))
        cum2, cumt2 = cum * LOG2_E, cumt * LOG2_E
        edge = tt - 1 if d == 0 else 0
        c_beta = SM_BETA + N_HEADS * d
        c_g = SM_ALPHA + N_HEADS * d
        yield

        beta_l = _head_lanes(head_l, lambda h: z[:, c_beta + h:c_beta + h + 1])
        cum_l = _head_lanes(head_l, lambda h: cum[:, c_g + h:c_g + h + 1])
        tot_l = _head_lanes(head_l[0:1, :], lambda h: cum[edge:edge + 1, c_g + h:c_g + h + 1])
        eg = jnp.exp(cum_l)
        kb = k * beta_l
        rhs = jnp.concatenate([v * beta_l, kb * eg], axis=1).astype(BF16)
        kbf, kbb, qb = k.astype(BF16), kb.astype(BF16), q.astype(BF16)
        yield
        qk_heads = []
        for h in range(N_HEADS):
            diff = cum2[:, c_g + h:c_g + h + 1] - cumt2[c_g + h:c_g + h + 1, :]
            decay = jnp.where(incl, jnp.exp2(diff), 0.0)
            a_lists[d].append(jnp.where(strict, _mm_nt(kbb * head_mask[h], kbf) * decay, 0.0))
            qk_heads.append((_mm_nt(qb * head_mask[h], kbf) * decay).astype(BF16))
            yield
        pre_out[d] = (rhs, qk_heads, (q * eg).astype(BF16), k * jnp.exp(tot_l - cum_l),
                      jnp.exp(tot_l), o_ref)

    def tail(d):
        rhs, qk_heads, qg, kd, e_tot, o_ref = pre_out[d]
        mask2 = [jnp.concatenate([m, m], axis=1) for m in head_mask]
        uw = jnp.dot(jnp.concatenate(x_lists[d], axis=1),
                     jnp.concatenate([rhs * m for m in mask2], axis=0), preferred_element_type=F32)
        yield
        state = s_scr[d]
        sb = state.astype(BF16)
        v_new = uw[:, :GROUP_W] - _mm(uw[:, GROUP_W:], sb)
        vb = v_new.astype(BF16)
        yield
        o_ref[...] = jnp.dot(jnp.concatenate([qg] + qk_heads, axis=1),
                             jnp.concatenate([sb] + [vb * m for m in head_mask], axis=0),
                             preferred_element_type=F32)
        yield
        s_scr[d] = state * e_tot + jnp.where(blockdiag, _mm_tn(kd, vb), 0.0)

    _interleave(pre(0), pre(1))
    x_all = []
    _interleave(_unit_tri_inverses(a_lists[0] + a_lists[1], [True] * N_HEADS + [False] * N_HEADS, lvl, eye,
                                   x_all))
    x_lists[0].extend(x_all[:N_HEADS])
    x_lists[1].extend(x_all[N_HEADS:])
    _interleave(tail(0), tail(1))

    @pl.when((pos == nt - 1) & is_ctx)
    def _():
        for d in range(2):
            for h in range(N_HEADS):
                sl = slice(h * HEAD_DIM, (h + 1) * HEAD_DIM)
                sfin_ref[0, d, h] = s_scr[d, sl, sl]


def _state_specs(geom, block, l):
    seq_of = lambda t: geom.tile_pos(t)[0]
    lat_seq = lambda t: (jnp.maximum(seq_of(t) - geom.n_ctx_seq, 0), l) + (0,) * len(block[0])
    ctx_seq = lambda t: (jnp.minimum(seq_of(t), geom.n_ctx_seq - 1),) + (0,) * len(block[1])
    return pl.BlockSpec((1, None) + block[0], lat_seq), pl.BlockSpec((1,) + block[1], ctx_seq)


def _deltanet(geom, qkv, small, s0, conv_w, alog_row, bias_row, l):
    tt = SEQ_TILE
    s0_spec, sfin_spec = _state_specs(geom, ((2, GROUP_W, GROUP_W), (2, N_HEADS, HEAD_DIM, HEAD_DIM)), l)
    in_specs = (_seq_tile_specs(geom, 3 * GROUP_W, False) + _seq_tile_specs(geom, 3 * GROUP_W, True)
                + [pl.BlockSpec((tt, LANES), lambda t: (t, 0)),
                   pl.BlockSpec((tt, LANES), lambda t: (geom.mirror(t), 0)),
                   s0_spec, _layer_spec(conv_w, l), _layer_spec(alog_row, l), _layer_spec(bias_row, l)])
    return pl.pallas_call(
        functools.partial(_dn_kernel, geom=geom),
        out_shape=[jax.ShapeDtypeStruct((geom.n_tok, GROUP_W), F32),
                   jax.ShapeDtypeStruct((geom.n_tok, GROUP_W), F32),
                   jax.ShapeDtypeStruct((geom.n_ctx_seq, 2, N_HEADS, HEAD_DIM, HEAD_DIM), F32)],
        grid=(geom.n_tiles,),
        in_specs=in_specs,
        out_specs=[pl.BlockSpec((tt, GROUP_W), lambda t: (t, 0)),
                   pl.BlockSpec((tt, GROUP_W), lambda t: (geom.mirror(t), 0)),
                   sfin_spec],
        scratch_shapes=[pltpu.VMEM((2, GROUP_W, GROUP_W), F32)],
        compiler_params=_cparams(("arbitrary",)),
        name="deltanet",
    )(qkv, qkv, qkv, qkv, qkv, qkv, small, small, s0, conv_w, alog_row, bias_row)


def _ssm_kernel(xf_ref, xfp_ref, xfn_ref, xb_ref, xbp_ref, xbn_ref, gf_ref, gb_ref, s0_ref,
                cw_ref, cb_ref, alog_ref, bias_ref, dvec_ref, yf_ref, yb_ref, sfin_ref, s_scr,
                *, geom):
    t = pl.program_id(0)
    seq, pos, nt = geom.tile_pos(t)
    is_ctx = seq < geom.n_ctx_seq
    tt = SEQ_TILE

    @pl.when((pos == 0) & is_ctx)
    def _():
        s_scr[...] = jnp.zeros(s_scr.shape, F32)

    @pl.when((pos == 0) & jnp.logical_not(is_ctx))
    def _():
        s_scr[...] = s0_ref[0]

    row = lax.broadcasted_iota(jnp.int32, (tt, tt), 0)
    col = lax.broadcasted_iota(jnp.int32, (tt, tt), 1)
    head_l = lax.broadcasted_iota(jnp.int32, (tt, GROUP_W), 1) // HEAD_DIM
    lane_g = lax.broadcasted_iota(jnp.int32, (tt, LANES), 1)
    group_l = lane_g // SSM_N
    state_head = lax.broadcasted_iota(jnp.int32, (GROUP_W, 2 * SSM_N), 0) // HEAD_DIM
    state_group = lax.broadcasted_iota(jnp.int32, (GROUP_W, 2 * SSM_N), 1) // SSM_N
    state_mask = state_head // 2 == state_group
    not_first, not_last = pos > 0, pos < nt - 1

    dirs = ((xf_ref, xfp_ref, xfn_ref, gf_ref, yf_ref, not_first, not_last),
            (xb_ref, xbp_ref, xbn_ref, gb_ref, yb_ref, not_last, not_first))
    for d, (x_ref, xp_ref, xn_ref, g_ref, y_ref, has_prev, has_next) in enumerate(dirs):
        x_prev = jnp.where(has_prev, xp_ref[SUBLANES - 1:SUBLANES, :], 0.0)
        x_next = jnp.where(has_next, xn_ref[0:1, :], 0.0)
        y = _silu(_conv3(x_ref[...], x_prev, x_next, cw_ref) + cb_ref[...])
        xs, bm, cm = y[:, :GROUP_W], y[:, GROUP_W:GROUP_W + 2 * SSM_N], y[:, GROUP_W + 2 * SSM_N:]

        dt = _softplus(g_ref[...] + bias_ref[...])
        a = -jnp.exp(alog_ref[...]) * dt
        incl = (row >= col) if d == 0 else (row <= col)
        cum, cumt = _cumulative(a, _ones_where(incl, BF16), parts=3)
        cum2, cumt2 = cum * LOG2_E, cumt * LOG2_E
        edge = tt - 1 if d == 0 else 0
        c0 = SM_DT + N_HEADS * d

        dt_l = _head_lanes(head_l, lambda h: dt[:, c0 + h:c0 + h + 1])
        cum_l = _head_lanes(head_l, lambda h: cum[:, c0 + h:c0 + h + 1])
        tot_l = _head_lanes(head_l[0:1, :], lambda h: cum[edge:edge + 1, c0 + h:c0 + h + 1])
        xdt = xs * dt_l
        cb_scores = [_mm_nt(jnp.where(group_l == g, cm, 0.0), bm) for g in range(2)]
        out = jnp.zeros((tt, GROUP_W), F32)
        for h in range(N_HEADS):
            diff = cum2[:, c0 + h:c0 + h + 1] - cumt2[c0 + h:c0 + h + 1, :]
            lmat = jnp.where(incl, jnp.exp2(diff), 0.0)
            out = jnp.where(head_l == h, _mm(cb_scores[h // 2] * lmat, xdt), out)
        state = s_scr[d]
        out = out + _mm_nt(cm, state) * jnp.exp(cum_l)
        if d == 0:
            out = out + dvec_ref[...] * xs
        y_ref[...] = out
        tot_rows = _head_lanes(state_head, lambda h: cum[edge:edge + 1, c0 + h:c0 + h + 1])
        s_scr[d] = (state * jnp.exp(tot_rows)
                    + jnp.where(state_mask, _mm_tn(xdt * jnp.exp(tot_l - cum_l), bm), 0.0))

    @pl.when((pos == nt - 1) & is_ctx)
    def _():
        for d in range(2):
            for h in range(N_HEADS):
                g = h // 2
                sfin_ref[0, d, h] = s_scr[d, h * HEAD_DIM:(h + 1) * HEAD_DIM, g * SSM_N:(g + 1) * SSM_N]


def _ssd(geom, xbc, small, s0, conv_w, conv_b, alog_row, bias_row, dvec, l):
    tt = SEQ_TILE
    wx = GROUP_W + 4 * SSM_N
    s0_spec, sfin_spec = _state_specs(geom, ((2, GROUP_W, 2 * SSM_N), (2, N_HEADS, HEAD_DIM, SSM_N)), l)
    in_specs = (_seq_tile_specs(geom, wx, False) + _seq_tile_specs(geom, wx, True)
                + [pl.BlockSpec((tt, LANES), lambda t: (t, 0)),
                   pl.BlockSpec((tt, LANES), lambda t: (geom.mirror(t), 0)),
                   s0_spec, _layer_spec(conv_w, l), _layer_spec(conv_b, l), _layer_spec(alog_row, l),
                   _layer_spec(bias_row, l), _layer_spec(dvec, l)])
    return pl.pallas_call(
        functools.partial(_ssm_kernel, geom=geom),
        out_shape=[jax.ShapeDtypeStruct((geom.n_tok, GROUP_W), F32),
                   jax.ShapeDtypeStruct((geom.n_tok, GROUP_W), F32),
                   jax.ShapeDtypeStruct((geom.n_ctx_seq, 2, N_HEADS, HEAD_DIM, SSM_N), F32)],
        grid=(geom.n_tiles,),
        in_specs=in_specs,
        out_specs=[pl.BlockSpec((tt, GROUP_W), lambda t: (t, 0)),
                   pl.BlockSpec((tt, GROUP_W), lambda t: (geom.mirror(t), 0)),
                   sfin_spec],
        scratch_shapes=[pltpu.VMEM((2, GROUP_W, 2 * SSM_N), F32)],
        compiler_params=_cparams(("arbitrary",)),
        name="ssd",
    )(xbc, xbc, xbc, xbc, xbc, xbc, small, small, s0, conv_w, conv_b, alog_row, bias_row, dvec)


def _rope_slab(x, cos, sin_a, sin_b, half):
    w = x.shape[-1]
    return x * cos + pltpu.roll(x, w - half, 1) * sin_a + pltpu.roll(x, half, 1) * sin_b


def _mla_proj_kernel(ql_ref, kvl_ref, sm_ref, cos_ref, sa_ref, sb_ref, qnw_ref, kvnw_ref,
                     wuq_ref, wk_ref, wv_ref, qh_ref, kh_ref, vh_ref, ckv_ref, *, geom):
    cos, sa, sb = cos_ref[...], sa_ref[...], sb_ref[...]
    half = MLA_ROPE // 2
    qp = _mm(_rms(ql_ref[...], qnw_ref[...]), wuq_ref[...]) * (MLA_SCALE * LOG2_E)
    ckv = _rms(kvl_ref[...], kvnw_ref[...])

    @pl.when(geom.is_ctx_tile(pl.program_id(0)))
    def _():
        ckv_ref[...] = ckv

    lane = lax.broadcasted_iota(jnp.int32, cos.shape, 1)
    is_pe = (lane >= MLA_NOPE) & (lane < MLA_NOPE + MLA_ROPE)
    kpe = jnp.where(is_pe, _rope_slab(sm_ref[...], cos, sa, sb, half), 0.0)
    kp = _mm(ckv, wk_ref[...])
    for h in range(N_HEADS):
        sl = slice(h * LANES, (h + 1) * LANES)
        qh_ref[:, sl] = _rope_slab(qp[:, sl], cos, sa, sb, half).astype(BF16)
        kh_ref[:, sl] = (kp[:, sl] + kpe).astype(BF16)
    vh_ref[...] = _mm(ckv, wv_ref[...]).astype(BF16)


def _mla_proj(geom, q_lat, kv_lat, small, cos, sa, sb, qnw, kvnw, wuq, wk, wv, l):
    tm = ROW_TILE
    tok = lambda w: pl.BlockSpec((tm, w), lambda i: (i, 0))
    full = lambda a: _layer_spec(a, l)

    def tab_block(i):
        r = i * tm
        return jnp.where(r >= geom.n_ctx, 1 + ((r - geom.n_ctx) % geom.lat_len) // tm, 0), 0

    tab = pl.BlockSpec((tm, LANES), tab_block)
    return pl.pallas_call(
        functools.partial(_mla_proj_kernel, geom=geom),
        out_shape=[jax.ShapeDtypeStruct((geom.n_tok, N_HEADS * LANES), BF16),
                   jax.ShapeDtypeStruct((geom.n_tok, N_HEADS * LANES), BF16),
                   jax.ShapeDtypeStruct((geom.n_tok, GROUP_W), BF16),
                   jax.ShapeDtypeStruct((geom.n_ctx, MLA_KV_LORA), F32)],
        grid=(geom.n_tok // tm,),
        in_specs=[tok(MLA_Q_LORA), tok(MLA_KV_LORA), tok(LANES), tab, tab, tab,
                  full(qnw), full(kvnw), full(wuq), full(wk), full(wv)],
        out_specs=[tok(N_HEADS * LANES), tok(N_HEADS * LANES), tok(GROUP_W),
                   geom.split_specs(MLA_KV_LORA)[0]],
        compiler_params=_cparams(("arbitrary",)),
        name="mla_proj",
    )(q_lat, kv_lat, small, cos, sa, sb, qnw, kvnw, wuq, wk, wv)


def _mla_attn_kernel(*refs, has_cache):
    if has_cache:
        q_ref, k_ref, v_ref, ckv_ref, kpe_ref, wk_ref, wv_ref, o_ref = refs
        ckv_c = ckv_ref[0]
        v_c = _mm(ckv_c, wv_ref[...])
    else:
        q_ref, k_ref, v_ref, o_ref = refs
    n_heads = q_ref.shape[1] // LANES
    heads = [slice(j * LANES, (j + 1) * LANES) for j in range(n_heads)]
    pair = lambda j: slice((j // 2) * LANES, (j // 2 + 1) * LANES)
    n_keys = k_ref.shape[0]
    chunk = n_keys // KEY_CHUNKS if n_keys % (KEY_CHUNKS * LANES) == 0 else n_keys
    def own_lanes(v, j):
        v = v[:, pair(j)]
        lane = lax.broadcasted_iota(jnp.int32, v.shape, 1)
        return jnp.where((lane >= HEAD_DIM) == (j % 2 == 1), v, 1.0).astype(BF16)

    qs = [q_ref[:, sl] for sl in heads]
    if has_cache:
        scores = [_mm_nt(q, _mm(ckv_c, wk_ref[:, sl]) + kpe_ref[0]) for q, sl in zip(qs, heads)]
        ms = [jnp.max(s, axis=-1, keepdims=True) for s in scores]
        accs = [_mm(jnp.exp2(s - m), own_lanes(v_c, j)) for j, (s, m) in enumerate(zip(scores, ms))]
    for c in range(n_keys // chunk):
        rows = slice(c * chunk, (c + 1) * chunk)
        scores = [_mm_nt(q, k_ref[rows, sl]) for q, sl in zip(qs, heads)]
        v = v_ref[rows, :]
        if c == 0 and not has_cache:
            ms = [jnp.max(s, axis=-1, keepdims=True) for s in scores]
            accs = [_mm(jnp.exp2(s - m), own_lanes(v, j)) for j, (s, m) in enumerate(zip(scores, ms))]
            continue
        for j, s in enumerate(scores):
            m_new = jnp.maximum(ms[j], jnp.max(s, axis=-1, keepdims=True))
            accs[j] = accs[j] * jnp.exp2(ms[j] - m_new) + _mm(jnp.exp2(s - m_new), own_lanes(v, j))
            ms[j] = m_new
    outs = [acc / pltpu.roll(acc, HEAD_DIM, 1) for acc in accs]
    lane = lax.broadcasted_iota(jnp.int32, outs[0].shape, 1)
    for j in range(0, n_heads, 2):
        o_ref[:, pair(j)] = jnp.where(lane < HEAD_DIM, outs[j], outs[j + 1])


def _mla_attn(qh, kh, vh, n_seq, seq_len, tok0, cache=None, l=0):
    tq = min(2 * ROW_TILE, seq_len)
    nq = seq_len // tq
    q0, k0 = tok0 // tq, tok0 // seq_len
    assert tok0 % seq_len == 0
    pairs = 1 if cache is not None else N_HEADS // 2
    in_specs = [pl.BlockSpec((tq, 2 * LANES * pairs), lambda b, hp, i: (q0 + b * nq + i, hp)),
                pl.BlockSpec((seq_len, 2 * LANES * pairs), lambda b, hp, i: (k0 + b, hp)),
                pl.BlockSpec((seq_len, LANES * pairs), lambda b, hp, i: (k0 + b, hp))]
    args = [qh, kh, vh]
    if cache is not None:
        ckv_c, kpe_c, wk, wv = cache
        past = ckv_c.shape[2]
        in_specs += [pl.BlockSpec((1, None, past, MLA_KV_LORA), lambda b, hp, i: (b, l, 0, 0)),
                     pl.BlockSpec((1, None, past, LANES), lambda b, hp, i: (b, l, 0, 0)),
                     pl.BlockSpec((None, MLA_KV_LORA, 2 * LANES), lambda b, hp, i: (l, 0, hp)),
                     pl.BlockSpec((None, MLA_KV_LORA, LANES), lambda b, hp, i: (l, 0, hp))]
        args += [ckv_c, kpe_c, wk, wv]
    return pl.pallas_call(
        functools.partial(_mla_attn_kernel, has_cache=cache is not None),
        out_shape=jax.ShapeDtypeStruct((n_seq * seq_len, GROUP_W), F32),
        grid=(n_seq, N_HEADS // 2 // pairs, nq),
        in_specs=in_specs,
        out_specs=pl.BlockSpec((tq, LANES * pairs), lambda b, hp, i: (b * nq + i, hp)),
        compiler_params=_cparams(("parallel", "parallel", "arbitrary")),
        name="mla_attn_lat" if cache is not None else "mla_attn_ctx",
    )(*args)


def _swa_core(q, k_all, v_all, et_ref, sink_ref, valid_t):
    et = et_ref[...]
    kx = _mm_nt(k_all, et).astype(BF16)
    vxt = _mm_nt(et, v_all).astype(BF16)
    head_l = lax.broadcasted_iota(jnp.int32, q.shape, 1) // HEAD_DIM
    heads = range(N_HEADS)
    st = [_mm_nt(kx, jnp.where(head_l == h, q, 0.0)) for h in heads]
    if valid_t is not None:
        st = [jnp.where(valid_t, x, NEG_INF) for x in st]
    sink = [sink_ref[:, h * HEAD_DIM:h * HEAD_DIM + 1] * LOG2_E for h in heads]
    m = [jnp.maximum(jnp.max(st[h], axis=0, keepdims=True), sink[h]) for h in heads]
    pt = [jnp.exp2(st[h] - m[h]) for h in heads]
    den = [jnp.sum(pt[h], axis=0, keepdims=True) + jnp.exp2(sink[h] - m[h]) for h in heads]
    ot = [jnp.dot(vxt, pt[h].astype(BF16), preferred_element_type=F32) / den[h] for h in heads]
    row_head = lax.broadcasted_iota(jnp.int32, ot[0].shape, 0) // HEAD_DIM
    out_t = ot[N_HEADS - 1]
    for h in range(N_HEADS - 2, -1, -1):
        out_t = jnp.where(row_head == h, ot[h], out_t)
    return out_t.T


def _swa_ctx_kernel(x_ref, e_ref, sink_ref, o_ref):
    x = x_ref[...]
    q = x[:, :GROUP_W] * (SWA_SCALE * LOG2_E)
    k, v = x[:, GROUP_W:GROUP_W + LANES], x[:, GROUP_W + LANES:]
    o_ref[...] = _swa_core(q, k, v, e_ref, sink_ref, None)


def _swa_ctx(geom, swa, e_mat, sink_l, l):
    t = geom.ctx_len
    return pl.pallas_call(
        _swa_ctx_kernel,
        out_shape=jax.ShapeDtypeStruct((geom.n_ctx, GROUP_W), F32),
        grid=(geom.n_ctx_seq,),
        in_specs=[pl.BlockSpec((t, 2 * GROUP_W), lambda b: (b, 0)),
                  pl.BlockSpec(e_mat.shape, lambda b: (0, 0)),
                  _layer_spec(sink_l, l)],
        out_specs=pl.BlockSpec((t, GROUP_W), lambda b: (b, 0)),
        compiler_params=_cparams(("parallel",)),
        name="swa_ctx",
    )(swa, e_mat, sink_l)


def _swa_lat_kernel(xc_ref, xp_ref, xn_ref, cc_ref, ac_ref, bc_ref, cp_ref, ap_ref, bp_ref,
                    cn_ref, an_ref, bn_ref, kc_ref, vc_ref, e_ref, sink_ref, o_ref, *, n_tiles):
    i = pl.program_id(1)
    win = SWA_BLOCK
    tq = xc_ref.shape[0]
    half = HEAD_DIM // 2

    def rope(x, c_ref, a_ref, b_ref):
        reps = x.shape[1] // LANES
        wide = lambda t_ref: jnp.concatenate([t_ref[...]] * reps, axis=1) if reps > 1 else t_ref[...]
        return _rope_slab(x, wide(c_ref), wide(a_ref), wide(b_ref), half)

    ksl, vsl = slice(GROUP_W, GROUP_W + LANES), slice(GROUP_W + LANES, 2 * GROUP_W)
    q = rope(xc_ref[:, :GROUP_W], cc_ref, ac_ref, bc_ref) * (SWA_SCALE * LOG2_E)
    k_all = jnp.concatenate([rope(xp_ref[:, ksl], cp_ref, ap_ref, bp_ref),
                             rope(xc_ref[:, ksl], cc_ref, ac_ref, bc_ref),
                             rope(xn_ref[:, ksl], cn_ref, an_ref, bn_ref),
                             kc_ref[0]], axis=0)
    v_all = jnp.concatenate([xp_ref[:, vsl], xc_ref[:, vsl], xn_ref[:, vsl], vc_ref[0]], axis=0)
    nk = k_all.shape[0]
    n_local = tq + 2 * win
    c = lax.broadcasted_iota(jnp.int32, (nk, tq), 0)
    r = lax.broadcasted_iota(jnp.int32, (nk, tq), 1)
    in_seq = ((c >= win) | (i > 0)) & ((c < win + tq) | (i < n_tiles - 1))
    valid_t = ((c >= r) & (c <= r + 2 * win) & in_seq) | (c >= n_local)
    o_ref[...] = _swa_core(q, k_all, v_all, e_ref, sink_ref, valid_t)


def _swa_lat(geom, swa, cos, sa, sb, k_cache, v_cache, e_mat, sink_l, l):
    win = SWA_BLOCK
    per = SWA_TILE_BLOCKS
    tq = per * win
    n_tiles = geom.lat_len // tq
    nblk = geom.lat_len // win
    t0, b0 = geom.n_ctx // tq, geom.n_ctx // win
    past = k_cache.shape[2]
    prv = lambda i: jnp.maximum(per * i - 1, 0)
    nxt = lambda i: jnp.minimum(per * (i + 1), nblk - 1)
    tok = lambda w: [pl.BlockSpec((tq, w), lambda b, i: (t0 + b * n_tiles + i, 0)),
                     pl.BlockSpec((win, w), lambda b, i: (b0 + b * nblk + prv(i), 0)),
                     pl.BlockSpec((win, w), lambda b, i: (b0 + b * nblk + nxt(i), 0))]
    tab = lambda f, rows: [pl.BlockSpec((rows, LANES), lambda b, i: (f(i), 0))] * 3
    return pl.pallas_call(
        functools.partial(_swa_lat_kernel, n_tiles=n_tiles),
        out_shape=jax.ShapeDtypeStruct((geom.n_lat, GROUP_W), F32),
        grid=(geom.n_lat_seq, n_tiles),
        in_specs=tok(2 * GROUP_W) + tab(lambda i: i, tq) + tab(prv, win) + tab(nxt, win)
        + [pl.BlockSpec((1, None, past, LANES), lambda b, i: (b, l, 0, 0)),
           pl.BlockSpec((1, None, past, LANES), lambda b, i: (b, l, 0, 0)),
           pl.BlockSpec(e_mat.shape, lambda b, i: (0, 0)),
           _layer_spec(sink_l, l)],
        out_specs=pl.BlockSpec((tq, GROUP_W), lambda b, i: (b * n_tiles + i, 0)),
        compiler_params=_cparams(("parallel", "arbitrary")),
        name="swa_lat",
    )(swa, swa, swa, cos, sa, sb, cos, sa, sb, cos, sa, sb, k_cache, v_cache, e_mat, sink_l)


def _mix_ffn_kernel(*refs, geom, n_x, final):
    x_refs, refs = refs[:n_x], refs[n_x:]
    (ada_ref, dof_ref, dob_ref, dz_ref, syf_ref, syb_ref, sz_ref, omc_ref, oml_ref, osc_ref, osl_ref,
     dnw_ref, snw_ref, wo_ref, nw_ref, wgu_ref, wd_ref, fw_ref), out_refs = refs[:18], refs[18:]
    is_ctx = geom.is_ctx_tile(pl.program_id(0))
    x = x_refs[0][...] if n_x == 1 else _read_split(*x_refs, is_ctx)

    row = lax.broadcasted_iota(jnp.int32, (GROUP_W, GROUP_W), 0)
    col = lax.broadcasted_iota(jnp.int32, (GROUP_W, GROUP_W), 1)
    gones = _ones_where((row // HEAD_DIM) == (col // HEAD_DIM), BF16)
    o = dof_ref[...] + dob_ref[...]
    ms = _mm(o * o, gones) * (1.0 / HEAD_DIM)
    dn = o * lax.rsqrt(ms + EPS) * dnw_ref[...] * _silu(dz_ref[...])
    acc = _mm(dn, wo_ref[0:GROUP_W, :])
    acc = acc + _mm(_read_split(omc_ref, oml_ref, is_ctx), wo_ref[GROUP_W:2 * GROUP_W, :])
    y = (syf_ref[...] + syb_ref[...]) * _silu(sz_ref[...])
    for g in range(2):
        sl = slice(g * LANES, (g + 1) * LANES)
        acc = acc + _mm(_rms(y[:, sl], snw_ref[:, sl]),
                        wo_ref[2 * GROUP_W + g * LANES:2 * GROUP_W + (g + 1) * LANES, :])
    acc = acc + _mm(_read_split(osc_ref, osl_ref, is_ctx), wo_ref[3 * GROUP_W:, :])
    x = x + ada_ref[0, 2:3, :] * acc

    h = (_rms(x, nw_ref[...]) * (1.0 + ada_ref[0, 4:5, :]) + ada_ref[0, 3:4, :]).astype(BF16)
    acts = []
    for c in range(FF_DIM // FF_CHUNK):
        g = jnp.dot(h, wgu_ref[:, c * FF_CHUNK:(c + 1) * FF_CHUNK], preferred_element_type=F32)
        u = jnp.dot(h, wgu_ref[:, FF_DIM + c * FF_CHUNK:FF_DIM + (c + 1) * FF_CHUNK],
                    preferred_element_type=F32)
        acts.append((_silu(g) * u).astype(BF16))
    acc = jnp.dot(jnp.concatenate(acts, axis=1), wd_ref[...], preferred_element_type=F32)
    y = x + ada_ref[0, 5:6, :] * acc
    if not final:
        out_refs[0][...] = y
        return
    y = _rms(y, fw_ref[...])

    @pl.when(is_ctx)
    def _():
        out_refs[0][...] = y

    @pl.when(jnp.logical_not(is_ctx))
    def _():
        out_refs[1][...] = y


def _mix_ffn(geom, xs, ada, parts, dnw, snw, w_out, norm_w, wgu, w_down, final_w, l, final):
    tm = ROW_TILE
    d = xs[0].shape[1]
    tok = lambda w: pl.BlockSpec((tm, w), lambda i: (i, 0))
    resident = lambda a: _layer_spec(a, l, pipeline_mode=pl.Buffered(1))
    x_specs = [tok(d)] if len(xs) == 1 else geom.split_specs(d)
    if final:
        out_shape = [jax.ShapeDtypeStruct((geom.n_ctx, d), F32), jax.ShapeDtypeStruct((geom.n_lat, d), F32)]
        out_specs = geom.split_specs(d)
    else:
        out_shape = [jax.ShapeDtypeStruct((geom.n_tok, d), F32)]
        out_specs = [tok(d)]
    return pl.pallas_call(
        functools.partial(_mix_ffn_kernel, geom=geom, n_x=len(xs), final=final),
        out_shape=out_shape,
        grid=(geom.n_tok // tm,),
        in_specs=x_specs + [_ada_spec(geom, ada, l)]
        + [tok(GROUP_W)] * 6 + geom.split_specs(GROUP_W) + geom.split_specs(GROUP_W)
        + [resident(dnw), resident(snw), resident(w_out), resident(norm_w), resident(wgu),
           resident(w_down), pl.BlockSpec(final_w.shape, lambda i: (0, 0))],
        out_specs=out_specs,
        compiler_params=_cparams(("arbitrary",)),
        name="mix_ffn",
    )(*xs, ada, *parts, dnw, snw, w_out, norm_w, wgu, w_down, final_w)


def _w_in_layout(w):
    dn, mla, ssm, swa = 0, 1040, 1456, 2232
    zeros = lambda n: jnp.zeros(w.shape[:-1] + (n,), w.dtype)
    small = jnp.concatenate([
        w[..., dn + 1024:dn + 1040],
        w[..., ssm + 768:ssm + 776],
        zeros(SM_KPE - 24),
        w[..., mla + 384:mla + 416],
        zeros(LANES - SM_KPE - MLA_ROPE)], axis=-1)
    return jnp.concatenate([
        w[..., dn:dn + 768], w[..., dn + 768:dn + 1024],
        w[..., mla:mla + 256], w[..., mla + 256:mla + 384],
        w[..., ssm:ssm + 256], w[..., ssm + 256:ssm + 768],
        w[..., swa:swa + 512], small], axis=-1).astype(BF16)


def _gate_rows(dn_vec, ssm_vec):
    depth = dn_vec.shape[0]
    rows = jnp.zeros((depth, 1, LANES), F32)
    rows = rows.at[:, 0, SM_ALPHA:SM_ALPHA + 8].set(dn_vec.reshape(depth, 8))
    return rows.at[:, 0, SM_DT:SM_DT + 8].set(ssm_vec.reshape(depth, 8))


def _axial_angles(rows, rot_dim):
    row_ids = jnp.broadcast_to(jnp.arange(rows)[:, None], (rows, GRID_W)).reshape(-1).astype(F32)
    col_ids = jnp.broadcast_to(jnp.arange(GRID_W)[None, :], (rows, GRID_W)).reshape(-1).astype(F32)
    n_freq = rot_dim // 4
    inv_freq = ROPE_THETA ** (-jnp.arange(n_freq, dtype=F32) / n_freq)
    return jnp.concatenate([row_ids[:, None] * inv_freq, col_ids[:, None] * inv_freq], axis=-1)


def _rope_tables(ang, lane0, reps, n_ident):
    n, half = ang.shape
    cos, sin = jnp.cos(ang), jnp.sin(ang)
    zeros = jnp.zeros_like(sin)
    period = LANES // reps

    def table(first, second, fill):
        one = jnp.concatenate([jnp.full((n, lane0), fill, F32), first, second,
                               jnp.full((n, period - lane0 - 2 * half), fill, F32)], axis=1)
        tab = jnp.concatenate([one] * reps, axis=1)
        ident = jnp.full((n_ident, LANES), fill, F32)
        return jnp.concatenate([ident, tab], axis=0)

    return table(cos, cos, 1.0), table(-sin, zeros, 0.0), table(zeros, sin, 0.0)


def kernel(x_prompt, x_sample, c, state_dn, cache_mla_ckv, cache_mla_kpe, state_ssm, cache_swa_k,
           cache_swa_v, c_ctx, norm1_w, norm2_w, w_ada, b_ada, w_in, w_out, dn_conv_w, dn_a_log,
           dn_dt_bias, dn_norm_w, mla_q_norm_w, mla_w_uq, mla_kv_norm_w, mla_w_ukv, ssm_conv_w,
           ssm_conv_b, ssm_a_log, ssm_dt_bias, ssm_d, ssm_norm_w, swa_sinks, w_gate_up, w_down,
           final_norm_w):
    batch, seq, d = x_prompt.shape
    dec_batch, dec_seq, _ = x_sample.shape
    depth = w_in.shape[0]
    geom = _Geom(batch, seq, dec_batch, dec_seq)
    n_ctx = geom.n_ctx

    xs = (x_prompt.reshape(n_ctx, d), x_sample.reshape(geom.n_lat, d))
    n_mod = -(-(1 + dec_batch) // SUBLANES) * SUBLANES
    cc = jnp.concatenate([c_ctx[None], c, jnp.zeros((n_mod - 1 - dec_batch, d), F32)], axis=0)
    ada = _ada(cc, w_ada, b_ada).reshape(depth, n_mod, 6, d)

    mla_tabs = _rope_tables(_axial_angles(dec_seq // GRID_W, MLA_ROPE), MLA_NOPE, 1, ROW_TILE)
    swa_tabs = _rope_tables(_axial_angles(dec_seq // GRID_W, HEAD_DIM), 0, 2, 0)

    lane = jnp.arange(GROUP_W)
    e_mat = ((lane // LANES) * HEAD_DIM + lane % HEAD_DIM)[:, None] == jnp.arange(LANES)[None, :]
    e_mat = e_mat.astype(BF16)
    eye_h = jnp.eye(N_HEADS, dtype=F32)
    grp_h = (jnp.arange(N_HEADS)[:, None] // 2 == jnp.arange(2)[None, :]).astype(F32)
    row = lambda a: a.reshape(depth, 1, -1)
    w_pad = _w_in_layout(w_in)
    alog_rows = _gate_rows(dn_a_log, ssm_a_log)
    bias_rows = _gate_rows(dn_dt_bias, ssm_dt_bias)
    s0_dn = state_dn[:, :, :, :, :, None, :] * eye_h[None, None, None, :, None, :, None]
    s0_dn = s0_dn.reshape(dec_batch, depth, 2, GROUP_W, GROUP_W)
    s0_ssm = state_ssm[:, :, :, :, :, None, :] * grp_h[None, None, None, :, None, :, None]
    s0_ssm = s0_ssm.reshape(dec_batch, depth, 2, GROUP_W, 2 * SSM_N)
    ssm_dvec = row(jnp.repeat(ssm_d, HEAD_DIM, axis=-1))
    uq = mla_w_uq.reshape(depth, MLA_Q_LORA, N_HEADS, MLA_NOPE + MLA_ROPE)
    wuq = jnp.pad(uq, ((0, 0), (0, 0), (0, 0), (0, LANES - MLA_NOPE - MLA_ROPE)))
    wuq = wuq.reshape(depth, MLA_Q_LORA, N_HEADS * LANES).astype(BF16)
    ukv = mla_w_ukv.reshape(depth, MLA_KV_LORA, N_HEADS, MLA_NOPE + HEAD_DIM)
    wk = jnp.pad(ukv[..., :MLA_NOPE], ((0, 0), (0, 0), (0, 0), (0, LANES - MLA_NOPE)))
    wk = wk.reshape(depth, MLA_KV_LORA, N_HEADS * LANES).astype(BF16)
    wv = ukv[..., MLA_NOPE:].reshape(depth, MLA_KV_LORA, GROUP_W).astype(BF16)
    kpe_c = jnp.pad(cache_mla_kpe, ((0, 0), (0, 0), (0, 0), (MLA_NOPE, LANES - MLA_NOPE - MLA_ROPE)))
    past = cache_swa_k.shape[2]
    swa_kc = cache_swa_k.reshape(dec_batch, depth, past, LANES)
    swa_vc = cache_swa_v.reshape(dec_batch, depth, past, LANES)
    sinks = row(jnp.repeat(swa_sinks, HEAD_DIM, axis=-1))
    dnw = row(jnp.tile(dn_norm_w, (1, N_HEADS)))
    w_out_b, wgu_b, w_down_b = w_out.astype(BF16), w_gate_up.astype(BF16), w_down.astype(BF16)

    st_dn, st_ckv, st_kpe, st_ssm, st_k, st_v = [], [], [], [], [], []
    for l in range(depth):
        segs = _inproj(geom, xs, ada, row(norm1_w), w_pad, l)
        dn_qkv, dn_z, mla_q, mla_kv, ssm_z, ssm_xbc, swa, small = segs

        dn_of, dn_ob, dn_fin = _deltanet(geom, dn_qkv, small, s0_dn, dn_conv_w, alog_rows, bias_rows, l)
        ssm_yf, ssm_yb, ssm_fin = _ssd(geom, ssm_xbc, small, s0_ssm, ssm_conv_w, row(ssm_conv_b),
                                       alog_rows, bias_rows, ssm_dvec, l)
        qh, kh, vh, ckv = _mla_proj(geom, mla_q, mla_kv, small, *mla_tabs, row(mla_q_norm_w),
                                    row(mla_kv_norm_w), wuq, wk, wv, l)
        o_mla_ctx = _mla_attn(qh, kh, vh, batch, seq, 0)
        o_mla_lat = _mla_attn(qh, kh, vh, dec_batch, dec_seq, n_ctx,
                              cache=(cache_mla_ckv, kpe_c, wk, wv), l=l)
        o_swa_ctx = _swa_ctx(geom, swa, e_mat, sinks, l)
        o_swa_lat = _swa_lat(geom, swa, *swa_tabs, swa_kc, swa_vc, e_mat, sinks, l)

        st_dn.append(dn_fin)
        st_ssm.append(ssm_fin)
        st_ckv.append(ckv.reshape(batch, seq, MLA_KV_LORA))
        st_kpe.append(small[:n_ctx, SM_KPE:SM_KPE + MLA_ROPE].reshape(batch, seq, MLA_ROPE))
        st_k.append(swa[:n_ctx, GROUP_W:GROUP_W + LANES].reshape(batch, seq, 2, HEAD_DIM))
        st_v.append(swa[:n_ctx, GROUP_W + LANES:].reshape(batch, seq, 2, HEAD_DIM))

        parts = (dn_of, dn_ob, dn_z, ssm_yf, ssm_yb, ssm_z, o_mla_ctx, o_mla_lat, o_swa_ctx, o_swa_lat)
        xs = _mix_ffn(geom, xs, ada, parts, dnw, row(ssm_norm_w), w_out_b, row(norm2_w), wgu_b, w_down_b,
                      final_norm_w.reshape(1, d), l, l == depth - 1)

    return (xs[0].reshape(batch, seq, d), xs[1].reshape(dec_batch, dec_seq, d),
            jnp.stack(st_dn, axis=1), jnp.stack(st_ckv, axis=1), jnp.stack(st_kpe, axis=1),
            jnp.stack(st_ssm, axis=1), jnp.stack(st_k, axis=1), jnp.stack(st_v, axis=1))
```

```python
import functools

import jax
import jax.numpy as jnp
from jax import lax
from jax.experimental import pallas as pl
from jax.experimental.pallas import tpu as pltpu

F32 = jnp.float32
BF16 = jnp.bfloat16

D_MODEL = 1024
GRID_W = 64
HEAD_DIM = 64
GROUP_W = 256
EPS = 1e-6
ROPE_THETA = 10000.0
NEG_INF = -1e30
N_HEADS = 4
MLA_NOPE = 64
MLA_ROPE = 32
MLA_Q_LORA = 256
MLA_KV_LORA = 128
MLA_SCALE = (MLA_NOPE + MLA_ROPE) ** -0.5
SSM_N = 64
SWA_SCALE = HEAD_DIM ** -0.5
LOG2_E = 1.4426950408889634
SWA_BLOCK = 128
SWA_TILE_BLOCKS = 4
FF_DIM = 2816
FF_CHUNK = 256
KEY_CHUNKS = 4

LANES = 128
SUBLANES = 8
BF16_ROWS = 16
SEQ_TILE = 256
ROW_TILE = 512
VMEM_LIMIT = 56 * 1024 * 1024

SEG_WIDTHS = (768, 256, 256, 128, 256, 512, 512, 128)
IN_PAD = sum(SEG_WIDTHS)
SM_BETA, SM_ALPHA, SM_DT, SM_KPE = 0, 8, 16, 64


def _sigmoid(x):
    return 1.0 / (1.0 + jnp.exp(-x))


def _silu(x):
    half = 0.5 * x
    return half + half * jnp.tanh(half)


def _softplus(x):
    return jnp.maximum(x, 0.0) + jnp.log1p(jnp.exp(-jnp.abs(x)))


def _mm(a, b):
    return jnp.dot(a.astype(BF16), b.astype(BF16), preferred_element_type=F32)


def _mm_nt(a, b):
    return lax.dot_general(a.astype(BF16), b.astype(BF16), (((1,), (1,)), ((), ())),
                           preferred_element_type=F32)


def _mm_tn(a, b):
    return lax.dot_general(a.astype(BF16), b.astype(BF16), (((0,), (0,)), ((), ())),
                           preferred_element_type=F32)


def _split(a, parts):
    out = []
    for _ in range(parts):
        hi = a.astype(BF16)
        out.append(hi)
        a = a - hi.astype(F32)
    return out


def _mm_split_lhs(a, b_exact, parts, nt=False):
    dims = (((1,), (1,)), ((), ())) if nt else (((1,), (0,)), ((), ()))
    acc = None
    for piece in _split(a, parts):
        r = lax.dot_general(piece, b_exact, dims, preferred_element_type=F32)
        acc = r if acc is None else acc + r
    return acc


def _mm_split_rhs(a_exact, b, parts):
    acc = None
    for piece in _split(b, parts):
        r = jnp.dot(a_exact, piece, preferred_element_type=F32)
        acc = r if acc is None else acc + r
    return acc


def _ones_where(mask, dtype):
    return jnp.where(mask, 1.0, 0.0).astype(dtype)


def _rms(x, w):
    return x * lax.rsqrt(jnp.mean(x * x, axis=-1, keepdims=True) + EPS) * w


def _cparams(sem):
    return pltpu.CompilerParams(dimension_semantics=sem, vmem_limit_bytes=VMEM_LIMIT)


def _ada_kernel(c_ref, w_ref, b_ref, o_ref):
    o_ref[0] = _mm(_silu(c_ref[...]), w_ref[0]) + b_ref[0]


def _ada(cc, w_ada, b_ada):
    depth, d, n = w_ada.shape
    tn = 1536
    return pl.pallas_call(
        _ada_kernel,
        out_shape=jax.ShapeDtypeStruct((depth, cc.shape[0], n), F32),
        grid=(depth, n // tn),
        in_specs=[pl.BlockSpec(cc.shape, lambda l, j: (0, 0)),
                  pl.BlockSpec((1, d, tn), lambda l, j: (l, 0, j)),
                  pl.BlockSpec((1, 1, tn), lambda l, j: (l, 0, j))],
        out_specs=pl.BlockSpec((1, cc.shape[0], tn), lambda l, j: (l, 0, j)),
        compiler_params=_cparams(("arbitrary", "arbitrary")),
        name="ada",
    )(cc, w_ada, b_ada.reshape(depth, 1, n))


class _Geom:
    def __init__(self, n_ctx_seq, ctx_len, n_lat_seq, lat_len):
        self.n_ctx_seq, self.ctx_len = n_ctx_seq, ctx_len
        self.n_lat_seq, self.lat_len = n_lat_seq, lat_len
        self.n_ctx = n_ctx_seq * ctx_len
        self.n_lat = n_lat_seq * lat_len
        self.n_tok = self.n_ctx + self.n_lat
        assert ctx_len % SEQ_TILE == 0 and lat_len % ROW_TILE == 0 and self.n_ctx % ROW_TILE == 0
        self.cps = ctx_len // SEQ_TILE
        self.lps = lat_len // SEQ_TILE
        self.n_ctx_tiles = n_ctx_seq * self.cps
        self.n_tiles = self.n_ctx_tiles + n_lat_seq * self.lps
        self.n_seq = n_ctx_seq + n_lat_seq

    def mod_row(self, i):
        r = i * ROW_TILE
        return jnp.where(r >= self.n_ctx, 1 + (r - self.n_ctx) // self.lat_len, 0)

    def tile_pos(self, t):
        is_lat = t >= self.n_ctx_tiles
        u = t - self.n_ctx_tiles
        seq = jnp.where(is_lat, self.n_ctx_seq + u // self.lps, t // self.cps)
        pos = jnp.where(is_lat, u % self.lps, t % self.cps)
        nt = jnp.where(is_lat, self.lps, self.cps)
        return seq, pos, nt

    def mirror(self, t):
        _, pos, nt = self.tile_pos(t)
        return t - pos + (nt - 1 - pos)

    def split_specs(self, width):
        nc = self.n_ctx // ROW_TILE
        return [pl.BlockSpec((ROW_TILE, width), lambda i: (jnp.minimum(i, nc - 1), 0)),
                pl.BlockSpec((ROW_TILE, width), lambda i: (jnp.maximum(i - nc, 0), 0))]

    def is_ctx_tile(self, i):
        return i < self.n_ctx // ROW_TILE


def _layer_spec(a, l, **kw):
    return pl.BlockSpec((None,) + a.shape[1:], lambda *_: (l,) + (0,) * (a.ndim - 1), **kw)


def _ada_spec(geom, ada, l):
    return pl.BlockSpec((None, 1) + ada.shape[2:], lambda i: (l, geom.mod_row(i), 0, 0))


def _read_split(ctx_ref, lat_ref, is_ctx):
    return jnp.where(is_ctx, ctx_ref[...], lat_ref[...])


def _inproj_kernel(*refs, geom, n_x):
    x_refs, (ada_ref, nw_ref, w_ref), out_refs = refs[:n_x], refs[n_x:n_x + 3], refs[n_x + 3:]
    x = x_refs[0][...] if n_x == 1 else _read_split(*x_refs, geom.is_ctx_tile(pl.program_id(0)))
    h = _rms(x, nw_ref[...]) * (1.0 + ada_ref[0, 1:2, :]) + ada_ref[0, 0:1, :]
    h = h.astype(BF16)
    off = 0
    for o_ref in out_refs:
        wd = o_ref.shape[-1]
        o_ref[...] = jnp.dot(h, w_ref[:, off:off + wd], preferred_element_type=F32)
        off += wd


def _inproj(geom, xs, ada, norm_w, w_pad, l):
    d = xs[0].shape[1]
    x_specs = [pl.BlockSpec((ROW_TILE, d), lambda i: (i, 0))] if len(xs) == 1 else geom.split_specs(d)
    return pl.pallas_call(
        functools.partial(_inproj_kernel, geom=geom, n_x=len(xs)),
        out_shape=[jax.ShapeDtypeStruct((geom.n_tok, wd), F32) for wd in SEG_WIDTHS],
        grid=(geom.n_tok // ROW_TILE,),
        in_specs=x_specs + [_ada_spec(geom, ada, l), _layer_spec(norm_w, l), _layer_spec(w_pad, l)],
        out_specs=[pl.BlockSpec((ROW_TILE, wd), lambda i: (i, 0)) for wd in SEG_WIDTHS],
        compiler_params=_cparams(("arbitrary",)),
        name="inproj",
    )(*xs, ada, norm_w, w_pad)


def _conv3(x, x_prev, x_next, w_ref):
    n, s = x.shape[0], SUBLANES
    r = lax.broadcasted_iota(jnp.int32, (s, x.shape[1]), 0)
    x_dn, x_up = pltpu.roll(x, 1, 0), pltpu.roll(x, n - 1, 0)
    x_dn = jnp.concatenate([jnp.where(r == 0, x_prev, x_dn[:s]), x_dn[s:]], axis=0)
    x_up = jnp.concatenate([x_up[:n - s], jnp.where(r == s - 1, x_next, x_up[n - s:])], axis=0)
    return w_ref[0:1, :] * x_dn + w_ref[1:2, :] * x + w_ref[2:3, :] * x_up


def _head_lanes(head_l, colfn):
    out = colfn(N_HEADS - 1)
    for h in range(N_HEADS - 2, -1, -1):
        out = jnp.where(head_l == h, colfn(h), out)
    return out


def _cumulative(z, incl_b, parts=2):
    cum = _mm_split_rhs(incl_b, z, parts)
    cumt = _mm_split_lhs(z.T, incl_b, parts, nt=True)
    return cum, cumt


def _unit_tri_inverses(a_list, lower, lvl, eye, out):
    dot = functools.partial(jnp.dot, preferred_element_type=F32)
    a0 = [jnp.where(lvl < 3, a, 0.0) for a in a_list]
    a0b = [a.astype(BF16) for a in a0]
    x = [eye - a for a in a0]
    pb = [dot(a, a).astype(BF16) for a in a0b]
    yield
    x = [xi + dot(xi.astype(BF16), pi) for xi, pi in zip(x, pb)]
    yield
    pb = [dot(pi, pi).astype(BF16) for pi in pb]
    yield
    xb = [(xi + dot(xi.astype(BF16), pi)).astype(BF16) for xi, pi in zip(x, pb)]
    yield
    ab = [a.astype(BF16) for a in a_list]
    n = a_list[0].shape[0]
    for m in range(3, n.bit_length() - 1):
        mask = _ones_where(lvl == m, BF16)
        s = 1 << m
        dt = F32 if s < BF16_ROWS else BF16
        blocks = range(0, n, 2 * s)
        rows = {True: [slice(b + s, b + 2 * s) for b in blocks], False: [slice(b, b + s) for b in blocks]}
        zeros = jnp.zeros((s, n), dt)

        def take(v, low):
            v = v.astype(dt)
            return jnp.concatenate([v[sl] for sl in rows[low]], axis=0).astype(BF16)

        def spread(vc, low):
            vc = vc.astype(dt)
            pieces = []
            for j in range(len(blocks)):
                piece = vc[j * s:(j + 1) * s]
                pieces += [zeros, piece] if low else [piece, zeros]
            return jnp.concatenate(pieces, axis=0).astype(BF16)

        masks = {low: take(mask, low) for low in (True, False)}
        y = [spread(dot(take(ai, low) * masks[low], xi), low) for ai, xi, low in zip(ab, xb, lower)]
        yield
        xb = [xi - spread(dot(take(xi, low), yi), low) for xi, yi, low in zip(xb, y, lower)]
        yield
    out.extend(xb)


def _interleave(*gens):
    live = list(gens)
    while live:
        for g in list(live):
            if next(g, StopIteration) is StopIteration:
                live.remove(g)


def _seq_tile_specs(geom, width, mirror):
    rows8 = SEQ_TILE // SUBLANES
    last8 = geom.n_tok // SUBLANES - 1
    tile = (lambda t: geom.mirror(t)) if mirror else (lambda t: t)
    return [pl.BlockSpec((SEQ_TILE, width), lambda t: (tile(t), 0)),
            pl.BlockSpec((SUBLANES, width), lambda t: (jnp.maximum(tile(t) * rows8 - 1, 0), 0)),
            pl.BlockSpec((SUBLANES, width), lambda t: (jnp.minimum((tile(t) + 1) * rows8, last8), 0))]


def _dn_kernel(xf_ref, xfp_ref, xfn_ref, xb_ref, xbp_ref, xbn_ref, gf_ref, gb_ref, s0_ref,
               cw_ref, alog_ref, bias_ref, of_ref, ob_ref, sfin_ref, s_scr, *, geom):
    t = pl.program_id(0)
    seq, pos, nt = geom.tile_pos(t)
    is_ctx = seq < geom.n_ctx_seq
    tt = SEQ_TILE

    @pl.when((pos == 0) & is_ctx)
    def _():
        s_scr[...] = jnp.zeros(s_scr.shape, F32)

    @pl.when((pos == 0) & jnp.logical_not(is_ctx))
    def _():
        s_scr[...] = s0_ref[0]

    row = lax.broadcasted_iota(jnp.int32, (tt, tt), 0)
    col = lax.broadcasted_iota(jnp.int32, (tt, tt), 1)
    xr = row ^ col
    lvl = jnp.where(xr >= 2, 1, 0)
    for kbit in range(2, tt.bit_length() - 1):
        lvl = lvl + jnp.where(xr >= (1 << kbit), 1, 0)
    head_l = lax.broadcasted_iota(jnp.int32, (tt, GROUP_W), 1) // HEAD_DIM
    blockdiag = (lax.broadcasted_iota(jnp.int32, (GROUP_W, GROUP_W), 0) // HEAD_DIM
                 == lax.broadcasted_iota(jnp.int32, (GROUP_W, GROUP_W), 1) // HEAD_DIM)
    eye = _ones_where(row == col, F32)
    gones = _ones_where(blockdiag, BF16)
    head_mask = [_ones_where(head_l == h, BF16) for h in range(N_HEADS)]
    lane_g = lax.broadcasted_iota(jnp.int32, (tt, LANES), 1)
    not_first, not_last = pos > 0, pos < nt - 1

    dirs = ((xf_ref, xfp_ref, xfn_ref, gf_ref, of_ref, not_first, not_last),
            (xb_ref, xbp_ref, xbn_ref, gb_ref, ob_ref, not_last, not_first))
    a_lists, x_lists, pre_out = ([], []), ([], []), [None, None]

    def pre(d):
        x_ref, xp_ref, xn_ref, g_ref, o_ref, has_prev, has_next = dirs[d]
        x_prev = jnp.where(has_prev, xp_ref[SUBLANES - 1:SUBLANES, :], 0.0)
        x_next = jnp.where(has_next, xn_ref[0:1, :], 0.0)
        y = _silu(_conv3(x_ref[...], x_prev, x_next, cw_ref))
        yield
        q, k, v = y[:, :GROUP_W], y[:, GROUP_W:2 * GROUP_W], y[:, 2 * GROUP_W:]
        q = q * lax.rsqrt(_mm(q * q, gones) + EPS) * (HEAD_DIM ** -0.5)
        k = k * lax.rsqrt(_mm(k * k, gones) + EPS)
        yield

        s = g_ref[...]
        gate = -jnp.exp(alog_ref[...]) * _softplus(s + bias_ref[...])
        z = jnp.where(lane_g < SM_ALPHA, _sigmoid(s), gate)
        incl = (row >= col) if d == 0 else (row <= col)
        strict = (row > col) if d == 0 else (row < col)
        cum, cumt = _cumulative(z, _ones_where(incl, BF16))
        cum2, cumt2 = cum * LOG2_E, cumt * LOG2_E
        edge = tt - 1 if d == 0 else 0
        c_beta = SM_BETA + N_HEADS * d
        c_g = SM_ALPHA + N_HEADS * d
        yield

        beta_l = _head_lanes(head_l, lambda h: z[:, c_beta + h:c_beta + h + 1])
        cum_l = _head_lanes(head_l, lambda h: cum[:, c_g + h:c_g + h + 1])
        tot_l = _head_lanes(head_l[0:1, :], lambda h: cum[edge:edge + 1, c_g + h:c_g + h + 1])
        eg = jnp.exp(cum_l)
        kb = k * beta_l
        rhs = jnp.concatenate([v * beta_l, kb * eg], axis=1).astype(BF16)
        kbf, kbb, qb = k.astype(BF16), kb.astype(BF16), q.astype(BF16)
        yield
        qk_heads = []
        for h in range(N_HEADS):
            diff = cum2[:, c_g + h:c_g + h + 1] - cumt2[c_g + h:c_g + h + 1, :]
            decay = jnp.where(incl, jnp.exp2(diff), 0.0)
            a_lists[d].append(jnp.where(strict, _mm_nt(kbb * head_mask[h], kbf) * decay, 0.0))
            qk_heads.append((_mm_nt(qb * head_mask[h], kbf) * decay).astype(BF16))
            yield
        pre_out[d] = (rhs, qk_heads, (q * eg).astype(BF16), k * jnp.exp(tot_l - cum_l),
                      jnp.exp(tot_l), o_ref)

    def tail(d):
        rhs, qk_heads, qg, kd, e_tot, o_ref = pre_out[d]
        half = tt // 2

        def tri_dot(tris, rights, dense=None):
            keep = slice(0, half) if d == 0 else slice(half, tt)
            full = slice(half, tt) if d == 0 else slice(0, half)
            lead_l = [dense[0]] if dense else []
            lead_r = [dense[1]] if dense else []
            short = jnp.dot(jnp.concatenate([a[keep] for a in lead_l] + [a[keep, keep] for a in tris], axis=1),
                            jnp.concatenate(lead_r + [b[keep] for b in rights], axis=0),
                            preferred_element_type=F32)
            long = jnp.dot(jnp.concatenate([a[full] for a in lead_l + list(tris)], axis=1),
                           jnp.concatenate(lead_r + list(rights), axis=0), preferred_element_type=F32)
            return jnp.concatenate([short, long] if d == 0 else [long, short], axis=0)

        mask2 = [jnp.concatenate([m, m], axis=1) for m in head_mask]
        uw = tri_dot(x_lists[d], [rhs * m for m in mask2])
        yield
        state = s_scr[d]
        sb = state.astype(BF16)
        v_new = uw[:, :GROUP_W] - _mm(uw[:, GROUP_W:], sb)
        vb = v_new.astype(BF16)
        yield
        o_ref[...] = tri_dot(qk_heads, [vb * m for m in head_mask], dense=(qg, sb))
        yield
        s_scr[d] = state * e_tot + jnp.where(blockdiag, _mm_tn(kd, vb), 0.0)

    _interleave(pre(0), pre(1))
    x_all = []
    _interleave(_unit_tri_inverses(a_lists[0] + a_lists[1], [True] * N_HEADS + [False] * N_HEADS, lvl, eye,
                                   x_all))
    x_lists[0].extend(x_all[:N_HEADS])
    x_lists[1].extend(x_all[N_HEADS:])
    _interleave(tail(0), tail(1))

    @pl.when((pos == nt - 1) & is_ctx)
    def _():
        for d in range(2):
            for h in range(N_HEADS):
                sl = slice(h * HEAD_DIM, (h + 1) * HEAD_DIM)
                sfin_ref[0, d, h] = s_scr[d, sl, sl]


def _state_specs(geom, block, l):
    seq_of = lambda t: geom.tile_pos(t)[0]
    lat_seq = lambda t: (jnp.maximum(seq_of(t) - geom.n_ctx_seq, 0), l) + (0,) * len(block[0])
    ctx_seq = lambda t: (jnp.minimum(seq_of(t), geom.n_ctx_seq - 1),) + (0,) * len(block[1])
    return pl.BlockSpec((1, None) + block[0], lat_seq), pl.BlockSpec((1,) + block[1], ctx_seq)


def _deltanet(geom, qkv, small, s0, conv_w, alog_row, bias_row, l):
    tt = SEQ_TILE
    s0_spec, sfin_spec = _state_specs(geom, ((2, GROUP_W, GROUP_W), (2, N_HEADS, HEAD_DIM, HEAD_DIM)), l)
    in_specs = (_seq_tile_specs(geom, 3 * GROUP_W, False) + _seq_tile_specs(geom, 3 * GROUP_W, True)
                + [pl.BlockSpec((tt, LANES), lambda t: (t, 0)),
                   pl.BlockSpec((tt, LANES), lambda t: (geom.mirror(t), 0)),
                   s0_spec, _layer_spec(conv_w, l), _layer_spec(alog_row, l), _layer_spec(bias_row, l)])
    return pl.pallas_call(
        functools.partial(_dn_kernel, geom=geom),
        out_shape=[jax.ShapeDtypeStruct((geom.n_tok, GROUP_W), F32),
                   jax.ShapeDtypeStruct((geom.n_tok, GROUP_W), F32),
                   jax.ShapeDtypeStruct((geom.n_ctx_seq, 2, N_HEADS, HEAD_DIM, HEAD_DIM), F32)],
        grid=(geom.n_tiles,),
        in_specs=in_specs,
        out_specs=[pl.BlockSpec((tt, GROUP_W), lambda t: (t, 0)),
                   pl.BlockSpec((tt, GROUP_W), lambda t: (geom.mirror(t), 0)),
                   sfin_spec],
        scratch_shapes=[pltpu.VMEM((2, GROUP_W, GROUP_W), F32)],
        compiler_params=_cparams(("arbitrary",)),
        name="deltanet",
    )(qkv, qkv, qkv, qkv, qkv, qkv, small, small, s0, conv_w, alog_row, bias_row)


def _ssm_kernel(xf_ref, xfp_ref, xfn_ref, xb_ref, xbp_ref, xbn_ref, gf_ref, gb_ref, s0_ref,
                cw_ref, cb_ref, alog_ref, bias_ref, dvec_ref, yf_ref, yb_ref, sfin_ref, s_scr,
                *, geom):
    t = pl.program_id(0)
    seq, pos, nt = geom.tile_pos(t)
    is_ctx = seq < geom.n_ctx_seq
    tt = SEQ_TILE

    @pl.when((pos == 0) & is_ctx)
    def _():
        s_scr[...] = jnp.zeros(s_scr.shape, F32)

    @pl.when((pos == 0) & jnp.logical_not(is_ctx))
    def _():
        s_scr[...] = s0_ref[0]

    row = lax.broadcasted_iota(jnp.int32, (tt, tt), 0)
    col = lax.broadcasted_iota(jnp.int32, (tt, tt), 1)
    head_l = lax.broadcasted_iota(jnp.int32, (tt, GROUP_W), 1) // HEAD_DIM
    lane_g = lax.broadcasted_iota(jnp.int32, (tt, LANES), 1)
    group_l = lane_g // SSM_N
    state_head = lax.broadcasted_iota(jnp.int32, (GROUP_W, 2 * SSM_N), 0) // HEAD_DIM
    state_group = lax.broadcasted_iota(jnp.int32, (GROUP_W, 2 * SSM_N), 1) // SSM_N
    state_mask = state_head // 2 == state_group
    not_first, not_last = pos > 0, pos < nt - 1

    dirs = ((xf_ref, xfp_ref, xfn_ref, gf_ref, yf_ref, not_first, not_last),
            (xb_ref, xbp_ref, xbn_ref, gb_ref, yb_ref, not_last, not_first))
    for d, (x_ref, xp_ref, xn_ref, g_ref, y_ref, has_prev, has_next) in enumerate(dirs):
        x_prev = jnp.where(has_prev, xp_ref[SUBLANES - 1:SUBLANES, :], 0.0)
        x_next = jnp.where(has_next, xn_ref[0:1, :], 0.0)
        y = _silu(_conv3(x_ref[...], x_prev, x_next, cw_ref) + cb_ref[...])
        xs, bm, cm = y[:, :GROUP_W], y[:, GROUP_W:GROUP_W + 2 * SSM_N], y[:, GROUP_W + 2 * SSM_N:]

        dt = _softplus(g_ref[...] + bias_ref[...])
        a = -jnp.exp(alog_ref[...]) * dt
        incl = (row >= col) if d == 0 else (row <= col)
        cum, cumt = _cumulative(a, _ones_where(incl, BF16), parts=3)
        cum2, cumt2 = cum * LOG2_E, cumt * LOG2_E
        edge = tt - 1 if d == 0 else 0
        c0 = SM_DT + N_HEADS * d

        dt_l = _head_lanes(head_l, lambda h: dt[:, c0 + h:c0 + h + 1])
        cum_l = _head_lanes(head_l, lambda h: cum[:, c0 + h:c0 + h + 1])
        tot_l = _head_lanes(head_l[0:1, :], lambda h: cum[edge:edge + 1, c0 + h:c0 + h + 1])
        xdt = xs * dt_l
        cb_scores = [_mm_nt(jnp.where(group_l == g, cm, 0.0), bm) for g in range(2)]
        out = jnp.zeros((tt, GROUP_W), F32)
        for h in range(N_HEADS):
            diff = cum2[:, c0 + h:c0 + h + 1] - cumt2[c0 + h:c0 + h + 1, :]
            lmat = jnp.where(incl, jnp.exp2(diff), 0.0)
            out = jnp.where(head_l == h, _mm(cb_scores[h // 2] * lmat, xdt), out)
        state = s_scr[d]
        out = out + _mm_nt(cm, state) * jnp.exp(cum_l)
        if d == 0:
            out = out + dvec_ref[...] * xs
        y_ref[...] = out
        tot_rows = _head_lanes(state_head, lambda h: cum[edge:edge + 1, c0 + h:c0 + h + 1])
        s_scr[d] = (state * jnp.exp(tot_rows)
                    + jnp.where(state_mask, _mm_tn(xdt * jnp.exp(tot_l - cum_l), bm), 0.0))

    @pl.when((pos == nt - 1) & is_ctx)
    def _():
        for d in range(2):
            for h in range(N_HEADS):
                g = h // 2
                sfin_ref[0, d, h] = s_scr[d, h * HEAD_DIM:(h + 1) * HEAD_DIM, g * SSM_N:(g + 1) * SSM_N]


def _ssd(geom, xbc, small, s0, conv_w, conv_b, alog_row, bias_row, dvec, l):
    tt = SEQ_TILE
    wx = GROUP_W + 4 * SSM_N
    s0_spec, sfin_spec = _state_specs(geom, ((2, GROUP_W, 2 * SSM_N), (2, N_HEADS, HEAD_DIM, SSM_N)), l)
    in_specs = (_seq_tile_specs(geom, wx, False) + _seq_tile_specs(geom, wx, True)
                + [pl.BlockSpec((tt, LANES), lambda t: (t, 0)),
                   pl.BlockSpec((tt, LANES), lambda t: (geom.mirror(t), 0)),
                   s0_spec, _layer_spec(conv_w, l), _layer_spec(conv_b, l), _layer_spec(alog_row, l),
                   _layer_spec(bias_row, l), _layer_spec(dvec, l)])
    return pl.pallas_call(
        functools.partial(_ssm_kernel, geom=geom),
        out_shape=[jax.ShapeDtypeStruct((geom.n_tok, GROUP_W), F32),
                   jax.ShapeDtypeStruct((geom.n_tok, GROUP_W), F32),
                   jax.ShapeDtypeStruct((geom.n_ctx_seq, 2, N_HEADS, HEAD_DIM, SSM_N), F32)],
        grid=(geom.n_tiles,),
        in_specs=in_specs,
        out_specs=[pl.BlockSpec((tt, GROUP_W), lambda t: (t, 0)),
                   pl.BlockSpec((tt, GROUP_W), lambda t: (geom.mirror(t), 0)),
                   sfin_spec],
        scratch_shapes=[pltpu.VMEM((2, GROUP_W, 2 * SSM_N), F32)],
        compiler_params=_cparams(("arbitrary",)),
        name="ssd",
    )(xbc, xbc, xbc, xbc, xbc, xbc, small, small, s0, conv_w, conv_b, alog_row, bias_row, dvec)


def _rope_slab(x, cos, sin_a, sin_b, half):
    w = x.shape[-1]
    return x * cos + pltpu.roll(x, w - half, 1) * sin_a + pltpu.roll(x, half, 1) * sin_b


def _mla_proj_kernel(ql_ref, kvl_ref, sm_ref, cos_ref, sa_ref, sb_ref, qnw_ref, kvnw_ref,
                     wuq_ref, wk_ref, wv_ref, qh_ref, kh_ref, vh_ref, ckv_ref, *, geom):
    cos, sa, sb = cos_ref[...], sa_ref[...], sb_ref[...]
    half = MLA_ROPE // 2
    qp = _mm(_rms(ql_ref[...], qnw_ref[...]), wuq_ref[...]) * (MLA_SCALE * LOG2_E)
    ckv = _rms(kvl_ref[...], kvnw_ref[...])

    @pl.when(geom.is_ctx_tile(pl.program_id(0)))
    def _():
        ckv_ref[...] = ckv

    lane = lax.broadcasted_iota(jnp.int32, cos.shape, 1)
    is_pe = (lane >= MLA_NOPE) & (lane < MLA_NOPE + MLA_ROPE)
    kpe = jnp.where(is_pe, _rope_slab(sm_ref[...], cos, sa, sb, half), 0.0)
    kp = _mm(ckv, wk_ref[...])
    for h in range(N_HEADS):
        sl = slice(h * LANES, (h + 1) * LANES)
        qh_ref[:, sl] = _rope_slab(qp[:, sl], cos, sa, sb, half).astype(BF16)
        kh_ref[:, sl] = (kp[:, sl] + kpe).astype(BF16)
    vh_ref[...] = _mm(ckv, wv_ref[...]).astype(BF16)


def _mla_proj(geom, q_lat, kv_lat, small, cos, sa, sb, qnw, kvnw, wuq, wk, wv, l):
    tm = ROW_TILE
    tok = lambda w: pl.BlockSpec((tm, w), lambda i: (i, 0))
    full = lambda a: _layer_spec(a, l)

    def tab_block(i):
        r = i * tm
        return jnp.where(r >= geom.n_ctx, 1 + ((r - geom.n_ctx) % geom.lat_len) // tm, 0), 0

    tab = pl.BlockSpec((tm, LANES), tab_block)
    return pl.pallas_call(
        functools.partial(_mla_proj_kernel, geom=geom),
        out_shape=[jax.ShapeDtypeStruct((geom.n_tok, N_HEADS * LANES), BF16),
                   jax.ShapeDtypeStruct((geom.n_tok, N_HEADS * LANES), BF16),
                   jax.ShapeDtypeStruct((geom.n_tok, GROUP_W), BF16),
                   jax.ShapeDtypeStruct((geom.n_ctx, MLA_KV_LORA), F32)],
        grid=(geom.n_tok // tm,),
        in_specs=[tok(MLA_Q_LORA), tok(MLA_KV_LORA), tok(LANES), tab, tab, tab,
                  full(qnw), full(kvnw), full(wuq), full(wk), full(wv)],
        out_specs=[tok(N_HEADS * LANES), tok(N_HEADS * LANES), tok(GROUP_W),
                   geom.split_specs(MLA_KV_LORA)[0]],
        compiler_params=_cparams(("arbitrary",)),
        name="mla_proj",
    )(q_lat, kv_lat, small, cos, sa, sb, qnw, kvnw, wuq, wk, wv)


def _mla_attn_kernel(*refs, has_cache):
    if has_cache:
        q_ref, k_ref, v_ref, ckv_ref, kpe_ref, wk_ref, wv_ref, o_ref = refs
        ckv_c = ckv_ref[0]
        v_c = _mm(ckv_c, wv_ref[...])
    else:
        q_ref, k_ref, v_ref, o_ref = refs
    n_heads = q_ref.shape[1] // LANES
    heads = [slice(j * LANES, (j + 1) * LANES) for j in range(n_heads)]
    pair = lambda j: slice((j // 2) * LANES, (j // 2 + 1) * LANES)
    n_keys = k_ref.shape[0]
    chunk = n_keys // KEY_CHUNKS if n_keys % (KEY_CHUNKS * LANES) == 0 else n_keys
    def own_lanes(v, j):
        v = v[:, pair(j)]
        lane = lax.broadcasted_iota(jnp.int32, v.shape, 1)
        return jnp.where((lane >= HEAD_DIM) == (j % 2 == 1), v, 1.0).astype(BF16)

    qs = [q_ref[:, sl] for sl in heads]
    if has_cache:
        scores = [_mm_nt(q, _mm(ckv_c, wk_ref[:, sl]) + kpe_ref[0]) for q, sl in zip(qs, heads)]
        ms = [jnp.max(s, axis=-1, keepdims=True) for s in scores]
        accs = [_mm(jnp.exp2(s - m), own_lanes(v_c, j)) for j, (s, m) in enumerate(zip(scores, ms))]
    for c in range(n_keys // chunk):
        rows = slice(c * chunk, (c + 1) * chunk)
        scores = [_mm_nt(q, k_ref[rows, sl]) for q, sl in zip(qs, heads)]
        v = v_ref[rows, :]
        if c == 0 and not has_cache:
            ms = [jnp.max(s, axis=-1, keepdims=True) for s in scores]
            accs = [_mm(jnp.exp2(s - m), own_lanes(v, j)) for j, (s, m) in enumerate(zip(scores, ms))]
            continue
        for j, s in enumerate(scores):
            m_new = jnp.maximum(ms[j], jnp.max(s, axis=-1, keepdims=True))
            accs[j] = accs[j] * jnp.exp2(ms[j] - m_new) + _mm(jnp.exp2(s - m_new), own_lanes(v, j))
            ms[j] = m_new
    outs = [acc / pltpu.roll(acc, HEAD_DIM, 1) for acc in accs]
    lane = lax.broadcasted_iota(jnp.int32, outs[0].shape, 1)
    for j in range(0, n_heads, 2):
        o_ref[:, pair(j)] = jnp.where(lane < HEAD_DIM, outs[j], outs[j + 1])


def _mla_attn(qh, kh, vh, n_seq, seq_len, tok0, cache=None, l=0):
    tq = min(2 * ROW_TILE, seq_len)
    nq = seq_len // tq
    q0, k0 = tok0 // tq, tok0 // seq_len
    assert tok0 % seq_len == 0
    pairs = 1 if cache is not None else N_HEADS // 2
    in_specs = [pl.BlockSpec((tq, 2 * LANES * pairs), lambda b, hp, i: (q0 + b * nq + i, hp)),
                pl.BlockSpec((seq_len, 2 * LANES * pairs), lambda b, hp, i: (k0 + b, hp)),
                pl.BlockSpec((seq_len, LANES * pairs), lambda b, hp, i: (k0 + b, hp))]
    args = [qh, kh, vh]
    if cache is not None:
        ckv_c, kpe_c, wk, wv = cache
        past = ckv_c.shape[2]
        in_specs += [pl.BlockSpec((1, None, past, MLA_KV_LORA), lambda b, hp, i: (b, l, 0, 0)),
                     pl.BlockSpec((1, None, past, LANES), lambda b, hp, i: (b, l, 0, 0)),
                     pl.BlockSpec((None, MLA_KV_LORA, 2 * LANES), lambda b, hp, i: (l, 0, hp)),
                     pl.BlockSpec((None, MLA_KV_LORA, LANES), lambda b, hp, i: (l, 0, hp))]
        args += [ckv_c, kpe_c, wk, wv]
    return pl.pallas_call(
        functools.partial(_mla_attn_kernel, has_cache=cache is not None),
        out_shape=jax.ShapeDtypeStruct((n_seq * seq_len, GROUP_W), F32),
        grid=(n_seq, N_HEADS // 2 // pairs, nq),
        in_specs=in_specs,
        out_specs=pl.BlockSpec((tq, LANES * pairs), lambda b, hp, i: (b * nq + i, hp)),
        compiler_params=_cparams(("parallel", "parallel", "arbitrary")),
        name="mla_attn_lat" if cache is not None else "mla_attn_ctx",
    )(*args)


def _swa_core(q, k_all, v_all, et_ref, sink_ref, valid_t):
    et = et_ref[...]
    kx = _mm_nt(k_all, et).astype(BF16)
    vxt = _mm_nt(et, v_all).astype(BF16)
    head_l = lax.broadcasted_iota(jnp.int32, q.shape, 1) // HEAD_DIM
    heads = range(N_HEADS)
    st = [_mm_nt(kx, jnp.where(head_l == h, q, 0.0)) for h in heads]
    if valid_t is not None:
        st = [jnp.where(valid_t, x, NEG_INF) for x in st]
    sink = [sink_ref[:, h * HEAD_DIM:h * HEAD_DIM + 1] * LOG2_E for h in heads]
    m = [jnp.maximum(jnp.max(st[h], axis=0, keepdims=True), sink[h]) for h in heads]
    pt = [jnp.exp2(st[h] - m[h]) for h in heads]
    den = [jnp.sum(pt[h], axis=0, keepdims=True) + jnp.exp2(sink[h] - m[h]) for h in heads]
    ot = [jnp.dot(vxt, pt[h].astype(BF16), preferred_element_type=F32) / den[h] for h in heads]
    row_head = lax.broadcasted_iota(jnp.int32, ot[0].shape, 0) // HEAD_DIM
    out_t = ot[N_HEADS - 1]
    for h in range(N_HEADS - 2, -1, -1):
        out_t = jnp.where(row_head == h, ot[h], out_t)
    return out_t.T


def _swa_ctx_kernel(x_ref, e_ref, sink_ref, o_ref):
    x = x_ref[...]
    q = x[:, :GROUP_W] * (SWA_SCALE * LOG2_E)
    k, v = x[:, GROUP_W:GROUP_W + LANES], x[:, GROUP_W + LANES:]
    o_ref[...] = _swa_core(q, k, v, e_ref, sink_ref, None)


def _swa_ctx(geom, swa, e_mat, sink_l, l):
    t = geom.ctx_len
    return pl.pallas_call(
        _swa_ctx_kernel,
        out_shape=jax.ShapeDtypeStruct((geom.n_ctx, GROUP_W), F32),
        grid=(geom.n_ctx_seq,),
        in_specs=[pl.BlockSpec((t, 2 * GROUP_W), lambda b: (b, 0)),
                  pl.BlockSpec(e_mat.shape, lambda b: (0, 0)),
                  _layer_spec(sink_l, l)],
        out_specs=pl.BlockSpec((t, GROUP_W), lambda b: (b, 0)),
        compiler_params=_cparams(("parallel",)),
        name="swa_ctx",
    )(swa, e_mat, sink_l)


def _swa_lat_kernel(xc_ref, xp_ref, xn_ref, cc_ref, ac_ref, bc_ref, cp_ref, ap_ref, bp_ref,
                    cn_ref, an_ref, bn_ref, kc_ref, vc_ref, e_ref, sink_ref, o_ref, *, n_tiles):
    i = pl.program_id(1)
    win = SWA_BLOCK
    tq = xc_ref.shape[0]
    half = HEAD_DIM // 2

    def rope(x, c_ref, a_ref, b_ref):
        reps = x.shape[1] // LANES
        wide = lambda t_ref: jnp.concatenate([t_ref[...]] * reps, axis=1) if reps > 1 else t_ref[...]
        return _rope_slab(x, wide(c_ref), wide(a_ref), wide(b_ref), half)

    ksl, vsl = slice(GROUP_W, GROUP_W + LANES), slice(GROUP_W + LANES, 2 * GROUP_W)
    q = rope(xc_ref[:, :GROUP_W], cc_ref, ac_ref, bc_ref) * (SWA_SCALE * LOG2_E)
    k_all = jnp.concatenate([rope(xp_ref[:, ksl], cp_ref, ap_ref, bp_ref),
                             rope(xc_ref[:, ksl], cc_ref, ac_ref, bc_ref),
                             rope(xn_ref[:, ksl], cn_ref, an_ref, bn_ref),
                             kc_ref[0]], axis=0)
    v_all = jnp.concatenate([xp_ref[:, vsl], xc_ref[:, vsl], xn_ref[:, vsl], vc_ref[0]], axis=0)
    nk = k_all.shape[0]
    n_local = tq + 2 * win
    c = lax.broadcasted_iota(jnp.int32, (nk, tq), 0)
    r = lax.broadcasted_iota(jnp.int32, (nk, tq), 1)
    in_seq = ((c >= win) | (i > 0)) & ((c < win + tq) | (i < n_tiles - 1))
    valid_t = ((c >= r) & (c <= r + 2 * win) & in_seq) | (c >= n_local)
    o_ref[...] = _swa_core(q, k_all, v_all, e_ref, sink_ref, valid_t)


def _swa_lat(geom, swa, cos, sa, sb, k_cache, v_cache, e_mat, sink_l, l):
    win = SWA_BLOCK
    per = SWA_TILE_BLOCKS
    tq = per * win
    n_tiles = geom.lat_len // tq
    nblk = geom.lat_len // win
    t0, b0 = geom.n_ctx // tq, geom.n_ctx // win
    past = k_cache.shape[2]
    prv = lambda i: jnp.maximum(per * i - 1, 0)
    nxt = lambda i: jnp.minimum(per * (i + 1), nblk - 1)
    tok = lambda w: [pl.BlockSpec((tq, w), lambda b, i: (t0 + b * n_tiles + i, 0)),
                     pl.BlockSpec((win, w), lambda b, i: (b0 + b * nblk + prv(i), 0)),
                     pl.BlockSpec((win, w), lambda b, i: (b0 + b * nblk + nxt(i), 0))]
    tab = lambda f, rows: [pl.BlockSpec((rows, LANES), lambda b, i: (f(i), 0))] * 3
    return pl.pallas_call(
        functools.partial(_swa_lat_kernel, n_tiles=n_tiles),
        out_shape=jax.ShapeDtypeStruct((geom.n_lat, GROUP_W), F32),
        grid=(geom.n_lat_seq, n_tiles),
        in_specs=tok(2 * GROUP_W) + tab(lambda i: i, tq) + tab(prv, win) + tab(nxt, win)
        + [pl.BlockSpec((1, None, past, LANES), lambda b, i: (b, l, 0, 0)),
           pl.BlockSpec((1, None, past, LANES), lambda b, i: (b, l, 0, 0)),
           pl.BlockSpec(e_mat.shape, lambda b, i: (0, 0)),
           _layer_spec(sink_l, l)],
        out_specs=pl.BlockSpec((tq, GROUP_W), lambda b, i: (b * n_tiles + i, 0)),
        compiler_params=_cparams(("parallel", "arbitrary")),
        name="swa_lat",
    )(swa, swa, swa, cos, sa, sb, cos, sa, sb, cos, sa, sb, k_cache, v_cache, e_mat, sink_l)


def _mix_ffn_kernel(*refs, geom, n_x, final):
    x_refs, refs = refs[:n_x], refs[n_x:]
    (ada_ref, dof_ref, dob_ref, dz_ref, syf_ref, syb_ref, sz_ref, omc_ref, oml_ref, osc_ref, osl_ref,
     dnw_ref, snw_ref, wo_ref, nw_ref, wgu_ref, wd_ref, fw_ref), out_refs = refs[:18], refs[18:]
    is_ctx = geom.is_ctx_tile(pl.program_id(0))
    x = x_refs[0][...] if n_x == 1 else _read_split(*x_refs, is_ctx)

    row = lax.broadcasted_iota(jnp.int32, (GROUP_W, GROUP_W), 0)
    col = lax.broadcasted_iota(jnp.int32, (GROUP_W, GROUP_W), 1)
    gones = _ones_where((row // HEAD_DIM) == (col // HEAD_DIM), BF16)
    o = dof_ref[...] + dob_ref[...]
    ms = _mm(o * o, gones) * (1.0 / HEAD_DIM)
    dn = o * lax.rsqrt(ms + EPS) * dnw_ref[...] * _silu(dz_ref[...])
    acc = _mm(dn, wo_ref[0:GROUP_W, :])
    acc = acc + _mm(_read_split(omc_ref, oml_ref, is_ctx), wo_ref[GROUP_W:2 * GROUP_W, :])
    y = (syf_ref[...] + syb_ref[...]) * _silu(sz_ref[...])
    for g in range(2):
        sl = slice(g * LANES, (g + 1) * LANES)
        acc = acc + _mm(_rms(y[:, sl], snw_ref[:, sl]),
                        wo_ref[2 * GROUP_W + g * LANES:2 * GROUP_W + (g + 1) * LANES, :])
    acc = acc + _mm(_read_split(osc_ref, osl_ref, is_ctx), wo_ref[3 * GROUP_W:, :])
    x = x + ada_ref[0, 2:3, :] * acc

    h = (_rms(x, nw_ref[...]) * (1.0 + ada_ref[0, 4:5, :]) + ada_ref[0, 3:4, :]).astype(BF16)
    acts = []
    for c in range(FF_DIM // FF_CHUNK):
        g = jnp.dot(h, wgu_ref[:, c * FF_CHUNK:(c + 1) * FF_CHUNK], preferred_element_type=F32)
        u = jnp.dot(h, wgu_ref[:, FF_DIM + c * FF_CHUNK:FF_DIM + (c + 1) * FF_CHUNK],
                    preferred_element_type=F32)
        acts.append((_silu(g) * u).astype(BF16))
    acc = jnp.dot(jnp.concatenate(acts, axis=1), wd_ref[...], preferred_element_type=F32)
    y = x + ada_ref[0, 5:6, :] * acc
    if not final:
        out_refs[0][...] = y
        return
    y = _rms(y, fw_ref[...])

    @pl.when(is_ctx)
    def _():
        out_refs[0][...] = y

    @pl.when(jnp.logical_not(is_ctx))
    def _():
        out_refs[1][...] = y


def _mix_ffn(geom, xs, ada, parts, dnw, snw, w_out, norm_w, wgu, w_down, final_w, l, final):
    tm = ROW_TILE
    d = xs[0].shape[1]
    tok = lambda w: pl.BlockSpec((tm, w), lambda i: (i, 0))
    resident = lambda a: _layer_spec(a, l, pipeline_mode=pl.Buffered(1))
    x_specs = [tok(d)] if len(xs) == 1 else geom.split_specs(d)
    if final:
        out_shape = [jax.ShapeDtypeStruct((geom.n_ctx, d), F32), jax.ShapeDtypeStruct((geom.n_lat, d), F32)]
        out_specs = geom.split_specs(d)
    else:
        out_shape = [jax.ShapeDtypeStruct((geom.n_tok, d), F32)]
        out_specs = [tok(d)]
    return pl.pallas_call(
        functools.partial(_mix_ffn_kernel, geom=geom, n_x=len(xs), final=final),
        out_shape=out_shape,
        grid=(geom.n_tok // tm,),
        in_specs=x_specs + [_ada_spec(geom, ada, l)]
        + [tok(GROUP_W)] * 6 + geom.split_specs(GROUP_W) + geom.split_specs(GROUP_W)
        + [resident(dnw), resident(snw), resident(w_out), resident(norm_w), resident(wgu),
           resident(w_down), pl.BlockSpec(final_w.shape, lambda i: (0, 0))],
        out_specs=out_specs,
        compiler_params=_cparams(("arbitrary",)),
        name="mix_ffn",
    )(*xs, ada, *parts, dnw, snw, w_out, norm_w, wgu, w_down, final_w)


def _w_in_layout(w):
    dn, mla, ssm, swa = 0, 1040, 1456, 2232
    zeros = lambda n: jnp.zeros(w.shape[:-1] + (n,), w.dtype)
    small = jnp.concatenate([
        w[..., dn + 1024:dn + 1040],
        w[..., ssm + 768:ssm + 776],
        zeros(SM_KPE - 24),
        w[..., mla + 384:mla + 416],
        zeros(LANES - SM_KPE - MLA_ROPE)], axis=-1)
    return jnp.concatenate([
        w[..., dn:dn + 768], w[..., dn + 768:dn + 1024],
        w[..., mla:mla + 256], w[..., mla + 256:mla + 384],
        w[..., ssm:ssm + 256], w[..., ssm + 256:ssm + 768],
        w[..., swa:swa + 512], small], axis=-1).astype(BF16)


def _gate_rows(dn_vec, ssm_vec):
    depth = dn_vec.shape[0]
    rows = jnp.zeros((depth, 1, LANES), F32)
    rows = rows.at[:, 0, SM_ALPHA:SM_ALPHA + 8].set(dn_vec.reshape(depth, 8))
    return rows.at[:, 0, SM_DT:SM_DT + 8].set(ssm_vec.reshape(depth, 8))


def _axial_angles(rows, rot_dim):
    row_ids = jnp.broadcast_to(jnp.arange(rows)[:, None], (rows, GRID_W)).reshape(-1).astype(F32)
    col_ids = jnp.broadcast_to(jnp.arange(GRID_W)[None, :], (rows, GRID_W)).reshape(-1).astype(F32)
    n_freq = rot_dim // 4
    inv_freq = ROPE_THETA ** (-jnp.arange(n_freq, dtype=F32) / n_freq)
    return jnp.concatenate([row_ids[:, None] * inv_freq, col_ids[:, None] * inv_freq], axis=-1)


def _rope_tables(ang, lane0, reps, n_ident):
    n, half = ang.shape
    cos, sin = jnp.cos(ang), jnp.sin(ang)
    zeros = jnp.zeros_like(sin)
    period = LANES // reps

    def table(first, second, fill):
        one = jnp.concatenate([jnp.full((n, lane0), fill, F32), first, second,
                               jnp.full((n, period - lane0 - 2 * half), fill, F32)], axis=1)
        tab = jnp.concatenate([one] * reps, axis=1)
        ident = jnp.full((n_ident, LANES), fill, F32)
        return jnp.concatenate([ident, tab], axis=0)

    return table(cos, cos, 1.0), table(-sin, zeros, 0.0), table(zeros, sin, 0.0)


def kernel(x_prompt, x_sample, c, state_dn, cache_mla_ckv, cache_mla_kpe, state_ssm, cache_swa_k,
           cache_swa_v, c_ctx, norm1_w, norm2_w, w_ada, b_ada, w_in, w_out, dn_conv_w, dn_a_log,
           dn_dt_bias, dn_norm_w, mla_q_norm_w, mla_w_uq, mla_kv_norm_w, mla_w_ukv, ssm_conv_w,
           ssm_conv_b, ssm_a_log, ssm_dt_bias, ssm_d, ssm_norm_w, swa_sinks, w_gate_up, w_down,
           final_norm_w):
    batch, seq, d = x_prompt.shape
    dec_batch, dec_seq, _ = x_sample.shape
    depth = w_in.shape[0]
    geom = _Geom(batch, seq, dec_batch, dec_seq)
    n_ctx = geom.n_ctx

    xs = (x_prompt.reshape(n_ctx, d), x_sample.reshape(geom.n_lat, d))
    n_mod = -(-(1 + dec_batch) // SUBLANES) * SUBLANES
    cc = jnp.concatenate([c_ctx[None], c, jnp.zeros((n_mod - 1 - dec_batch, d), F32)], axis=0)
    ada = _ada(cc, w_ada, b_ada).reshape(depth, n_mod, 6, d)

    mla_tabs = _rope_tables(_axial_angles(dec_seq // GRID_W, MLA_ROPE), MLA_NOPE, 1, ROW_TILE)
    swa_tabs = _rope_tables(_axial_angles(dec_seq // GRID_W, HEAD_DIM), 0, 2, 0)

    lane = jnp.arange(GROUP_W)
    e_mat = ((lane // LANES) * HEAD_DIM + lane % HEAD_DIM)[:, None] == jnp.arange(LANES)[None, :]
    e_mat = e_mat.astype(BF16)
    eye_h = jnp.eye(N_HEADS, dtype=F32)
    grp_h = (jnp.arange(N_HEADS)[:, None] // 2 == jnp.arange(2)[None, :]).astype(F32)
    row = lambda a: a.reshape(depth, 1, -1)
    w_pad = _w_in_layout(w_in)
    alog_rows = _gate_rows(dn_a_log, ssm_a_log)
    bias_rows = _gate_rows(dn_dt_bias, ssm_dt_bias)
    s0_dn = state_dn[:, :, :, :, :, None, :] * eye_h[None, None, None, :, None, :, None]
    s0_dn = s0_dn.reshape(dec_batch, depth, 2, GROUP_W, GROUP_W)
    s0_ssm = state_ssm[:, :, :, :, :, None, :] * grp_h[None, None, None, :, None, :, None]
    s0_ssm = s0_ssm.reshape(dec_batch, depth, 2, GROUP_W, 2 * SSM_N)
    ssm_dvec = row(jnp.repeat(ssm_d, HEAD_DIM, axis=-1))
    uq = mla_w_uq.reshape(depth, MLA_Q_LORA, N_HEADS, MLA_NOPE + MLA_ROPE)
    wuq = jnp.pad(uq, ((0, 0), (0, 0), (0, 0), (0, LANES - MLA_NOPE - MLA_ROPE)))
    wuq = wuq.reshape(depth, MLA_Q_LORA, N_HEADS * LANES).astype(BF16)
    ukv = mla_w_ukv.reshape(depth, MLA_KV_LORA, N_HEADS, MLA_NOPE + HEAD_DIM)
    wk = jnp.pad(ukv[..., :MLA_NOPE], ((0, 0), (0, 0), (0, 0), (0, LANES - MLA_NOPE)))
    wk = wk.reshape(depth, MLA_KV_LORA, N_HEADS * LANES).astype(BF16)
    wv = ukv[..., MLA_NOPE:].reshape(depth, MLA_KV_LORA, GROUP_W).astype(BF16)
    kpe_c = jnp.pad(cache_mla_kpe, ((0, 0), (0, 0), (0, 0), (MLA_NOPE, LANES - MLA_NOPE - MLA_ROPE)))
    past = cache_swa_k.shape[2]
    swa_kc = cache_swa_k.reshape(dec_batch, depth, past, LANES)
    swa_vc = cache_swa_v.reshape(dec_batch, depth, past, LANES)
    sinks = row(jnp.repeat(swa_sinks, HEAD_DIM, axis=-1))
    dnw = row(jnp.tile(dn_norm_w, (1, N_HEADS)))
    w_out_b, wgu_b, w_down_b = w_out.astype(BF16), w_gate_up.astype(BF16), w_down.astype(BF16)

    st_dn, st_ckv, st_kpe, st_ssm, st_k, st_v = [], [], [], [], [], []
    for l in range(depth):
        segs = _inproj(geom, xs, ada, row(norm1_w), w_pad, l)
        dn_qkv, dn_z, mla_q, mla_kv, ssm_z, ssm_xbc, swa, small = segs

        dn_of, dn_ob, dn_fin = _deltanet(geom, dn_qkv, small, s0_dn, dn_conv_w, alog_rows, bias_rows, l)
        ssm_yf, ssm_yb, ssm_fin = _ssd(geom, ssm_xbc, small, s0_ssm, ssm_conv_w, row(ssm_conv_b),
                                       alog_rows, bias_rows, ssm_dvec, l)
        qh, kh, vh, ckv = _mla_proj(geom, mla_q, mla_kv, small, *mla_tabs, row(mla_q_norm_w),
                                    row(mla_kv_norm_w), wuq, wk, wv, l)
        o_mla_ctx = _mla_attn(qh, kh, vh, batch, seq, 0)
        o_mla_lat = _mla_attn(qh, kh, vh, dec_batch, dec_seq, n_ctx,
                              cache=(cache_mla_ckv, kpe_c, wk, wv), l=l)
        o_swa_ctx = _swa_ctx(geom, swa, e_mat, sinks, l)
        o_swa_lat = _swa_lat(geom, swa, *swa_tabs, swa_kc, swa_vc, e_mat, sinks, l)

        st_dn.append(dn_fin)
        st_ssm.append(ssm_fin)
        st_ckv.append(ckv.reshape(batch, seq, MLA_KV_LORA))
        st_kpe.append(small[:n_ctx, SM_KPE:SM_KPE + MLA_ROPE].reshape(batch, seq, MLA_ROPE))
        st_k.append(swa[:n_ctx, GROUP_W:GROUP_W + LANES].reshape(batch, seq, 2, HEAD_DIM))
        st_v.append(swa[:n_ctx, GROUP_W + LANES:].reshape(batch, seq, 2, HEAD_DIM))

        parts = (dn_of, dn_ob, dn_z, ssm_yf, ssm_yb, ssm_z, o_mla_ctx, o_mla_lat, o_swa_ctx, o_swa_lat)
        xs = _mix_ffn(geom, xs, ada, parts, dnw, row(ssm_norm_w), w_out_b, row(norm2_w), wgu_b, w_down_b,
                      final_norm_w.reshape(1, d), l, l == depth - 1)

    return (xs[0].reshape(batch, seq, d), xs[1].reshape(dec_batch, dec_seq, d),
            jnp.stack(st_dn, axis=1), jnp.stack(st_ckv, axis=1), jnp.stack(st_kpe, axis=1),
            jnp.stack(st_ssm, axis=1), jnp.stack(st_k, axis=1), jnp.stack(st_v, axis=1))
```
